```python
import jax, jax.numpy as jnp
from jax import lax
import numpy as np

D_MODEL = 1024
BATCH = 16
SEQ = 4096
DEPTH = 2
DEC_BATCH = 16
DEC_SEQ = 64
PAST_LEN = 2048

CHUNK = 64
MLA_HEADS = 8
MLA_Q_RANK = 384
MLA_KV_RANK = 256
MLA_NOPE = 64
MLA_ROPE = 32
MLA_V = 64
MLA_SCALE = (MLA_NOPE + MLA_ROPE) ** -0.5
RET_HEADS = 4
RET_DK = 64
RET_DV = 128
BAND_HEADS = 8
BAND_HD = 64
BAND_PREV_CHUNKS = 8
BAND_WINDOW = BAND_PREV_CHUNKS * CHUNK
MAX_REL = 128
MEM_LEN = 256
MEM_HEADS = 4
MEM_HD = 128
D_FF = 4 * D_MODEL
N_BRANCH = 3
ROPE_THETA = 10000.0
EPS = 1e-6
Q_BLOCK = 128
NEG_INF = -1e30
IN_SIZES = (MLA_Q_RANK, MLA_KV_RANK, MLA_ROPE,
            RET_HEADS * RET_DK, RET_HEADS * RET_DK, RET_HEADS * RET_DV, RET_HEADS * RET_DV,
            BAND_HEADS * BAND_HD, BAND_HEADS * BAND_HD, BAND_HEADS * BAND_HD,
            N_BRANCH * D_MODEL)
IN_WIDTH = sum(IN_SIZES)

kernel_name = 'hybrid_mla_retention_band_stream_step'


def rms_norm(x, g):
    xf = x.astype(jnp.float32)
    y = xf * lax.rsqrt(jnp.mean(xf * xf, axis=-1, keepdims=True) + EPS)
    return (y * g.astype(jnp.float32)).astype(x.dtype)


def rope(x, pos):
    half = x.shape[-1] // 2
    inv = ROPE_THETA ** (-jnp.arange(half, dtype=jnp.float32) / half)
    ang = pos.astype(jnp.float32)[:, None] * inv[None, :]
    cos, sin = jnp.cos(ang)[:, None, :], jnp.sin(ang)[:, None, :]
    xf = x.astype(jnp.float32)
    x1, x2 = xf[..., :half], xf[..., half:]
    return jnp.concatenate([x1 * cos - x2 * sin, x2 * cos + x1 * sin], axis=-1).astype(x.dtype)


def split_cols(z):
    parts, start = [], 0
    for n in IN_SIZES:
        parts.append(z[..., start:start + n])
        start += n
    return parts


def mla_core(q_nope, q_pe, k_nope, k_pe, v, mask):
    s = (jnp.einsum('bqhd,bkhd->bhqk', q_nope, k_nope)
         + jnp.einsum('bqhd,bkd->bhqk', q_pe, k_pe)).astype(jnp.float32) * MLA_SCALE
    if mask is not None:
        s = jnp.where(mask, s, NEG_INF)
    p = jax.nn.softmax(s, axis=-1).astype(v.dtype)
    return jnp.einsum('bhqk,bkhd->bqhd', p, v)


def mla_prompt(q_nope, q_pe, k_nope, k_pe, v):
    B, S, H, _ = q_nope.shape
    nb = S // Q_BLOCK
    blocks = lambda t: t.reshape(B, nb, Q_BLOCK, H, t.shape[-1]).swapaxes(0, 1)
    k_chunk = jnp.arange(S) // CHUNK

    def one_block(args):
        i, qn, qp = args
        q_chunk = (i * Q_BLOCK + jnp.arange(Q_BLOCK)) // CHUNK
        return mla_core(qn, qp, k_nope, k_pe, v, k_chunk[None, :] <= q_chunk[:, None])

    o = lax.map(one_block, (jnp.arange(nb), blocks(q_nope), blocks(q_pe)))
    return o.swapaxes(0, 1).reshape(B, S, H, v.shape[-1])


def ret_log_decay():
    return jnp.log1p(-jnp.exp2(-5.0 - jnp.arange(RET_HEADS, dtype=jnp.float32)))


def retention_chunk(q, k, v, state, log_g):
    L = q.shape[1]
    idx = jnp.arange(L, dtype=jnp.float32)
    diff = idx[:, None] - idx[None, :]
    decay = jnp.where(diff >= 0, jnp.exp(log_g[:, None, None] * jnp.maximum(diff, 0.0)), 0.0)
    s = jnp.einsum('blhd,bmhd->bhlm', q, k) * decay[None]
    inner = jnp.einsum('bhlm,bmhe->blhe', s, v)
    cross = jnp.einsum('blhd,bhde->blhe', q, state) * jnp.exp(log_g[None, :] * (idx[:, None] + 1.0))[None, :, :, None]
    w = jnp.exp(log_g[None, :] * (L - 1.0 - idx[:, None]))
    new_state = state * jnp.exp(log_g * L)[None, :, None, None] + jnp.einsum('blhd,blhe,lh->bhde', k, v, w)
    return inner + cross, new_state


def retention_prompt(q, k, v, log_g):
    B, S, H, dk = q.shape
    nC = S // CHUNK
    to_chunks = lambda t: t.reshape(B, nC, CHUNK, H, t.shape[-1]).swapaxes(0, 1)

    def step(state, xs):
        qc, kc, vc = xs
        o, state = retention_chunk(qc, kc, vc, state, log_g)
        return state, o

    s0 = jnp.zeros((B, H, dk, v.shape[-1]), jnp.float32)
    state, o = lax.scan(step, s0, (to_chunks(q), to_chunks(k), to_chunks(v)))
    return o.swapaxes(0, 1).reshape(B, S, H, v.shape[-1]), state


def head_group_norm(o, g):
    mu = jnp.mean(o, axis=-1, keepdims=True)
    var = jnp.mean(jnp.square(o - mu), axis=-1, keepdims=True)
    y = (o - mu) * lax.rsqrt(var + EPS)
    return y.reshape(o.shape[0], o.shape[1], -1) * g.astype(jnp.float32)


def rel_bias(table, rel):
    return table[:, jnp.clip(rel, -MAX_REL, MAX_REL) + MAX_REL].astype(jnp.float32)


def band_core(q, k, v, bias, valid):
    s = jnp.einsum('bqhd,bkhd->bhqk', q, k).astype(jnp.float32) * (BAND_HD ** -0.5) + bias[None]
    if valid is not None:
        s = jnp.where(valid, s, NEG_INF)
    p = jax.nn.softmax(s, axis=-1).astype(v.dtype)
    return jnp.einsum('bhqk,bkhd->bqhd', p, v)


def band_prompt(q, k, v, table):
    B, S, H, D = q.shape
    nC = S // CHUNK
    span = BAND_WINDOW + CHUNK
    pad = ((0, 0), (BAND_WINDOW, 0), (0, 0), (0, 0))
    kp, vp = jnp.pad(k, pad), jnp.pad(v, pad)
    j = jnp.arange(span)
    bias = rel_bias(table, BAND_WINDOW + jnp.arange(CHUNK)[:, None] - j[None, :])
    qc = q.reshape(B, nC, CHUNK, H, D).swapaxes(0, 1)

    def one_chunk(args):
        c, qb = args
        kb = lax.dynamic_slice_in_dim(kp, c * CHUNK, span, axis=1)
        vb = lax.dynamic_slice_in_dim(vp, c * CHUNK, span, axis=1)
        valid = c * CHUNK - BAND_WINDOW + j >= 0
        return band_core(qb, kb, vb, bias, valid)

    o = lax.map(one_chunk, (jnp.arange(nC), qc))
    return o.swapaxes(0, 1).reshape(B, S, H, D)


def band_sample(q, k, v, cache_k, cache_v, table):
    L, W = q.shape[1], cache_k.shape[1]
    kpos = jnp.concatenate([jnp.arange(W) - W, jnp.arange(L)])
    bias = rel_bias(table, jnp.arange(L)[:, None] - kpos[None, :])
    return band_core(q, jnp.concatenate([cache_k, k], axis=1), jnp.concatenate([cache_v, v], axis=1), bias, None)


def memory_kv(mem, g, w_k, w_v):
    m = rms_norm(mem, g)
    return jnp.einsum('bmd,dhe->bmhe', m, w_k), jnp.einsum('bmd,dhe->bmhe', m, w_v)


def memory_attend(u, mk, mv, w_q, w_o):
    q = jnp.einsum('bsd,dhe->bshe', u, w_q)
    s = jnp.einsum('bshe,bmhe->bhsm', q, mk).astype(jnp.float32) * (MEM_HD ** -0.5)
    p = jax.nn.softmax(s, axis=-1).astype(mv.dtype)
    o = jnp.einsum('bhsm,bmhe->bshe', p, mv)
    return jnp.einsum('bshe,hed->bsd', o, w_o)


def trunk_layer(x, pos, l, P, mem_k, mem_v, past):
    B, S, _ = x.shape
    u = rms_norm(x, P['g_pre_mix'][l])
    z = jnp.einsum('bsd,de->bse', u, P['w_in'][l])
    zq, zkv, zpe, rq, rk, rv, rg, cq, ck, cv, zg = split_cols(z)
    heads = lambda t, h: t.reshape(B, S, h, -1)
    q_a = jnp.einsum('bsr,rhd->bshd', rms_norm(zq, P['g_mla_q'][l]), P['w_mla_uq'][l])
    q_nope, q_pe = q_a[..., :MLA_NOPE], rope(q_a[..., MLA_NOPE:], pos)
    ckv = rms_norm(zkv, P['g_mla_kv'][l])
    kpe = rope(zpe[:, :, None, :], pos)[:, :, 0, :]
    q_b = rope(heads(rq, RET_HEADS), pos).astype(jnp.float32)
    k_b = rope(heads(rk, RET_HEADS), pos).astype(jnp.float32) * (RET_DK ** -0.5)
    v_b = heads(rv, RET_HEADS).astype(jnp.float32)
    log_g = ret_log_decay()
    q_c, k_c, v_c = heads(cq, BAND_HEADS), heads(ck, BAND_HEADS), heads(cv, BAND_HEADS)
    table = P['band_rel_bias'][l]
    if past is None:
        ckv_all, kpe_all = ckv, kpe
    else:
        c_ckv, c_kpe, s_ret, c_bk, c_bv = past
        ckv_all = jnp.concatenate([c_ckv, ckv], axis=1)
        kpe_all = jnp.concatenate([c_kpe, kpe], axis=1)
    kv = jnp.einsum('bkr,rhd->bkhd', ckv_all, P['w_mla_ukv'][l])
    k_nope, v_a = kv[..., :MLA_NOPE], kv[..., MLA_NOPE:]
    if past is None:
        o_a = mla_prompt(q_nope, q_pe, k_nope, kpe_all, v_a)
        o_b, ret_state = retention_prompt(q_b, k_b, v_b, log_g)
        o_c = band_prompt(q_c, k_c, v_c, table)
        new = (ckv, kpe, ret_state.astype(x.dtype), k_c[:, -BAND_WINDOW:], v_c[:, -BAND_WINDOW:])
    else:
        o_a = mla_core(q_nope, q_pe, k_nope, kpe_all, v_a, None)
        o_b, ret_state = retention_chunk(q_b, k_b, v_b, s_ret.astype(jnp.float32), log_g)
        o_c = band_sample(q_c, k_c, v_c, c_bk, c_bv, table)
        new = (ckv, kpe, ret_state.astype(s_ret.dtype), k_c, v_c)
    y_b = (head_group_norm(o_b, P['g_ret_gn'][l]) * jax.nn.silu(rg.astype(jnp.float32))).astype(x.dtype)
    br_a = jnp.einsum('bse,ed->bsd', o_a.reshape(B, S, -1), P['w_br_a'][l])
    br_b = jnp.einsum('bse,ed->bsd', y_b, P['w_br_b'][l])
    br_c = jnp.einsum('bse,ed->bsd', o_c.reshape(B, S, -1), P['w_br_c'][l])
    gates = jax.nn.sigmoid((zg + P['b_gate'][l]).astype(jnp.float32)).astype(x.dtype)
    gates = gates.reshape(B, S, N_BRANCH, D_MODEL)
    merged = gates[:, :, 0] * br_a + gates[:, :, 1] * br_b + gates[:, :, 2] * br_c
    x = x + rms_norm(jnp.einsum('bsd,de->bse', merged, P['w_out'][l]), P['g_post_mix'][l])
    u = rms_norm(x, P['g_pre_mem'][l])
    x = x + rms_norm(memory_attend(u, mem_k, mem_v, P['w_mem_q'][l], P['w_mem_o'][l]), P['g_post_mem'][l])
    u = rms_norm(x, P['g_pre_ff'][l])
    h = jnp.square(jax.nn.relu(jnp.einsum('bsd,df->bsf', u, P['w_up'][l])))
    x = x + rms_norm(jnp.einsum('bsf,fd->bsd', h, P['w_down'][l]), P['g_post_ff'][l])
    return x, new


def setup_inputs(seed: int = 0) -> dict:
    key = jax.random.key(seed)
    ks = iter(jax.random.split(key, 40))
    nrm = lambda shape, scale: jax.random.normal(next(ks), shape, jnp.float32) * scale
    gain = lambda shape: 1.0 + nrm(shape, 0.05)
    bw = min(BAND_WINDOW, PAST_LEN)
    return {
        'x_prompt': nrm((BATCH, SEQ, D_MODEL), 1.0),
        'x_sample': nrm((DEC_BATCH, DEC_SEQ, D_MODEL), 1.0),
        'cache_mla_ckv': nrm((DEPTH, DEC_BATCH, PAST_LEN, MLA_KV_RANK), 1.0),
        'cache_mla_kpe': nrm((DEPTH, DEC_BATCH, PAST_LEN, MLA_ROPE), 1.0),
        'state_ret': nrm((DEPTH, DEC_BATCH, RET_HEADS, RET_DK, RET_DV), 1.0),
        'cache_band_k': nrm((DEPTH, DEC_BATCH, bw, BAND_HEADS, BAND_HD), 1.0),
        'cache_band_v': nrm((DEPTH, DEC_BATCH, bw, BAND_HEADS, BAND_HD), 1.0),
        'cache_mem_k': nrm((DEPTH, DEC_BATCH, MEM_LEN, MEM_HEADS, MEM_HD), 1.0),
        'cache_mem_v': nrm((DEPTH, DEC_BATCH, MEM_LEN, MEM_HEADS, MEM_HD), 1.0),
        'mem_prompt': nrm((BATCH, MEM_LEN, D_MODEL), 1.0),
        'g_pre_mix': gain((DEPTH, D_MODEL)),
        'w_in': nrm((DEPTH, D_MODEL, IN_WIDTH), D_MODEL ** -0.5),
        'g_mla_q': gain((DEPTH, MLA_Q_RANK)),
        'w_mla_uq': nrm((DEPTH, MLA_Q_RANK, MLA_HEADS, MLA_NOPE + MLA_ROPE), MLA_Q_RANK ** -0.5),
        'g_mla_kv': gain((DEPTH, MLA_KV_RANK)),
        'w_mla_ukv': nrm((DEPTH, MLA_KV_RANK, MLA_HEADS, MLA_NOPE + MLA_V), MLA_KV_RANK ** -0.5),
        'g_ret_gn': gain((DEPTH, RET_HEADS * RET_DV)),
        'band_rel_bias': nrm((DEPTH, BAND_HEADS, 2 * MAX_REL + 1), 0.2),
        'w_br_a': nrm((DEPTH, MLA_HEADS * MLA_V, D_MODEL), (MLA_HEADS * MLA_V) ** -0.5),
        'w_br_b': nrm((DEPTH, RET_HEADS * RET_DV, D_MODEL), (RET_HEADS * RET_DV) ** -0.5),
        'w_br_c': nrm((DEPTH, BAND_HEADS * BAND_HD, D_MODEL), (BAND_HEADS * BAND_HD) ** -0.5),
        'b_gate': nrm((DEPTH, N_BRANCH * D_MODEL), 0.02),
        'w_out': nrm((DEPTH, D_MODEL, D_MODEL), D_MODEL ** -0.5),
        'g_post_mix': gain((DEPTH, D_MODEL)),
        'g_pre_mem': gain((DEPTH, D_MODEL)),
        'g_mem': gain((DEPTH, D_MODEL)),
        'w_mem_q': nrm((DEPTH, D_MODEL, MEM_HEADS, MEM_HD), D_MODEL ** -0.5),
        'w_mem_k': nrm((DEPTH, D_MODEL, MEM_HEADS, MEM_HD), D_MODEL ** -0.5),
        'w_mem_v': nrm((DEPTH, D_MODEL, MEM_HEADS, MEM_HD), D_MODEL ** -0.5),
        'w_mem_o': nrm((DEPTH, MEM_HEADS, MEM_HD, D_MODEL), (MEM_HEADS * MEM_HD) ** -0.5),
        'g_post_mem': gain((DEPTH, D_MODEL)),
        'g_pre_ff': gain((DEPTH, D_MODEL)),
        'w_up': nrm((DEPTH, D_MODEL, D_FF), D_MODEL ** -0.5),
        'w_down': nrm((DEPTH, D_FF, D_MODEL), D_FF ** -0.5),
        'g_post_ff': gain((DEPTH, D_MODEL)),
    }


def reference(x_prompt, x_sample, cache_mla_ckv, cache_mla_kpe, state_ret, cache_band_k, cache_band_v,
              cache_mem_k, cache_mem_v, mem_prompt, g_pre_mix, w_in, g_mla_q, w_mla_uq, g_mla_kv, w_mla_ukv,
              g_ret_gn, band_rel_bias, w_br_a, w_br_b, w_br_c, b_gate, w_out, g_post_mix, g_pre_mem, g_mem,
              w_mem_q, w_mem_k, w_mem_v, w_mem_o, g_post_mem, g_pre_ff, w_up, w_down, g_post_ff):
    P = dict(g_pre_mix=g_pre_mix, w_in=w_in, g_mla_q=g_mla_q, w_mla_uq=w_mla_uq, g_mla_kv=g_mla_kv,
             w_mla_ukv=w_mla_ukv, g_ret_gn=g_ret_gn, band_rel_bias=band_rel_bias, w_br_a=w_br_a,
             w_br_b=w_br_b, w_br_c=w_br_c, b_gate=b_gate, w_out=w_out, g_post_mix=g_post_mix,
             g_pre_mem=g_pre_mem, w_mem_q=w_mem_q, w_mem_o=w_mem_o, g_post_mem=g_post_mem,
             g_pre_ff=g_pre_ff, w_up=w_up, w_down=w_down, g_post_ff=g_post_ff)
    pos_p = jnp.arange(x_prompt.shape[1])
    pos_s = cache_mla_ckv.shape[2] + jnp.arange(x_sample.shape[1])
    xp, xs = x_prompt, x_sample
    new_p = [[] for _ in range(7)]
    new_s = [[] for _ in range(5)]
    for l in range(DEPTH):
        mk, mv = memory_kv(mem_prompt, g_mem[l], w_mem_k[l], w_mem_v[l])
        xp, st_p = trunk_layer(xp, pos_p, l, P, mk, mv, None)
        xs, st_s = trunk_layer(xs, pos_s, l, P, cache_mem_k[l], cache_mem_v[l],
                               (cache_mla_ckv[l], cache_mla_kpe[l], state_ret[l], cache_band_k[l], cache_band_v[l]))
        for acc, t in zip(new_p, st_p + (mk, mv)):
            acc.append(t)
        for acc, t in zip(new_s, st_s):
            acc.append(t)
    stack = lambda ts: jnp.stack(ts, axis=0)
    return (xp, xs,
            stack(new_p[0]), stack(new_p[1]), stack(new_p[2]), stack(new_p[3]), stack(new_p[4]),
            stack(new_p[5]), stack(new_p[6]),
            stack(new_s[0]), stack(new_s[1]), stack(new_s[2]), stack(new_s[3]), stack(new_s[4]))
```

```python
import functools

import jax
import jax.numpy as jnp
from jax import lax
from jax.experimental import pallas as pl
from jax.experimental.pallas import tpu as pltpu

F32 = jnp.float32
BF16 = jnp.bfloat16

CHUNK = 64
MLA_HEADS = 8
MLA_Q_RANK = 384
MLA_KV_RANK = 256
MLA_NOPE = 64
MLA_ROPE = 32
MLA_V = 64
MLA_SCALE = (MLA_NOPE + MLA_ROPE) ** -0.5
RET_HEADS = 4
RET_DK = 64
RET_DV = 128
BAND_HEADS = 8
BAND_HD = 64
BAND_PREV_CHUNKS = 8
BAND_WINDOW = BAND_PREV_CHUNKS * CHUNK
MAX_REL = 128
MEM_HEADS = 4
MEM_HD = 128
ROPE_THETA = 10000.0
EPS = 1e-6
NEG_INF = -1e30

LANE = 128
HEAD_SLAB = 128
VMEM_LIMIT = 48 * 1024 * 1024

Z_G = 0
Z_CQ = 3072
Z_CK = 3584
Z_CV = 4096
Z_RV = 4608
Z_RG = 5120
Z_RQ = 5632
Z_RK = 5888
Z_KV = 6144
Z_PE = 6400
Z_Q = 6528
Z_WIDTH = 6912


def _cparams(*sem):
    return pltpu.CompilerParams(dimension_semantics=sem, vmem_limit_bytes=VMEM_LIMIT)


def _rms(x, g):
    return x * lax.rsqrt(jnp.mean(x * x, axis=-1, keepdims=True) + EPS) * g


def _dot(a, b):
    return jnp.dot(a, b, preferred_element_type=F32)


def _dot_nt(a, b):
    return lax.dot_general(a, b, (((1,), (1,)), ((), ())), preferred_element_type=F32)


def _dot_tn(a, b):
    return lax.dot_general(a, b, (((0,), (0,)), ((), ())), preferred_element_type=F32)


def _sigmoid(x):
    return 1.0 / (1.0 + jnp.exp(-x))


def _lane_iota(shape):
    return lax.broadcasted_iota(jnp.int32, shape, len(shape) - 1)


def _norm_matmul_kernel(x_ref, g_ref, w_ref, o_ref, xn_ref):
    @pl.when(pl.program_id(1) == 0)
    def _():
        xn_ref[...] = _rms(x_ref[...].astype(F32), g_ref[...]).astype(BF16)

    o_ref[...] = _dot(xn_ref[...], w_ref[...]).astype(o_ref.dtype)


def norm_matmul(x, g, w, out_dtype, tm, tn, name):
    m, k = x.shape
    n = w.shape[1]
    assert m % tm == 0 and n % tn == 0, (m, tm, n, tn)
    return pl.pallas_call(
        _norm_matmul_kernel,
        grid=(m // tm, n // tn),
        in_specs=[
            pl.BlockSpec((tm, k), lambda i, j: (i, 0)),
            pl.BlockSpec((1, k), lambda i, j: (0, 0)),
            pl.BlockSpec((k, tn), lambda i, j: (0, j)),
        ],
        out_specs=pl.BlockSpec((tm, tn), lambda i, j: (i, j)),
        out_shape=jax.ShapeDtypeStruct((m, n), out_dtype),
        scratch_shapes=[pltpu.VMEM((tm, k), BF16)],
        compiler_params=_cparams("parallel", "arbitrary"),
        name=name,
    )(x, g.reshape(1, k), w)


def _rope_slab(x, cos, sin, rot):
    first_end, half, period = rot
    width = x.shape[1]
    right = pltpu.roll(x, width - half, 1)
    left = pltpu.roll(x, half, 1)
    partner = jnp.where((_lane_iota(x.shape) & (period - 1)) < first_end, right, left)
    return x * cos + partner * sin


_MLA_ROT = (MLA_NOPE + MLA_ROPE // 2, MLA_ROPE // 2, LANE)
_RET_ROT = (RET_DK // 2, RET_DK // 2, RET_DK)


def _mla_q_kernel(zq_ref, zkv_ref, zpe_ref, gq_ref, gkv_ref, wq_ref, cos_ref, sin_ref,
                  q_ref, ckv_ref, kpe_ref):
    cos = cos_ref[...]
    sin = sin_ref[...]
    qn = _rms(zq_ref[...].astype(F32), gq_ref[...]).astype(BF16)
    q = _dot(qn, wq_ref[...])
    for h in range(MLA_HEADS):
        sl = slice(h * HEAD_SLAB, (h + 1) * HEAD_SLAB)
        q_ref[:, sl] = (_rope_slab(q[:, sl], cos, sin, _MLA_ROT) * MLA_SCALE).astype(BF16)
    ckv_ref[...] = _rms(zkv_ref[...].astype(F32), gkv_ref[...])
    kpe_ref[...] = _rope_slab(zpe_ref[...].astype(F32), cos, sin, _MLA_ROT)


def mla_q_prep(z, g_q, g_kv, w_uq, cos, sin, tm, name):
    m = z.shape[0]
    nt = cos.shape[0] // tm
    row = lambda w: pl.BlockSpec((1, w), lambda i: (0, 0))
    return pl.pallas_call(
        _mla_q_kernel,
        grid=(m // tm,),
        in_specs=[
            pl.BlockSpec((tm, MLA_Q_RANK), lambda i: (i, Z_Q // MLA_Q_RANK)),
            pl.BlockSpec((tm, MLA_KV_RANK), lambda i: (i, Z_KV // MLA_KV_RANK)),
            pl.BlockSpec((tm, HEAD_SLAB), lambda i: (i, Z_PE // HEAD_SLAB)),
            row(MLA_Q_RANK), row(MLA_KV_RANK),
            pl.BlockSpec(w_uq.shape, lambda i: (0, 0)),
            pl.BlockSpec((tm, HEAD_SLAB), lambda i: (i % nt, 0)),
            pl.BlockSpec((tm, HEAD_SLAB), lambda i: (i % nt, 0)),
        ],
        out_specs=[
            pl.BlockSpec((tm, MLA_HEADS * HEAD_SLAB), lambda i: (i, 0)),
            pl.BlockSpec((tm, MLA_KV_RANK), lambda i: (i, 0)),
            pl.BlockSpec((tm, HEAD_SLAB), lambda i: (i, 0)),
        ],
        out_shape=[
            jax.ShapeDtypeStruct((m, MLA_HEADS * HEAD_SLAB), BF16),
            jax.ShapeDtypeStruct((m, MLA_KV_RANK), F32),
            jax.ShapeDtypeStruct((m, HEAD_SLAB), F32),
        ],
        compiler_params=_cparams("parallel"),
        name=name,
    )(z, z, z, g_q.reshape(1, -1), g_kv.reshape(1, -1), w_uq, cos, sin)


def _mla_kv_kernel(ckv_ref, kpe_ref, wk_ref, wv_ref, k_ref, v_ref):
    c = ckv_ref[...].astype(BF16)
    kn = _dot(c, wk_ref[...])
    kpe = kpe_ref[...]
    for h in range(MLA_HEADS):
        sl = slice(h * HEAD_SLAB, (h + 1) * HEAD_SLAB)
        k_ref[:, sl] = (kn[:, sl] + kpe).astype(BF16)
    v_ref[...] = _dot(c, wv_ref[...]).astype(BF16)


def mla_kv_up(ckv, kpe, w_uk, w_uv, tm, name):
    m = ckv.shape[0]
    return pl.pallas_call(
        _mla_kv_kernel,
        grid=(m // tm,),
        in_specs=[
            pl.BlockSpec((tm, MLA_KV_RANK), lambda i: (i, 0)),
            pl.BlockSpec((tm, HEAD_SLAB), lambda i: (i, 0)),
            pl.BlockSpec(w_uk.shape, lambda i: (0, 0)),
            pl.BlockSpec(w_uv.shape, lambda i: (0, 0)),
        ],
        out_specs=[
            pl.BlockSpec((tm, MLA_HEADS * HEAD_SLAB), lambda i: (i, 0)),
            pl.BlockSpec((tm, MLA_HEADS * MLA_V), lambda i: (i, 0)),
        ],
        out_shape=[
            jax.ShapeDtypeStruct((m, MLA_HEADS * HEAD_SLAB), BF16),
            jax.ShapeDtypeStruct((m, MLA_HEADS * MLA_V), BF16),
        ],
        compiler_params=_cparams("parallel"),
        name=name,
    )(ckv, kpe, w_uk, w_uv)


def _head_of_pair(x, hh):
    lane = _lane_iota(x.shape)
    keep = (lane < BAND_HD) if hh == 0 else (lane >= BAND_HD)
    return jnp.where(keep, x, jnp.zeros_like(x))


def _pick_pair(o0, o1):
    return jnp.where(_lane_iota(o0.shape) < BAND_HD, o0, o1)


def _mla_attn_kernel(q_ref, k_ref, v_ref, o_ref, acc_ref, *, tq, tk, causal, n_kblocks, nk_valid):
    qi = pl.program_id(2)
    qs = [q_ref[0, :, hh * HEAD_SLAB:(hh + 1) * HEAD_SLAB] for hh in range(2)]

    def step(kb, ms, ls, mask):
        rows = pl.ds(kb * tk if isinstance(kb, int) else pl.multiple_of(kb * tk, tk), tk)
        v = v_ref[0, rows, :]
        new_m, new_l = [], []
        for hh in range(2):
            k = k_ref[0, rows, hh * HEAD_SLAB:(hh + 1) * HEAD_SLAB]
            s = _dot_nt(qs[hh], k)
            if mask is not None:
                s = jnp.where(mask, s, NEG_INF)
            m = jnp.maximum(ms[hh], jnp.max(s, axis=-1, keepdims=True))
            alpha = jnp.exp(ms[hh] - m)
            p = jnp.exp(s - m)
            new_l.append(alpha * ls[hh] + jnp.sum(p, axis=-1, keepdims=True))
            acc_ref[hh] = alpha * acc_ref[hh] + _dot(p.astype(BF16), v)
            new_m.append(m)
        return tuple(new_m), tuple(new_l)

    acc_ref[...] = jnp.zeros_like(acc_ref)
    ms = tuple(jnp.full((tq, 1), NEG_INF, F32) for _ in range(2))
    ls = tuple(jnp.zeros((tq, 1), F32) for _ in range(2))
    if causal:
        ms, ls = lax.fori_loop(0, qi, lambda kb, c: step(kb, c[0], c[1], None), (ms, ls))
        shift = CHUNK.bit_length() - 1
        r = jnp.right_shift(lax.broadcasted_iota(jnp.int32, (tq, tk), 0), shift)
        c = jnp.right_shift(lax.broadcasted_iota(jnp.int32, (tq, tk), 1), shift)
        ms, ls = step(qi, ms, ls, c <= r)
    else:
        for kb in range(n_kblocks):
            mask = None
            if (kb + 1) * tk > nk_valid:
                mask = (kb * tk + lax.broadcasted_iota(jnp.int32, (tq, tk), 1)) < nk_valid
            ms, ls = step(kb, ms, ls, mask)
    o_ref[0] = _pick_pair(acc_ref[0] / ls[0], acc_ref[1] / ls[1]).astype(o_ref.dtype)


def mla_attention(q, k, v, tq, tk, causal, nk_valid, name):
    b, sq, _ = q.shape
    sk = k.shape[1]
    assert sq % tq == 0 and sk % tk == 0 and (not causal or (tq == tk and sq == sk))
    kern = functools.partial(_mla_attn_kernel, tq=tq, tk=tk, causal=causal,
                             n_kblocks=sk // tk, nk_valid=nk_valid)
    return pl.pallas_call(
        kern,
        grid=(b, MLA_HEADS // 2, sq // tq),
        in_specs=[
            pl.BlockSpec((1, tq, 2 * HEAD_SLAB), lambda b_, h, i: (b_, i, h)),
            pl.BlockSpec((1, sk, 2 * HEAD_SLAB), lambda b_, h, i: (b_, 0, h)),
            pl.BlockSpec((1, sk, 2 * MLA_V), lambda b_, h, i: (b_, 0, h)),
        ],
        out_specs=pl.BlockSpec((1, tq, 2 * MLA_V), lambda b_, h, i: (b_, i, h)),
        out_shape=jax.ShapeDtypeStruct((b, sq, MLA_HEADS * MLA_V), BF16),
        scratch_shapes=[pltpu.VMEM((2, tq, 2 * MLA_V), F32)],
        compiler_params=_cparams("parallel", "parallel", "arbitrary"),
        name=name,
    )(q, k, v)


def _band_attn_kernel(*refs, n_kv):
    q_ref = refs[0]
    k_refs = refs[1:1 + n_kv]
    v_refs = refs[1 + n_kv:1 + 2 * n_kv]
    bias_ref, o_ref = refs[1 + 2 * n_kv], refs[2 + 2 * n_kv]
    q = q_ref[0] * (BAND_HD ** -0.5)
    k = jnp.concatenate([r[0] for r in k_refs], axis=0) if n_kv > 1 else k_refs[0][0]
    v = jnp.concatenate([r[0] for r in v_refs], axis=0) if n_kv > 1 else v_refs[0][0]
    outs = []
    for hh in range(2):
        s = _dot_nt(_head_of_pair(q, hh), k) + bias_ref[0, hh]
        m = jnp.max(s, axis=-1, keepdims=True)
        p = jnp.exp(s - m)
        l = jnp.sum(p, axis=-1, keepdims=True)
        outs.append(_dot(p.astype(BF16), v) / l)
    o_ref[0] = _pick_pair(outs[0], outs[1]).astype(o_ref.dtype)


def band_attention(q, q_col, ks, k_col, vs, v_col, bias, tq, name):
    b, sq = q.shape[:2]
    n_kv = len(ks)
    nq = sq // tq
    n_var = bias.shape[0]
    span = bias.shape[-1]
    if n_kv == 1:
        kv_rows = ks[0].shape[1]
        kmaps = [lambda hp, i, b_: (b_, 0, k_col + hp)]
        vmaps = [lambda hp, i, b_: (b_, 0, v_col + hp)]
    else:
        kv_rows = tq
        back = lambda d: (lambda hp, i, b_: (b_, jnp.maximum(i - d, 0), k_col + hp))
        backv = lambda d: (lambda hp, i, b_: (b_, jnp.maximum(i - d, 0), v_col + hp))
        kmaps = [back(2), back(1), back(0)]
        vmaps = [backv(2), backv(1), backv(0)]
    assert kv_rows * n_kv == span
    return pl.pallas_call(
        functools.partial(_band_attn_kernel, n_kv=n_kv),
        grid=(BAND_HEADS // 2, nq, b),
        in_specs=(
            [pl.BlockSpec((1, tq, LANE), lambda hp, i, b_: (b_, i, q_col + hp))]
            + [pl.BlockSpec((1, kv_rows, LANE), mp) for mp in kmaps]
            + [pl.BlockSpec((1, kv_rows, LANE), mp) for mp in vmaps]
            + [pl.BlockSpec((1, 2, tq, span),
                            lambda hp, i, b_: (jnp.minimum(i, n_var - 1), hp, 0, 0))]
        ),
        out_specs=pl.BlockSpec((1, tq, LANE), lambda hp, i, b_: (b_, i, hp)),
        out_shape=jax.ShapeDtypeStruct((b, sq, BAND_HEADS * BAND_HD), BF16),
        compiler_params=_cparams("parallel", "parallel", "arbitrary"),
        name=name,
    )(q, *ks, *vs, bias)


def _retention_kernel(rq_ref, rk_ref, rv_ref, rg_ref, cq_ref, sq_ref, ck_ref, sk_ref,
                      dmat_ref, rowdec_ref, kw_ref, sdec_ref, gn_ref, init_ref,
                      y_ref, state_out_ref, state_ref):
    c = pl.program_id(1)

    @pl.when(c == 0)
    def _():
        state_ref[...] = init_ref[0]

    n_pair = RET_HEADS // 2
    q = _rope_slab(rq_ref[0].astype(F32), cq_ref[...], sq_ref[...], _RET_ROT)
    k = _rope_slab(rk_ref[0].astype(F32), ck_ref[...], sk_ref[...], _RET_ROT)
    qb = q.astype(BF16)
    kb = k.astype(BF16)
    kwb = (k * kw_ref[...]).astype(BF16)
    gn = gn_ref[...]
    row_is_first = lax.broadcasted_iota(jnp.int32, (LANE, RET_DV), 0) < RET_DK
    for p in range(n_pair):
        psl = slice(p * LANE, (p + 1) * LANE)
        st = state_ref[p]
        stb = st.astype(BF16)
        kv = []
        for hh in range(2):
            h = 2 * p + hh
            vsl = slice(h * RET_DV, (h + 1) * RET_DV)
            v = rv_ref[0, :, vsl]
            qh = _head_of_pair(qb[:, psl], hh)
            s = _dot_nt(qh, kb[:, psl]) * dmat_ref[h]
            o = _dot(s.astype(BF16), v) + _dot(qh, stb) * rowdec_ref[:, vsl]
            mu = jnp.mean(o, axis=-1, keepdims=True)
            d = o - mu
            yn = d * lax.rsqrt(jnp.mean(d * d, axis=-1, keepdims=True) + EPS)
            g = rg_ref[0, :, vsl].astype(F32)
            y_ref[0, :, vsl] = (yn * gn[:, vsl] * (g * _sigmoid(g))).astype(y_ref.dtype)
            kv.append(_dot_tn(kwb[:, psl], v))
        state_ref[p] = st * sdec_ref[p] + jnp.where(row_is_first, kv[0], kv[1])

    @pl.when(c == pl.num_programs(1) - 1)
    def _():
        state_out_ref[0] = state_ref[...]


def retention(z, tabs, g_gn, init_state, blk, name):
    b, s = z.shape[:2]
    nc = s // blk
    n_pair = RET_HEADS // 2
    zblk = lambda w, col: pl.BlockSpec((1, blk, w), lambda b_, c: (b_, c, col // w))
    tab = lambda w: pl.BlockSpec((blk, w), lambda b_, c: (c, 0))
    const = lambda a: pl.BlockSpec(a.shape, lambda b_, c: (0,) * a.ndim)
    qk_w = RET_HEADS * RET_DK
    v_w = RET_HEADS * RET_DV
    return pl.pallas_call(
        _retention_kernel,
        grid=(b, nc),
        in_specs=[
            zblk(qk_w, Z_RQ), zblk(qk_w, Z_RK), zblk(v_w, Z_RV), zblk(v_w, Z_RG),
            tab(qk_w), tab(qk_w), tab(qk_w), tab(qk_w),
            const(tabs["dmat"]), const(tabs["rowdec"]), const(tabs["kw"]), const(tabs["sdec"]),
            pl.BlockSpec((1, v_w), lambda b_, c: (0, 0)),
            pl.BlockSpec((1, n_pair, LANE, RET_DV), lambda b_, c: (b_, 0, 0, 0)),
        ],
        out_specs=[
            pl.BlockSpec((1, blk, v_w), lambda b_, c: (b_, c, 0)),
            pl.BlockSpec((1, n_pair, LANE, RET_DV), lambda b_, c: (b_, 0, 0, 0)),
        ],
        out_shape=[
            jax.ShapeDtypeStruct((b, s, v_w), BF16),
            jax.ShapeDtypeStruct((b, n_pair, LANE, RET_DV), F32),
        ],
        scratch_shapes=[pltpu.VMEM((n_pair, LANE, RET_DV), F32)],
        compiler_params=_cparams("parallel", "arbitrary"),
        name=name,
    )(z, z, z, z, tabs["cos_q"], tabs["sin_q"], tabs["cos_k"], tabs["sin_k"],
      tabs["dmat"], tabs["rowdec"], tabs["kw"], tabs["sdec"], g_gn.reshape(1, -1), init_state)


def _mix_out_kernel(oa_ref, yb_ref, oc_ref, zg0_ref, zg1_ref, zg2_ref, bg_ref, wa_ref, wb_ref,
                    wc_ref, wo_ref, g_ref, x_ref, o_ref):
    d = x_ref.shape[-1]
    merged = None
    for n, (br_ref, w_ref, zg_ref) in enumerate(
            ((oa_ref, wa_ref, zg0_ref), (yb_ref, wb_ref, zg1_ref), (oc_ref, wc_ref, zg2_ref))):
        gate = _sigmoid(zg_ref[...].astype(F32) + bg_ref[:, n * d:(n + 1) * d])
        term = gate * _dot(br_ref[...], w_ref[...])
        merged = term if merged is None else merged + term
    y = _dot(merged.astype(BF16), wo_ref[...])
    o_ref[...] = x_ref[...] + _rms(y, g_ref[...])


def mix_out(o_a, y_b, o_c, z, b_gate, w_a, w_b, w_c, w_out, g_post, x, tm, name):
    m, d = x.shape
    e = o_a.shape[1]
    act = pl.BlockSpec((tm, e), lambda i: (i, 0))
    zg = lambda n: pl.BlockSpec((tm, d), lambda i: (i, Z_G // d + n))
    const = lambda a: pl.BlockSpec(a.shape, lambda i: (0, 0))
    return pl.pallas_call(
        _mix_out_kernel,
        grid=(m // tm,),
        in_specs=[act, act, act, zg(0), zg(1), zg(2),
                  pl.BlockSpec((1, 3 * d), lambda i: (0, 0)),
                  const(w_a), const(w_b), const(w_c), const(w_out),
                  pl.BlockSpec((1, d), lambda i: (0, 0)),
                  pl.BlockSpec((tm, d), lambda i: (i, 0))],
        out_specs=pl.BlockSpec((tm, d), lambda i: (i, 0)),
        out_shape=jax.ShapeDtypeStruct((m, d), F32),
        compiler_params=_cparams("parallel"),
        name=name,
    )(o_a, y_b, o_c, z, z, z, b_gate.reshape(1, -1), w_a, w_b, w_c, w_out,
      g_post.reshape(1, -1), x)


def _mem_attn_kernel(x_ref, mk_ref, mv_ref, gpre_ref, wq_ref, wo_ref, gpost_ref, o_ref):
    x = x_ref[0]
    u = _rms(x, gpre_ref[...]).astype(BF16)
    q = _dot(u, wq_ref[...]).astype(BF16)
    outs = []
    for h in range(MEM_HEADS):
        sl = slice(h * MEM_HD, (h + 1) * MEM_HD)
        s = _dot_nt(q[:, sl], mk_ref[0, :, sl].astype(BF16)) * (MEM_HD ** -0.5)
        m = jnp.max(s, axis=-1, keepdims=True)
        p = jnp.exp(s - m)
        l = jnp.sum(p, axis=-1, keepdims=True)
        outs.append(_dot(p.astype(BF16), mv_ref[0, :, sl].astype(BF16)) / l)
    o = jnp.concatenate(outs, axis=-1).astype(BF16)
    o_ref[0] = x + _rms(_dot(o, wo_ref[...]), gpost_ref[...])


def mem_attention(x, mk, mv, g_pre, w_q, w_o, g_post, tm, name):
    b, s, d = x.shape
    const = lambda a: pl.BlockSpec(a.shape, lambda b_, i: (0, 0))
    vec = pl.BlockSpec((1, d), lambda b_, i: (0, 0))
    mem = pl.BlockSpec((1,) + mk.shape[1:], lambda b_, i: (b_, 0, 0))
    return pl.pallas_call(
        _mem_attn_kernel,
        grid=(b, s // tm),
        in_specs=[pl.BlockSpec((1, tm, d), lambda b_, i: (b_, i, 0)), mem, mem,
                  vec, const(w_q), const(w_o), vec],
        out_specs=pl.BlockSpec((1, tm, d), lambda b_, i: (b_, i, 0)),
        out_shape=jax.ShapeDtypeStruct((b, s, d), F32),
        compiler_params=_cparams("parallel", "parallel"),
        name=name,
    )(x, mk, mv, g_pre.reshape(1, -1), w_q, w_o, g_post.reshape(1, -1))


def _mlp_kernel(x_ref, gpre_ref, wu_ref, wd_ref, gpost_ref, o_ref, xn_ref, acc_ref):
    j = pl.program_id(1)

    @pl.when(j == 0)
    def _():
        xn_ref[...] = _rms(x_ref[...], gpre_ref[...]).astype(BF16)
        acc_ref[...] = jnp.zeros_like(acc_ref)

    h = jnp.square(jnp.maximum(_dot(xn_ref[...], wu_ref[...]), 0.0))
    acc_ref[...] += _dot(h.astype(BF16), wd_ref[...])

    @pl.when(j == pl.num_programs(1) - 1)
    def _():
        o_ref[...] = x_ref[...] + _rms(acc_ref[...], gpost_ref[...])


def mlp(x, g_pre, w_up, w_down, g_post, tm, tf, name):
    m, d = x.shape
    f = w_up.shape[1]
    vec = pl.BlockSpec((1, d), lambda i, j: (0, 0))
    return pl.pallas_call(
        _mlp_kernel,
        grid=(m // tm, f // tf),
        in_specs=[pl.BlockSpec((tm, d), lambda i, j: (i, 0)), vec,
                  pl.BlockSpec((d, tf), lambda i, j: (0, j)),
                  pl.BlockSpec((tf, d), lambda i, j: (j, 0)), vec],
        out_specs=pl.BlockSpec((tm, d), lambda i, j: (i, 0)),
        out_shape=jax.ShapeDtypeStruct((m, d), F32),
        scratch_shapes=[pltpu.VMEM((tm, d), BF16), pltpu.VMEM((tm, d), F32)],
        compiler_params=_cparams("parallel", "arbitrary"),
        name=name,
    )(x, g_pre.reshape(1, -1), w_up, w_down, g_post.reshape(1, -1))


def _rope_angles(pos, half):
    inv = ROPE_THETA ** (-jnp.arange(half, dtype=F32) / half)
    ang = pos.astype(F32)[:, None] * inv[None, :]
    return jnp.cos(ang), jnp.sin(ang)


def _mla_rope_tables(pos):
    cos, sin = _rope_angles(pos, MLA_ROPE // 2)
    t = pos.shape[0]
    one = jnp.ones((t, MLA_NOPE), F32)
    zero64 = jnp.zeros((t, MLA_NOPE), F32)
    pad = jnp.zeros((t, HEAD_SLAB - MLA_NOPE - MLA_ROPE), F32)
    return (jnp.concatenate([one, cos, cos, pad], axis=1),
            jnp.concatenate([zero64, -sin, sin, pad], axis=1))


def _ret_tables(pos, blk, n_real):
    cos, sin = _rope_angles(pos, RET_DK // 2)
    cos_q = jnp.tile(jnp.concatenate([cos, cos], axis=1), (1, RET_HEADS))
    sin_q = jnp.tile(jnp.concatenate([-sin, sin], axis=1), (1, RET_HEADS))
    k_scale = RET_DK ** -0.5
    log_g = jnp.log1p(-jnp.exp2(-5.0 - jnp.arange(RET_HEADS, dtype=F32)))
    idx = jnp.arange(blk, dtype=F32)
    diff = idx[:, None] - idx[None, :]
    dmat = jnp.where(diff >= 0, jnp.exp(log_g[:, None, None] * jnp.maximum(diff, 0.0)), 0.0)
    rowdec = jnp.exp(log_g[None, :] * (idx[:, None] + 1.0))
    w = jnp.where(idx[:, None] < n_real,
                  jnp.exp(log_g[None, :] * jnp.maximum(n_real - 1.0 - idx[:, None], 0.0)), 0.0)
    sdec = jnp.exp(log_g * n_real)
    n_pair = RET_HEADS // 2
    return dict(
        cos_q=cos_q, sin_q=sin_q, cos_k=cos_q * k_scale, sin_k=sin_q * k_scale,
        dmat=dmat,
        rowdec=jnp.repeat(rowdec, RET_DV, axis=1),
        kw=jnp.repeat(w, RET_DK, axis=1),
        sdec=jnp.broadcast_to(jnp.repeat(sdec, RET_DK).reshape(n_pair, LANE, 1),
                              (n_pair, LANE, RET_DV)),
    )


def _band_bias(table, tq, span, q_off, allowed):
    i = jnp.arange(tq)[:, None]
    j = jnp.arange(span)[None, :]
    rel = q_off + i - j
    bias = table[:, jnp.clip(rel, -MAX_REL, MAX_REL) + MAX_REL].astype(F32)
    return jnp.where(allowed(i, j)[None], bias, NEG_INF)


def _layer_weights(l, w_in, w_mla_uq, w_mla_ukv, w_br_a, w_br_b, w_br_c, w_out, w_mem_q, w_mem_k,
                   w_mem_v, w_mem_o, w_up, w_down):
    d = w_in.shape[1]
    parts, start = [], 0
    for n in (MLA_Q_RANK, MLA_KV_RANK, MLA_ROPE, 256, 256, 512, 512, 512, 512, 512, 3 * d):
        parts.append(w_in[l, :, start:start + n])
        start += n
    zq, zkv, zpe, rq, rk, rv, rg, cq, ck, cv, zg = parts
    zeros = lambda n: jnp.zeros((d, n), w_in.dtype)
    w_in_l = jnp.concatenate(
        [zg, cq, ck, cv, rv, rg, rq, rk, zkv, zeros(MLA_NOPE), zpe,
         zeros(HEAD_SLAB - MLA_NOPE - MLA_ROPE), zq], axis=1).astype(BF16)
    assert w_in_l.shape[1] == Z_WIDTH
    pad_head = lambda w: jnp.pad(w, ((0, 0), (0, 0), (0, HEAD_SLAB - w.shape[-1])))
    flat = lambda w: w.reshape(w.shape[0], -1).astype(BF16)
    return dict(
        w_in=w_in_l,
        w_uq=flat(pad_head(w_mla_uq[l])),
        w_uk=flat(pad_head(w_mla_ukv[l][..., :MLA_NOPE])),
        w_uv=flat(w_mla_ukv[l][..., MLA_NOPE:]),
        w_a=w_br_a[l].astype(BF16), w_b=w_br_b[l].astype(BF16), w_c=w_br_c[l].astype(BF16),
        w_out=w_out[l].astype(BF16),
        w_mq=flat(w_mem_q[l]),
        w_mkv=jnp.concatenate([flat(w_mem_k[l]), flat(w_mem_v[l])], axis=1),
        w_mo=w_mem_o[l].reshape(-1, d).astype(BF16),
        w_up=w_up[l].astype(BF16), w_down=w_down[l].astype(BF16),
    )


def _tile(n, pref):
    t = min(n, pref)
    while n % t:
        t -= LANE
    return t


def _trunk_layer(x, w, P, l, tabs, mem_k, mem_v, past, tag):
    b, s, d = x.shape
    m = b * s
    x2 = x.reshape(m, d)
    tm = _tile(m, 1024)
    tm2 = _tile(m, 512)
    z = norm_matmul(x2, P["g_pre_mix"][l], w["w_in"], BF16, tm, 1152, f"in_proj_{tag}")
    q, ckv, kpe = mla_q_prep(z, P["g_mla_q"][l], P["g_mla_kv"][l], w["w_uq"],
                             tabs["mla_cos"], tabs["mla_sin"], tm2, f"mla_q_{tag}")
    z3 = z.reshape(b, s, Z_WIDTH)
    q3 = q.reshape(b, s, -1)
    if past is None:
        k, v = mla_kv_up(ckv, kpe, w["w_uk"], w["w_uv"], tm, f"mla_kv_{tag}")
        t_att = _tile(s, 512)
        o_a = mla_attention(q3, k.reshape(b, s, -1), v.reshape(b, s, -1), t_att, t_att, True, s,
                            f"mla_attn_{tag}")
        init = jnp.zeros((b, RET_HEADS // 2, LANE, RET_DV), F32)
        y_b, state = retention(z3, tabs["ret"], P["g_ret_gn"][l], init, tabs["ret_blk"],
                               f"retention_{tag}")
        o_c = band_attention(z3, Z_CQ // LANE, [z3] * 3, Z_CK // LANE, [z3] * 3, Z_CV // LANE,
                             tabs["band_bias"][l], tabs["band_tq"], f"band_{tag}")
        n_real = s
        band_k = z3[:, s - BAND_WINDOW:, Z_CK:Z_CK + 512]
        band_v = z3[:, s - BAND_WINDOW:, Z_CV:Z_CV + 512]
    else:
        c_ckv, c_kpe, s_ret, c_bk, c_bv = past
        n_real = CHUNK
        n_past = c_ckv.shape[1]
        kpe_pad = jnp.pad(c_kpe, ((0, 0), (0, 0), (MLA_NOPE, HEAD_SLAB - MLA_NOPE - MLA_ROPE)))
        k_c, v_c = mla_kv_up(c_ckv.reshape(b * n_past, -1), kpe_pad.reshape(b * n_past, -1),
                             w["w_uk"], w["w_uv"], _tile(b * n_past, 1024), f"mla_kv_cache_{tag}")
        k_n, v_n = mla_kv_up(ckv, kpe, w["w_uk"], w["w_uv"], tm, f"mla_kv_{tag}")
        nk = n_past + n_real
        nk_pad = -(-nk // LANE) * LANE
        cat = lambda c, n: jnp.pad(
            jnp.concatenate([c.reshape(b, n_past, -1), n.reshape(b, s, -1)[:, :n_real]], axis=1),
            ((0, 0), (0, nk_pad - nk), (0, 0)))
        o_a = mla_attention(q3, cat(k_c, k_n), cat(v_c, v_n), s, nk_pad, False, nk,
                            f"mla_attn_{tag}")
        init = s_ret.astype(F32).reshape(b, RET_HEADS // 2, LANE, RET_DV)
        y_b, state = retention(z3, tabs["ret"], P["g_ret_gn"][l], init, s, f"retention_{tag}")
        band_k = z3[:, :n_real, Z_CK:Z_CK + 512]
        band_v = z3[:, :n_real, Z_CV:Z_CV + 512]
        span = tabs["band_bias"][l].shape[-1]
        w_band = c_bk.shape[1]
        catb = lambda c, n: jnp.pad(
            jnp.concatenate([c.reshape(b, w_band, -1).astype(BF16), n], axis=1),
            ((0, 0), (0, span - w_band - n_real), (0, 0)))
        o_c = band_attention(z3, Z_CQ // LANE, [catb(c_bk, band_k)], 0, [catb(c_bv, band_v)], 0,
                             tabs["band_bias"][l], s, f"band_{tag}")
    x2 = mix_out(o_a.reshape(m, -1), y_b.reshape(m, -1), o_c.reshape(m, -1), z, P["b_gate"][l],
                 w["w_a"], w["w_b"], w["w_c"], w["w_out"], P["g_post_mix"][l], x2, tm2,
                 f"mix_out_{tag}")
    x3 = mem_attention(x2.reshape(b, s, d), mem_k, mem_v, P["g_pre_mem"][l], w["w_mq"], w["w_mo"],
                       P["g_post_mem"][l], _tile(s, 512), f"mem_attn_{tag}")
    x4 = mlp(x3.reshape(m, d), P["g_pre_ff"][l], w["w_up"], w["w_down"], P["g_post_ff"][l],
             tm, 512, f"mlp_{tag}")
    new = (ckv.reshape(b, s, -1)[:, :n_real],
           kpe.reshape(b, s, -1)[:, :n_real, MLA_NOPE:MLA_NOPE + MLA_ROPE],
           state.reshape(b, RET_HEADS, RET_DK, RET_DV),
           band_k.astype(F32).reshape(b, -1, BAND_HEADS, BAND_HD),
           band_v.astype(F32).reshape(b, -1, BAND_HEADS, BAND_HD))
    return x4.reshape(b, s, d), new


def kernel(x_prompt, x_sample, cache_mla_ckv, cache_mla_kpe, state_ret, cache_band_k, cache_band_v, cache_mem_k, cache_mem_v, mem_prompt, g_pre_mix, w_in, g_mla_q, w_mla_uq, g_mla_kv, w_mla_ukv, g_ret_gn, band_rel_bias, w_br_a, w_br_b, w_br_c, b_gate, w_out, g_post_mix, g_pre_mem, g_mem, w_mem_q, w_mem_k, w_mem_v, w_mem_o, g_post_mem, g_pre_ff, w_up, w_down, g_post_ff):
    P = dict(g_pre_mix=g_pre_mix, g_mla_q=g_mla_q, g_mla_kv=g_mla_kv, g_ret_gn=g_ret_gn,
             b_gate=b_gate, g_post_mix=g_post_mix, g_pre_mem=g_pre_mem, g_post_mem=g_post_mem,
             g_pre_ff=g_pre_ff, g_post_ff=g_post_ff)
    depth = w_in.shape[0]
    bp, sp, d = x_prompt.shape
    bs, ss, _ = x_sample.shape
    n_past = cache_mla_ckv.shape[2]
    w_band = cache_band_k.shape[2]
    assert ss == CHUNK and sp % 512 == 0 and sp >= BAND_WINDOW
    s_pad = 2 * CHUNK

    pos_p = jnp.arange(sp)
    pos_s = n_past + jnp.arange(s_pad)
    ret_blk = 256
    band_tq = 256
    cos_p, sin_p = _mla_rope_tables(pos_p)
    cos_s, sin_s = _mla_rope_tables(pos_s)

    span_p = 3 * band_tq
    band_ok = lambda i, j: (j // CHUNK >= i // CHUNK) & (j // CHUNK <= i // CHUNK + BAND_PREV_CHUNKS)
    bias_p = [jnp.stack([
        _band_bias(band_rel_bias[l], band_tq, span_p, 2 * band_tq,
                   lambda i, j, v=v: band_ok(i, j) & (j >= (2 - v) * band_tq))
        for v in range(3)]) for l in range(depth)]
    span_s = -(-(w_band + CHUNK) // LANE) * LANE
    bias_s = [_band_bias(band_rel_bias[l], s_pad, span_s, w_band,
                         lambda i, j: (j < w_band + CHUNK) & (i >= 0))[None] for l in range(depth)]

    tabs_p = dict(mla_cos=cos_p, mla_sin=sin_p, ret=_ret_tables(pos_p, ret_blk, ret_blk),
                  ret_blk=ret_blk, band_bias=bias_p, band_tq=band_tq)
    tabs_s = dict(mla_cos=jnp.tile(cos_s, (bs, 1)), mla_sin=jnp.tile(sin_s, (bs, 1)),
                  ret=_ret_tables(pos_s, s_pad, CHUNK), band_bias=bias_s)

    xp = x_prompt
    xs = jnp.pad(x_sample, ((0, 0), (0, s_pad - ss), (0, 0)))
    mem2 = mem_prompt.reshape(-1, d)
    new_p = [[] for _ in range(7)]
    new_s = [[] for _ in range(5)]
    for l in range(depth):
        w = _layer_weights(l, w_in, w_mla_uq, w_mla_ukv, w_br_a, w_br_b, w_br_c, w_out, w_mem_q,
                           w_mem_k, w_mem_v, w_mem_o, w_up, w_down)
        mkv = norm_matmul(mem2, g_mem[l], w["w_mkv"], F32, _tile(mem2.shape[0], 1024), 512,
                          f"mem_kv_{l}")
        e = MEM_HEADS * MEM_HD
        mk = mkv[:, :e].reshape(bp, -1, e)
        mv = mkv[:, e:].reshape(bp, -1, e)
        xp, st_p = _trunk_layer(xp, w, P, l, tabs_p, mk, mv, None, f"p{l}")
        xs, st_s = _trunk_layer(xs, w, P, l, tabs_s, cache_mem_k[l].reshape(bs, -1, e),
                                cache_mem_v[l].reshape(bs, -1, e),
                                (cache_mla_ckv[l], cache_mla_kpe[l], state_ret[l], cache_band_k[l],
                                 cache_band_v[l]), f"s{l}")
        mem_shape = (bp, -1, MEM_HEADS, MEM_HD)
        for acc, t in zip(new_p, st_p + (mk.reshape(mem_shape), mv.reshape(mem_shape))):
            acc.append(t)
        for acc, t in zip(new_s, st_s):
            acc.append(t)
    stack = lambda ts: jnp.stack(ts, axis=0)
    return (xp, xs[:, :ss],
            stack(new_p[0]), stack(new_p[1]), stack(new_p[2]), stack(new_p[3]), stack(new_p[4]),
            stack(new_p[5]), stack(new_p[6]),
            stack(new_s[0]), stack(new_s[1]), stack(new_s[2]), stack(new_s[3]), stack(new_s[4]))
```

```python
import functools

import numpy as np
import jax
import jax.numpy as jnp
from jax import lax
from jax.experimental import pallas as pl
from jax.experimental.pallas import tpu as pltpu

F32 = jnp.float32
BF16 = jnp.bfloat16

CHUNK = 64
MLA_HEADS = 8
MLA_Q_RANK = 384
MLA_KV_RANK = 256
MLA_NOPE = 64
MLA_ROPE = 32
MLA_V = 64
MLA_SCALE = (MLA_NOPE + MLA_ROPE) ** -0.5
RET_HEADS = 4
RET_DK = 64
RET_DV = 128
BAND_HEADS = 8
BAND_HD = 64
BAND_PREV_CHUNKS = 8
BAND_WINDOW = BAND_PREV_CHUNKS * CHUNK
MAX_REL = 128
MEM_HEADS = 4
MEM_HD = 128
ROPE_THETA = 10000.0
EPS = 1e-6
NEG_INF = -1e30

LANE = 128
HEAD_SLAB = 128
VMEM_LIMIT = 48 * 1024 * 1024

Z_G = 0
Z_CQ = 3072
Z_CK = 3584
Z_CV = 4096
Z_RV = 4608
Z_RG = 5120
Z_RQ = 5632
Z_RK = 5888
Z_KV = 6144
Z_PE = 6400
Z_Q = 6528
Z_WIDTH = 6912


def _cparams(*sem):
    return pltpu.CompilerParams(dimension_semantics=sem, vmem_limit_bytes=VMEM_LIMIT)


def _rms(x, g):
    return x * lax.rsqrt(jnp.mean(x * x, axis=-1, keepdims=True) + EPS) * g


def _dot(a, b):
    return jnp.dot(a, b, preferred_element_type=F32)


def _dot_nt(a, b):
    return lax.dot_general(a, b, (((1,), (1,)), ((), ())), preferred_element_type=F32)


def _dot_tn(a, b):
    return lax.dot_general(a, b, (((0,), (0,)), ((), ())), preferred_element_type=F32)


def _sigmoid(x):
    return 1.0 / (1.0 + jnp.exp(-x))


def _lane_iota(shape):
    return lax.broadcasted_iota(jnp.int32, shape, len(shape) - 1)


def _norm_matmul_kernel(x_ref, g_ref, w_ref, o_ref, xn_ref):
    @pl.when(pl.program_id(1) == 0)
    def _():
        xn_ref[...] = _rms(x_ref[...].astype(F32), g_ref[...]).astype(BF16)

    o_ref[...] = _dot(xn_ref[...], w_ref[...]).astype(o_ref.dtype)


def norm_matmul(x, g, w, out_dtype, tm, tn, name):
    m, k = x.shape
    n = w.shape[1]
    assert m % tm == 0 and n % tn == 0, (m, tm, n, tn)
    return pl.pallas_call(
        _norm_matmul_kernel,
        grid=(m // tm, n // tn),
        in_specs=[
            pl.BlockSpec((tm, k), lambda i, j: (i, 0)),
            pl.BlockSpec((1, k), lambda i, j: (0, 0)),
            pl.BlockSpec((k, tn), lambda i, j: (0, j)),
        ],
        out_specs=pl.BlockSpec((tm, tn), lambda i, j: (i, j)),
        out_shape=jax.ShapeDtypeStruct((m, n), out_dtype),
        scratch_shapes=[pltpu.VMEM((tm, k), BF16)],
        compiler_params=_cparams("parallel", "arbitrary"),
        name=name,
    )(x, g.reshape(1, k), w)


def _rope_slab(x, cos, sin, rot):
    first_end, half, period = rot
    width = x.shape[1]
    right = pltpu.roll(x, width - half, 1)
    left = pltpu.roll(x, half, 1)
    partner = jnp.where((_lane_iota(x.shape) & (period - 1)) < first_end, right, left)
    return x * cos + partner * sin


_MLA_ROT = (MLA_NOPE + MLA_ROPE // 2, MLA_ROPE // 2, LANE)
_RET_ROT = (RET_DK // 2, RET_DK // 2, RET_DK)


def _mla_q_kernel(zq_ref, zkv_ref, zpe_ref, gq_ref, gkv_ref, wq_ref, cos_ref, sin_ref,
                  q_ref, ckv_ref, kpe_ref):
    cos = cos_ref[...]
    sin = sin_ref[...]
    qn = _rms(zq_ref[...].astype(F32), gq_ref[...]).astype(BF16)
    q = _dot(qn, wq_ref[...])
    for h in range(MLA_HEADS):
        sl = slice(h * HEAD_SLAB, (h + 1) * HEAD_SLAB)
        q_ref[:, sl] = (_rope_slab(q[:, sl], cos, sin, _MLA_ROT) * MLA_SCALE).astype(BF16)
    ckv_ref[...] = _rms(zkv_ref[...].astype(F32), gkv_ref[...])
    kpe_ref[...] = _rope_slab(zpe_ref[...].astype(F32), cos, sin, _MLA_ROT)


def mla_q_prep(z, g_q, g_kv, w_uq, cos, sin, tm, name):
    m = z.shape[0]
    nt = cos.shape[0] // tm
    row = lambda w: pl.BlockSpec((1, w), lambda i: (0, 0))
    return pl.pallas_call(
        _mla_q_kernel,
        grid=(m // tm,),
        in_specs=[
            pl.BlockSpec((tm, MLA_Q_RANK), lambda i: (i, Z_Q // MLA_Q_RANK)),
            pl.BlockSpec((tm, MLA_KV_RANK), lambda i: (i, Z_KV // MLA_KV_RANK)),
            pl.BlockSpec((tm, HEAD_SLAB), lambda i: (i, Z_PE // HEAD_SLAB)),
            row(MLA_Q_RANK), row(MLA_KV_RANK),
            pl.BlockSpec(w_uq.shape, lambda i: (0, 0)),
            pl.BlockSpec((tm, HEAD_SLAB), lambda i: (i % nt, 0)),
            pl.BlockSpec((tm, HEAD_SLAB), lambda i: (i % nt, 0)),
        ],
        out_specs=[
            pl.BlockSpec((tm, MLA_HEADS * HEAD_SLAB), lambda i: (i, 0)),
            pl.BlockSpec((tm, MLA_KV_RANK), lambda i: (i, 0)),
            pl.BlockSpec((tm, HEAD_SLAB), lambda i: (i, 0)),
        ],
        out_shape=[
            jax.ShapeDtypeStruct((m, MLA_HEADS * HEAD_SLAB), BF16),
            jax.ShapeDtypeStruct((m, MLA_KV_RANK), F32),
            jax.ShapeDtypeStruct((m, HEAD_SLAB), F32),
        ],
        compiler_params=_cparams("parallel"),
        name=name,
    )(z, z, z, g_q.reshape(1, -1), g_kv.reshape(1, -1), w_uq, cos, sin)


def _mla_kv_kernel(ckv_ref, kpe_ref, wk_ref, wv_ref, k_ref, v_ref):
    c = ckv_ref[...].astype(BF16)
    kn = _dot(c, wk_ref[...])
    kpe = kpe_ref[...]
    for h in range(MLA_HEADS):
        sl = slice(h * HEAD_SLAB, (h + 1) * HEAD_SLAB)
        k_ref[:, sl] = (kn[:, sl] + kpe).astype(BF16)
    v = _dot(c, wv_ref[...])
    ones_lane = (_lane_iota(v.shape) & (HEAD_SLAB - 1)) == MLA_V
    v_ref[...] = jnp.where(ones_lane, 1.0, v).astype(BF16)


def mla_kv_up(ckv, kpe, w_uk, w_uv, tm, name):
    m = ckv.shape[0]
    return pl.pallas_call(
        _mla_kv_kernel,
        grid=(m // tm,),
        in_specs=[
            pl.BlockSpec((tm, MLA_KV_RANK), lambda i: (i, 0)),
            pl.BlockSpec((tm, HEAD_SLAB), lambda i: (i, 0)),
            pl.BlockSpec(w_uk.shape, lambda i: (0, 0)),
            pl.BlockSpec(w_uv.shape, lambda i: (0, 0)),
        ],
        out_specs=[
            pl.BlockSpec((tm, MLA_HEADS * HEAD_SLAB), lambda i: (i, 0)),
            pl.BlockSpec((tm, MLA_HEADS * HEAD_SLAB), lambda i: (i, 0)),
        ],
        out_shape=[
            jax.ShapeDtypeStruct((m, MLA_HEADS * HEAD_SLAB), BF16),
            jax.ShapeDtypeStruct((m, MLA_HEADS * HEAD_SLAB), BF16),
        ],
        compiler_params=_cparams("parallel"),
        name=name,
    )(ckv, kpe, w_uk, w_uv)


def _head_of_pair(x, hh):
    lane = _lane_iota(x.shape)
    keep = (lane < BAND_HD) if hh == 0 else (lane >= BAND_HD)
    return jnp.where(keep, x, jnp.zeros_like(x))


def _mla_attn_kernel(q_ref, k_ref, v_ref, o_ref, acc_ref, sa_ref, sb_ref, *, tq, tk, causal,
                     n_kblocks, nk_valid):
    qi = pl.program_id(2)
    qs = [q_ref[0, :, hh * HEAD_SLAB:(hh + 1) * HEAD_SLAB] for hh in range(2)]
    heads = [slice(hh * HEAD_SLAB, (hh + 1) * HEAD_SLAB) for hh in range(2)]

    def rows_of(kb):
        return pl.ds(kb * tk if isinstance(kb, int) else pl.multiple_of(kb * tk, tk), tk)

    def scores(kb, s_ref):
        for hh in range(2):
            s_ref[hh] = _dot_nt(k_ref[0, rows_of(kb), heads[hh]], qs[hh])

    def consume(kb, s_ref, ms, mask):
        new_m = []
        for hh in range(2):
            s = s_ref[hh]
            if mask is not None:
                s = jnp.where(mask, s, NEG_INF)
            m = jnp.maximum(ms[hh], jnp.max(s, axis=0, keepdims=True))
            alpha = jnp.exp(ms[hh] - m)
            p = jnp.exp(s - m).astype(BF16)
            acc_ref[hh] = alpha * acc_ref[hh] + _dot_tn(v_ref[0, rows_of(kb), heads[hh]], p)
            new_m.append(m)
        return tuple(new_m)

    def finish():
        outs = [acc_ref[hh, :MLA_V, :] / acc_ref[hh, MLA_V:MLA_V + 1, :] for hh in range(2)]
        o_ref[0] = jnp.concatenate(outs, axis=0).T.astype(o_ref.dtype)

    acc_ref[...] = jnp.zeros_like(acc_ref)
    ms = tuple(jnp.full((1, tq), NEG_INF, F32) for _ in range(2))
    key = lambda: lax.broadcasted_iota(jnp.int32, (tk, tq), 0)
    if not causal:
        for kb in range(n_kblocks):
            mask = (kb * tk + key()) < nk_valid if (kb + 1) * tk > nk_valid else None
            scores(kb, sa_ref)
            ms = consume(kb, sa_ref, ms, mask)
        finish()
        return

    shift = CHUNK.bit_length() - 1
    qry = lax.broadcasted_iota(jnp.int32, (tk, tq), 1)
    diag = jnp.right_shift(key(), shift) <= jnp.right_shift(qry, shift)
    scores(0, sa_ref)

    def pair(j, ms):
        kb = 2 * j
        scores(kb + 1, sb_ref)
        ms = consume(kb, sa_ref, ms, None)
        scores(kb + 2, sa_ref)
        return consume(kb + 1, sb_ref, ms, None)

    ms = lax.fori_loop(0, qi // 2, pair, ms)

    @pl.when(qi % 2 == 0)
    def _():
        consume(qi, sa_ref, ms, diag)
        finish()

    @pl.when(qi % 2 == 1)
    def _():
        scores(qi, sb_ref)
        consume(qi, sb_ref, consume(qi - 1, sa_ref, ms, None), diag)
        finish()


def mla_attention(q, k, v, tq, tk, causal, nk_valid, name):
    b, sq, _ = q.shape
    sk = k.shape[1]
    assert sq % tq == 0 and sk % tk == 0 and (not causal or (tq == tk and sq == sk))
    kern = functools.partial(_mla_attn_kernel, tq=tq, tk=tk, causal=causal,
                             n_kblocks=sk // tk, nk_valid=nk_valid)
    return pl.pallas_call(
        kern,
        grid=(b, MLA_HEADS // 2, sq // tq),
        in_specs=[
            pl.BlockSpec((1, tq, 2 * HEAD_SLAB), lambda b_, h, i: (b_, i, h)),
            pl.BlockSpec((1, sk, 2 * HEAD_SLAB), lambda b_, h, i: (b_, 0, h)),
            pl.BlockSpec((1, sk, 2 * HEAD_SLAB), lambda b_, h, i: (b_, 0, h)),
        ],
        out_specs=pl.BlockSpec((1, tq, 2 * MLA_V), lambda b_, h, i: (b_, i, h)),
        out_shape=jax.ShapeDtypeStruct((b, sq, MLA_HEADS * MLA_V), BF16),
        scratch_shapes=[pltpu.VMEM((2, HEAD_SLAB, tq), F32), pltpu.VMEM((2, tk, tq), F32),
                        pltpu.VMEM((2, tk, tq), F32)],
        compiler_params=_cparams("parallel", "parallel", "arbitrary"),
        name=name,
    )(q, k, v)


def _band_attn_kernel(q_ref, k_ref, v_ref, bias_ref, o_ref, sa_ref, sb_ref, *, tq, nq, span,
                      sliding):
    def geom(i):
        if not sliding:
            return 0, span, 0
        if isinstance(i, int):
            n = min(i + 1, span // tq) * tq
            return max(i + 1 - span // tq, 0) * tq, n, span - n
        return pl.multiple_of((i + 1 - span // tq) * tq, tq), span, 0

    def q_rows(i):
        return pl.ds(i * tq if isinstance(i, int) else pl.multiple_of(i * tq, tq), tq)

    def scores(i, s_ref):
        start, n, boff = geom(i)
        q = q_ref[0, q_rows(i), :] * (BAND_HD ** -0.5)
        k = k_ref[0, pl.ds(start, n), :]
        for hh in range(2):
            s_ref[hh, :n, :] = _dot_nt(k, _head_of_pair(q, hh)) + bias_ref[0, hh, boff:boff + n, :]

    def consume(i, s_ref):
        start, n, _ = geom(i)
        v = v_ref[0, pl.ds(start, n), :]
        outs = []
        for hh in range(2):
            s = s_ref[hh, :n, :]
            m = jnp.max(s, axis=0, keepdims=True)
            p = jnp.exp(s - m)
            l = jnp.sum(p, axis=0, keepdims=True)
            outs.append(_dot_tn(v, p.astype(BF16)) / l)
        first = lax.broadcasted_iota(jnp.int32, outs[0].shape, 0) < BAND_HD
        o_ref[0, q_rows(i), :] = jnp.where(first, outs[0], outs[1]).T.astype(o_ref.dtype)

    scores(0, sa_ref)
    if nq == 1:
        consume(0, sa_ref)
        return
    scores(1, sb_ref)
    consume(0, sa_ref)
    scores(2, sa_ref)
    consume(1, sb_ref)

    def pair(j, carry):
        i = 2 * j + 2
        scores(i + 1, sb_ref)
        consume(i, sa_ref)
        scores(i + 2, sa_ref)
        consume(i + 1, sb_ref)
        return carry

    lax.fori_loop(0, (nq - 2) // 2 - 1, pair, 0)
    scores(nq - 1, sb_ref)
    consume(nq - 2, sa_ref)
    consume(nq - 1, sb_ref)


def band_attention(q, q_col, k, k_col, v, v_col, bias, tq, sliding, name):
    b, sq = q.shape[:2]
    sk = k.shape[1]
    nq = sq // tq
    span = bias.shape[-2]
    assert (sliding and sk == sq and nq >= 4 and nq % 2 == 0) or (nq == 1 and sk == span)
    col = lambda c: (lambda hp, b_: (b_, 0, c + hp))
    return pl.pallas_call(
        functools.partial(_band_attn_kernel, tq=tq, nq=nq, span=span, sliding=sliding),
        grid=(BAND_HEADS // 2, b),
        in_specs=[
            pl.BlockSpec((1, sq, LANE), col(q_col)),
            pl.BlockSpec((1, sk, LANE), col(k_col)),
            pl.BlockSpec((1, sk, LANE), col(v_col)),
            pl.BlockSpec((1, 2, span, tq), lambda hp, b_: (0, hp, 0, 0)),
        ],
        out_specs=pl.BlockSpec((1, sq, LANE), col(0)),
        out_shape=jax.ShapeDtypeStruct((b, sq, BAND_HEADS * BAND_HD), BF16),
        scratch_shapes=[pltpu.VMEM((2, span, tq), F32), pltpu.VMEM((2, span, tq), F32)],
        compiler_params=_cparams("parallel", "arbitrary"),
        name=name,
    )(q, k, v, bias.reshape((1,) + bias.shape))


def _retention_kernel(rq_ref, rk_ref, rv_ref, rg_ref, cq_ref, sq_ref, ck_ref, sk_ref,
                      dmat_ref, rowdec_ref, kw_ref, sdec_ref, gn_ref, init_ref,
                      y_ref, state_out_ref, state_ref):
    c = pl.program_id(1)

    @pl.when(c == 0)
    def _():
        state_ref[...] = init_ref[0]

    n_pair = RET_HEADS // 2
    q = _rope_slab(rq_ref[0].astype(F32), cq_ref[...], sq_ref[...], _RET_ROT)
    k = _rope_slab(rk_ref[0].astype(F32), ck_ref[...], sk_ref[...], _RET_ROT)
    qb = q.astype(BF16)
    kb = k.astype(BF16)
    kwb = (k * kw_ref[...]).astype(BF16)
    gn = gn_ref[...]
    row_is_first = lax.broadcasted_iota(jnp.int32, (LANE, RET_DV), 0) < RET_DK
    for p in range(n_pair):
        psl = slice(p * LANE, (p + 1) * LANE)
        st = state_ref[p]
        stb = st.astype(BF16)
        kv = []
        for hh in range(2):
            h = 2 * p + hh
            vsl = slice(h * RET_DV, (h + 1) * RET_DV)
            v = rv_ref[0, :, vsl]
            qh = _head_of_pair(qb[:, psl], hh)
            s = _dot_nt(qh, kb[:, psl]) * dmat_ref[h]
            o = _dot(s.astype(BF16), v) + _dot(qh, stb) * rowdec_ref[:, vsl]
            mu = jnp.mean(o, axis=-1, keepdims=True)
            d = o - mu
            yn = d * lax.rsqrt(jnp.mean(d * d, axis=-1, keepdims=True) + EPS)
            g = rg_ref[0, :, vsl].astype(F32)
            y_ref[0, :, vsl] = (yn * gn[:, vsl] * (g * _sigmoid(g))).astype(y_ref.dtype)
            kv.append(_dot_tn(kwb[:, psl], v))
        state_ref[p] = st * sdec_ref[p] + jnp.where(row_is_first, kv[0], kv[1])

    @pl.when(c == pl.num_programs(1) - 1)
    def _():
        state_out_ref[0] = state_ref[...]


def retention(z, tabs, g_gn, init_state, blk, name):
    b, s = z.shape[:2]
    nc = s // blk
    n_pair = RET_HEADS // 2
    zblk = lambda w, col: pl.BlockSpec((1, blk, w), lambda b_, c: (b_, c, col // w))
    tab = lambda w: pl.BlockSpec((blk, w), lambda b_, c: (c, 0))
    const = lambda a: pl.BlockSpec(a.shape, lambda b_, c: (0,) * a.ndim)
    qk_w = RET_HEADS * RET_DK
    v_w = RET_HEADS * RET_DV
    return pl.pallas_call(
        _retention_kernel,
        grid=(b, nc),
        in_specs=[
            zblk(qk_w, Z_RQ), zblk(qk_w, Z_RK), zblk(v_w, Z_RV), zblk(v_w, Z_RG),
            tab(qk_w), tab(qk_w), tab(qk_w), tab(qk_w),
            const(tabs["dmat"]), const(tabs["rowdec"]), const(tabs["kw"]), const(tabs["sdec"]),
            pl.BlockSpec((1, v_w), lambda b_, c: (0, 0)),
            pl.BlockSpec((1, n_pair, LANE, RET_DV), lambda b_, c: (b_, 0, 0, 0)),
        ],
        out_specs=[
            pl.BlockSpec((1, blk, v_w), lambda b_, c: (b_, c, 0)),
            pl.BlockSpec((1, n_pair, LANE, RET_DV), lambda b_, c: (b_, 0, 0, 0)),
        ],
        out_shape=[
            jax.ShapeDtypeStruct((b, s, v_w), BF16),
            jax.ShapeDtypeStruct((b, n_pair, LANE, RET_DV), F32),
        ],
        scratch_shapes=[pltpu.VMEM((n_pair, LANE, RET_DV), F32)],
        compiler_params=_cparams("parallel", "arbitrary"),
        name=name,
    )(z, z, z, z, tabs["cos_q"], tabs["sin_q"], tabs["cos_k"], tabs["sin_k"],
      tabs["dmat"], tabs["rowdec"], tabs["kw"], tabs["sdec"], g_gn.reshape(1, -1), init_state)


def _mix_out_kernel(oa_ref, yb_ref, oc_ref, zg0_ref, zg1_ref, zg2_ref, bg_ref, wa_ref, wb_ref,
                    wc_ref, wo_ref, g_ref, x_ref, o_ref):
    d = x_ref.shape[-1]
    merged = None
    for n, (br_ref, w_ref, zg_ref) in enumerate(
            ((oa_ref, wa_ref, zg0_ref), (yb_ref, wb_ref, zg1_ref), (oc_ref, wc_ref, zg2_ref))):
        gate = _sigmoid(zg_ref[...].astype(F32) + bg_ref[:, n * d:(n + 1) * d])
        term = gate * _dot(br_ref[...], w_ref[...])
        merged = term if merged is None else merged + term
    y = _dot(merged.astype(BF16), wo_ref[...])
    o_ref[...] = x_ref[...] + _rms(y, g_ref[...])


def mix_out(o_a, y_b, o_c, z, b_gate, w_a, w_b, w_c, w_out, g_post, x, tm, name):
    m, d = x.shape
    e = o_a.shape[1]
    act = pl.BlockSpec((tm, e), lambda i: (i, 0))
    zg = lambda n: pl.BlockSpec((tm, d), lambda i: (i, Z_G // d + n))
    const = lambda a: pl.BlockSpec(a.shape, lambda i: (0, 0))
    return pl.pallas_call(
        _mix_out_kernel,
        grid=(m // tm,),
        in_specs=[act, act, act, zg(0), zg(1), zg(2),
                  pl.BlockSpec((1, 3 * d), lambda i: (0, 0)),
                  const(w_a), const(w_b), const(w_c), const(w_out),
                  pl.BlockSpec((1, d), lambda i: (0, 0)),
                  pl.BlockSpec((tm, d), lambda i: (i, 0))],
        out_specs=pl.BlockSpec((tm, d), lambda i: (i, 0)),
        out_shape=jax.ShapeDtypeStruct((m, d), F32),
        compiler_params=_cparams("parallel"),
        name=name,
    )(o_a, y_b, o_c, z, z, z, b_gate.reshape(1, -1), w_a, w_b, w_c, w_out,
      g_post.reshape(1, -1), x)


def _mem_attn_kernel(x_ref, mk_ref, mv_ref, gpre_ref, wq_ref, wo_ref, gpost_ref, o_ref):
    x = x_ref[0]
    u = _rms(x, gpre_ref[...]).astype(BF16)
    q = _dot(u, wq_ref[...]).astype(BF16)
    outs = []
    for h in range(MEM_HEADS):
        sl = slice(h * MEM_HD, (h + 1) * MEM_HD)
        s = _dot_nt(q[:, sl], mk_ref[0, :, sl].astype(BF16)) * (MEM_HD ** -0.5)
        m = jnp.max(s, axis=-1, keepdims=True)
        p = jnp.exp(s - m)
        l = jnp.sum(p, axis=-1, keepdims=True)
        outs.append(_dot(p.astype(BF16), mv_ref[0, :, sl].astype(BF16)) / l)
    o = jnp.concatenate(outs, axis=-1).astype(BF16)
    o_ref[0] = x + _rms(_dot(o, wo_ref[...]), gpost_ref[...])


def mem_attention(x, mk, mv, g_pre, w_q, w_o, g_post, tm, name):
    b, s, d = x.shape
    const = lambda a: pl.BlockSpec(a.shape, lambda b_, i: (0, 0))
    vec = pl.BlockSpec((1, d), lambda b_, i: (0, 0))
    mem = pl.BlockSpec((1,) + mk.shape[1:], lambda b_, i: (b_, 0, 0))
    return pl.pallas_call(
        _mem_attn_kernel,
        grid=(b, s // tm),
        in_specs=[pl.BlockSpec((1, tm, d), lambda b_, i: (b_, i, 0)), mem, mem,
                  vec, const(w_q), const(w_o), vec],
        out_specs=pl.BlockSpec((1, tm, d), lambda b_, i: (b_, i, 0)),
        out_shape=jax.ShapeDtypeStruct((b, s, d), F32),
        compiler_params=_cparams("parallel", "parallel"),
        name=name,
    )(x, mk, mv, g_pre.reshape(1, -1), w_q, w_o, g_post.reshape(1, -1))


def _mlp_kernel(x_ref, gpre_ref, wu_ref, wd_ref, gpost_ref, o_ref, xn_ref, acc_ref):
    j = pl.program_id(1)

    @pl.when(j == 0)
    def _():
        xn_ref[...] = _rms(x_ref[...], gpre_ref[...]).astype(BF16)
        acc_ref[...] = jnp.zeros_like(acc_ref)

    h = jnp.square(jnp.maximum(_dot(xn_ref[...], wu_ref[...]), 0.0))
    acc_ref[...] += _dot(h.astype(BF16), wd_ref[...])

    @pl.when(j == pl.num_programs(1) - 1)
    def _():
        o_ref[...] = x_ref[...] + _rms(acc_ref[...], gpost_ref[...])


def mlp(x, g_pre, w_up, w_down, g_post, tm, tf, name):
    m, d = x.shape
    f = w_up.shape[1]
    vec = pl.BlockSpec((1, d), lambda i, j: (0, 0))
    return pl.pallas_call(
        _mlp_kernel,
        grid=(m // tm, f // tf),
        in_specs=[pl.BlockSpec((tm, d), lambda i, j: (i, 0)), vec,
                  pl.BlockSpec((d, tf), lambda i, j: (0, j)),
                  pl.BlockSpec((tf, d), lambda i, j: (j, 0)), vec],
        out_specs=pl.BlockSpec((tm, d), lambda i, j: (i, 0)),
        out_shape=jax.ShapeDtypeStruct((m, d), F32),
        scratch_shapes=[pltpu.VMEM((tm, d), BF16), pltpu.VMEM((tm, d), F32)],
        compiler_params=_cparams("parallel", "arbitrary"),
        name=name,
    )(x, g_pre.reshape(1, -1), w_up, w_down, g_post.reshape(1, -1))


def _rope_angles(pos, half):
    inv = ROPE_THETA ** (-jnp.arange(half, dtype=F32) / half)
    ang = pos.astype(F32)[:, None] * inv[None, :]
    return jnp.cos(ang), jnp.sin(ang)


def _mla_rope_tables(pos):
    cos, sin = _rope_angles(pos, MLA_ROPE // 2)
    t = pos.shape[0]
    one = jnp.ones((t, MLA_NOPE), F32)
    zero64 = jnp.zeros((t, MLA_NOPE), F32)
    pad = jnp.zeros((t, HEAD_SLAB - MLA_NOPE - MLA_ROPE), F32)
    return (jnp.concatenate([one, cos, cos, pad], axis=1),
            jnp.concatenate([zero64, -sin, sin, pad], axis=1))


def _ret_tables(pos, blk, n_real):
    cos, sin = _rope_angles(pos, RET_DK // 2)
    cos_q = jnp.tile(jnp.concatenate([cos, cos], axis=1), (1, RET_HEADS))
    sin_q = jnp.tile(jnp.concatenate([-sin, sin], axis=1), (1, RET_HEADS))
    k_scale = RET_DK ** -0.5
    log_g = jnp.log1p(-jnp.exp2(-5.0 - jnp.arange(RET_HEADS, dtype=F32)))
    idx = jnp.arange(blk, dtype=F32)
    diff = idx[:, None] - idx[None, :]
    dmat = jnp.where(diff >= 0, jnp.exp(log_g[:, None, None] * jnp.maximum(diff, 0.0)), 0.0)
    rowdec = jnp.exp(log_g[None, :] * (idx[:, None] + 1.0))
    w = jnp.where(idx[:, None] < n_real,
                  jnp.exp(log_g[None, :] * jnp.maximum(n_real - 1.0 - idx[:, None], 0.0)), 0.0)
    sdec = jnp.exp(log_g * n_real)
    n_pair = RET_HEADS // 2
    return dict(
        cos_q=cos_q, sin_q=sin_q, cos_k=cos_q * k_scale, sin_k=sin_q * k_scale,
        dmat=dmat,
        rowdec=jnp.repeat(rowdec, RET_DV, axis=1),
        kw=jnp.repeat(w, RET_DK, axis=1),
        sdec=jnp.broadcast_to(jnp.repeat(sdec, RET_DK).reshape(n_pair, LANE, 1),
                              (n_pair, LANE, RET_DV)),
    )


def _band_bias(table, tq, span, q_off, allowed):
    period = 1
    while period < tq + span - 1:
        period *= 2
    d = np.arange(period)
    ext = table[:, np.clip(d + q_off - span + 1, -MAX_REL, MAX_REL) + MAX_REL].astype(F32)
    g = jnp.roll(ext, -(span - 1), axis=1)
    h = table.shape[0]
    tile = jnp.tile(g, (1, span))[:, :span * (period - 1)].reshape(h, span, period - 1)[:, :, :tq]
    return jnp.where(allowed[None], tile, NEG_INF)


def _layer_weights(l, w_in, w_mla_uq, w_mla_ukv, w_br_a, w_br_b, w_br_c, w_out, w_mem_q, w_mem_k,
                   w_mem_v, w_mem_o, w_up, w_down):
    d = w_in.shape[1]
    parts, start = [], 0
    for n in (MLA_Q_RANK, MLA_KV_RANK, MLA_ROPE, 256, 256, 512, 512, 512, 512, 512, 3 * d):
        parts.append(w_in[l, :, start:start + n])
        start += n
    zq, zkv, zpe, rq, rk, rv, rg, cq, ck, cv, zg = parts
    zeros = lambda n: jnp.zeros((d, n), w_in.dtype)
    w_in_l = jnp.concatenate(
        [zg, cq, ck, cv, rv, rg, rq, rk, zkv, zeros(MLA_NOPE), zpe,
         zeros(HEAD_SLAB - MLA_NOPE - MLA_ROPE), zq], axis=1).astype(BF16)
    assert w_in_l.shape[1] == Z_WIDTH
    pad_head = lambda w: jnp.pad(w, ((0, 0), (0, 0), (0, HEAD_SLAB - w.shape[-1])))
    flat = lambda w: w.reshape(w.shape[0], -1).astype(BF16)
    return dict(
        w_in=w_in_l,
        w_uq=flat(pad_head(w_mla_uq[l])),
        w_uk=flat(pad_head(w_mla_ukv[l][..., :MLA_NOPE])),
        w_uv=flat(pad_head(w_mla_ukv[l][..., MLA_NOPE:])),
        w_a=w_br_a[l].astype(BF16), w_b=w_br_b[l].astype(BF16), w_c=w_br_c[l].astype(BF16),
        w_out=w_out[l].astype(BF16),
        w_mq=flat(w_mem_q[l]),
        w_mkv=jnp.concatenate([flat(w_mem_k[l]), flat(w_mem_v[l])], axis=1),
        w_mo=w_mem_o[l].reshape(-1, d).astype(BF16),
        w_up=w_up[l].astype(BF16), w_down=w_down[l].astype(BF16),
    )


def _tile(n, pref):
    t = min(n, pref)
    while n % t:
        t -= LANE
    return t


def _trunk_layer(x, w, P, l, tabs, mem_k, mem_v, past, tag):
    b, s, d = x.shape
    m = b * s
    x2 = x.reshape(m, d)
    tm = _tile(m, 1024)
    tm2 = _tile(m, 512)
    z = norm_matmul(x2, P["g_pre_mix"][l], w["w_in"], BF16, tm, 2304, f"in_proj_{tag}")
    q, ckv, kpe = mla_q_prep(z, P["g_mla_q"][l], P["g_mla_kv"][l], w["w_uq"],
                             tabs["mla_cos"], tabs["mla_sin"], tm2, f"mla_q_{tag}")
    z3 = z.reshape(b, s, Z_WIDTH)
    q3 = q.reshape(b, s, -1)
    if past is None:
        k, v = mla_kv_up(ckv, kpe, w["w_uk"], w["w_uv"], tm, f"mla_kv_{tag}")
        t_att = _tile(s, 512)
        o_a = mla_attention(q3, k.reshape(b, s, -1), v.reshape(b, s, -1), t_att, t_att, True, s,
                            f"mla_attn_{tag}")
        init = jnp.zeros((b, RET_HEADS // 2, LANE, RET_DV), F32)
        y_b, state = retention(z3, tabs["ret"], P["g_ret_gn"][l], init, tabs["ret_blk"],
                               f"retention_{tag}")
        o_c = band_attention(z3, Z_CQ // LANE, z3, Z_CK // LANE, z3, Z_CV // LANE,
                             tabs["band_bias"][l], tabs["band_tq"], True, f"band_{tag}")
        n_real = s
        band_k = z3[:, s - BAND_WINDOW:, Z_CK:Z_CK + 512]
        band_v = z3[:, s - BAND_WINDOW:, Z_CV:Z_CV + 512]
    else:
        c_ckv, c_kpe, s_ret, c_bk, c_bv = past
        n_real = CHUNK
        n_past = c_ckv.shape[1]
        kpe_pad = jnp.pad(c_kpe, ((0, 0), (0, 0), (MLA_NOPE, HEAD_SLAB - MLA_NOPE - MLA_ROPE)))
        k_c, v_c = mla_kv_up(c_ckv.reshape(b * n_past, -1), kpe_pad.reshape(b * n_past, -1),
                             w["w_uk"], w["w_uv"], _tile(b * n_past, 1024), f"mla_kv_cache_{tag}")
        k_n, v_n = mla_kv_up(ckv, kpe, w["w_uk"], w["w_uv"], tm, f"mla_kv_{tag}")
        nk = n_past + n_real
        nk_pad = -(-nk // LANE) * LANE
        cat = lambda c, n: jnp.pad(
            jnp.concatenate([c.reshape(b, n_past, -1), n.reshape(b, s, -1)[:, :n_real]], axis=1),
            ((0, 0), (0, nk_pad - nk), (0, 0)))
        o_a = mla_attention(q3, cat(k_c, k_n), cat(v_c, v_n), s, nk_pad, False, nk,
                            f"mla_attn_{tag}")
        init = s_ret.astype(F32).reshape(b, RET_HEADS // 2, LANE, RET_DV)
        y_b, state = retention(z3, tabs["ret"], P["g_ret_gn"][l], init, s, f"retention_{tag}")
        band_k = z3[:, :n_real, Z_CK:Z_CK + 512]
        band_v = z3[:, :n_real, Z_CV:Z_CV + 512]
        span = tabs["band_bias"][l].shape[-2]
        w_band = c_bk.shape[1]
        catb = lambda c, n: jnp.pad(
            jnp.concatenate([c.reshape(b, w_band, -1).astype(BF16), n], axis=1),
            ((0, 0), (0, span - w_band - n_real), (0, 0)))
        o_c = band_attention(z3, Z_CQ // LANE, catb(c_bk, band_k), 0, catb(c_bv, band_v), 0,
                             tabs["band_bias"][l], s, False, f"band_{tag}")
    x2 = mix_out(o_a.reshape(m, -1), y_b.reshape(m, -1), o_c.reshape(m, -1), z, P["b_gate"][l],
                 w["w_a"], w["w_b"], w["w_c"], w["w_out"], P["g_post_mix"][l], x2, tm2,
                 f"mix_out_{tag}")
    x3 = mem_attention(x2.reshape(b, s, d), mem_k, mem_v, P["g_pre_mem"][l], w["w_mq"], w["w_mo"],
                       P["g_post_mem"][l], _tile(s, 512), f"mem_attn_{tag}")
    x4 = mlp(x3.reshape(m, d), P["g_pre_ff"][l], w["w_up"], w["w_down"], P["g_post_ff"][l],
             tm, 512, f"mlp_{tag}")
    new = (ckv.reshape(b, s, -1)[:, :n_real],
           kpe.reshape(b, s, -1)[:, :n_real, MLA_NOPE:MLA_NOPE + MLA_ROPE],
           state.reshape(b, RET_HEADS, RET_DK, RET_DV),
           band_k.astype(F32).reshape(b, -1, BAND_HEADS, BAND_HD),
           band_v.astype(F32).reshape(b, -1, BAND_HEADS, BAND_HD))
    return x4.reshape(b, s, d), new


def kernel(x_prompt, x_sample, cache_mla_ckv, cache_mla_kpe, state_ret, cache_band_k, cache_band_v, cache_mem_k, cache_mem_v, mem_prompt, g_pre_mix, w_in, g_mla_q, w_mla_uq, g_mla_kv, w_mla_ukv, g_ret_gn, band_rel_bias, w_br_a, w_br_b, w_br_c, b_gate, w_out, g_post_mix, g_pre_mem, g_mem, w_mem_q, w_mem_k, w_mem_v, w_mem_o, g_post_mem, g_pre_ff, w_up, w_down, g_post_ff):
    P = dict(g_pre_mix=g_pre_mix, g_mla_q=g_mla_q, g_mla_kv=g_mla_kv, g_ret_gn=g_ret_gn,
             b_gate=b_gate, g_post_mix=g_post_mix, g_pre_mem=g_pre_mem, g_post_mem=g_post_mem,
             g_pre_ff=g_pre_ff, g_post_ff=g_post_ff)
    depth = w_in.shape[0]
    bp, sp, d = x_prompt.shape
    bs, ss, _ = x_sample.shape
    n_past = cache_mla_ckv.shape[2]
    w_band = cache_band_k.shape[2]
    assert ss == CHUNK and sp % 512 == 0 and sp >= BAND_WINDOW
    s_pad = 2 * CHUNK

    pos_p = jnp.arange(sp)
    pos_s = n_past + jnp.arange(s_pad)
    ret_blk = 256
    band_tq = 256
    cos_p, sin_p = _mla_rope_tables(pos_p)
    cos_s, sin_s = _mla_rope_tables(pos_s)

    span_p = 3 * band_tq
    jj = np.arange(span_p)[:, None]
    ii = np.arange(band_tq)[None, :]
    band_ok = (jj // CHUNK >= ii // CHUNK) & (jj // CHUNK <= ii // CHUNK + BAND_PREV_CHUNKS)
    bias_p = [_band_bias(band_rel_bias[l], band_tq, span_p, 2 * band_tq, band_ok)
              for l in range(depth)]
    span_s = -(-(w_band + CHUNK) // LANE) * LANE
    mask_s = np.broadcast_to(np.arange(span_s)[:, None] < w_band + CHUNK, (span_s, s_pad))
    bias_s = [_band_bias(band_rel_bias[l], s_pad, span_s, w_band, mask_s) for l in range(depth)]

    tabs_p = dict(mla_cos=cos_p, mla_sin=sin_p, ret=_ret_tables(pos_p, ret_blk, ret_blk),
                  ret_blk=ret_blk, band_bias=bias_p, band_tq=band_tq)
    tabs_s = dict(mla_cos=jnp.tile(cos_s, (bs, 1)), mla_sin=jnp.tile(sin_s, (bs, 1)),
                  ret=_ret_tables(pos_s, s_pad, CHUNK), band_bias=bias_s)

    xp = x_prompt
    xs = jnp.pad(x_sample, ((0, 0), (0, s_pad - ss), (0, 0)))
    mem2 = mem_prompt.reshape(-1, d)
    new_p = [[] for _ in range(7)]
    new_s = [[] for _ in range(5)]
    for l in range(depth):
        w = _layer_weights(l, w_in, w_mla_uq, w_mla_ukv, w_br_a, w_br_b, w_br_c, w_out, w_mem_q,
                           w_mem_k, w_mem_v, w_mem_o, w_up, w_down)
        mkv = norm_matmul(mem2, g_mem[l], w["w_mkv"], F32, _tile(mem2.shape[0], 1024), 512,
                          f"mem_kv_{l}")
        e = MEM_HEADS * MEM_HD
        mk = mkv[:, :e].reshape(bp, -1, e)
        mv = mkv[:, e:].reshape(bp, -1, e)
        xp, st_p = _trunk_layer(xp, w, P, l, tabs_p, mk, mv, None, f"p{l}")
        xs, st_s = _trunk_layer(xs, w, P, l, tabs_s, cache_mem_k[l].reshape(bs, -1, e),
                                cache_mem_v[l].reshape(bs, -1, e),
                                (cache_mla_ckv[l], cache_mla_kpe[l], state_ret[l], cache_band_k[l],
                                 cache_band_v[l]), f"s{l}")
        mem_shape = (bp, -1, MEM_HEADS, MEM_HD)
        for acc, t in zip(new_p, st_p + (mk.reshape(mem_shape), mv.reshape(mem_shape))):
            acc.append(t)
        for acc, t in zip(new_s, st_s):
            acc.append(t)
    stack = lambda ts: jnp.stack(ts, axis=0)
    return (xp, xs[:, :ss],
            stack(new_p[0]), stack(new_p[1]), stack(new_p[2]), stack(new_p[3]), stack(new_p[4]),
            stack(new_p[5]), stack(new_p[6]),
            stack(new_s[0]), stack(new_s[1]), stack(new_s[2]), stack(new_s[3]), stack(new_s[4]))
```

```python
import functools

import numpy as np
import jax
import jax.numpy as jnp
from jax import lax
from jax.experimental import pallas as pl
from jax.experimental.pallas import tpu as pltpu

F32 = jnp.float32
BF16 = jnp.bfloat16

CHUNK = 64
MLA_HEADS = 8
MLA_Q_RANK = 384
MLA_KV_RANK = 256
MLA_NOPE = 64
MLA_ROPE = 32
MLA_V = 64
MLA_SCALE = (MLA_NOPE + MLA_ROPE) ** -0.5
RET_HEADS = 4
RET_DK = 64
RET_DV = 128
BAND_HEADS = 8
BAND_HD = 64
BAND_PREV_CHUNKS = 8
BAND_WINDOW = BAND_PREV_CHUNKS * CHUNK
MAX_REL = 128
MEM_HEADS = 4
MEM_HD = 128
ROPE_THETA = 10000.0
EPS = 1e-6
NEG_INF = -1e30

LANE = 128
HEAD_SLAB = 128
VMEM_LIMIT = 48 * 1024 * 1024

Z_G = 0
Z_CQ = 3072
Z_CK = 3584
Z_CV = 4096
Z_RV = 4608
Z_RG = 5120
Z_RQ = 5632
Z_RK = 5888
Z_KV = 6144
Z_PE = 6400
Z_Q = 6528
Z_WIDTH = 6912


def _cparams(*sem):
    return pltpu.CompilerParams(dimension_semantics=sem, vmem_limit_bytes=VMEM_LIMIT)


def _rms(x, g):
    return x * lax.rsqrt(jnp.mean(x * x, axis=-1, keepdims=True) + EPS) * g


def _dot(a, b):
    return jnp.dot(a, b, preferred_element_type=F32)


def _dot_nt(a, b):
    return lax.dot_general(a, b, (((1,), (1,)), ((), ())), preferred_element_type=F32)


def _dot_tn(a, b):
    return lax.dot_general(a, b, (((0,), (0,)), ((), ())), preferred_element_type=F32)


def _sigmoid(x):
    return 1.0 / (1.0 + jnp.exp(-x))


def _lane_iota(shape):
    return lax.broadcasted_iota(jnp.int32, shape, len(shape) - 1)


def _norm_matmul_kernel(x_ref, g_ref, w_ref, o_ref, xn_ref):
    @pl.when(pl.program_id(1) == 0)
    def _():
        xn_ref[...] = _rms(x_ref[...].astype(F32), g_ref[...]).astype(BF16)

    o_ref[...] = _dot(xn_ref[...], w_ref[...]).astype(o_ref.dtype)


def norm_matmul(x, g, w, out_dtype, tm, tn, name):
    m, k = x.shape
    n = w.shape[1]
    assert m % tm == 0 and n % tn == 0, (m, tm, n, tn)
    return pl.pallas_call(
        _norm_matmul_kernel,
        grid=(m // tm, n // tn),
        in_specs=[
            pl.BlockSpec((tm, k), lambda i, j: (i, 0)),
            pl.BlockSpec((1, k), lambda i, j: (0, 0)),
            pl.BlockSpec((k, tn), lambda i, j: (0, j)),
        ],
        out_specs=pl.BlockSpec((tm, tn), lambda i, j: (i, j)),
        out_shape=jax.ShapeDtypeStruct((m, n), out_dtype),
        scratch_shapes=[pltpu.VMEM((tm, k), BF16)],
        compiler_params=_cparams("parallel", "arbitrary"),
        name=name,
    )(x, g.reshape(1, k), w)


def _rope_slab(x, cos, sin, rot):
    first_end, half, period = rot
    width = x.shape[1]
    right = pltpu.roll(x, width - half, 1)
    left = pltpu.roll(x, half, 1)
    partner = jnp.where((_lane_iota(x.shape) & (period - 1)) < first_end, right, left)
    return x * cos + partner * sin


_MLA_ROT = (MLA_NOPE + MLA_ROPE // 2, MLA_ROPE // 2, LANE)
_RET_ROT = (RET_DK // 2, RET_DK // 2, RET_DK)


def _mla_q_kernel(zq_ref, zkv_ref, zpe_ref, gq_ref, gkv_ref, wq_ref, cos_ref, sin_ref,
                  q_ref, ckv_ref, kpe_ref, kpe_out_ref):
    cos = cos_ref[...]
    sin = sin_ref[...]
    qn = _rms(zq_ref[...].astype(F32), gq_ref[...]).astype(BF16)
    q = _dot(qn, wq_ref[...])
    for h in range(MLA_HEADS):
        sl = slice(h * HEAD_SLAB, (h + 1) * HEAD_SLAB)
        q_ref[:, sl] = (_rope_slab(q[:, sl], cos, sin, _MLA_ROT) * MLA_SCALE).astype(BF16)
    ckv_ref[...] = _rms(zkv_ref[...].astype(F32), gkv_ref[...])
    kpe = _rope_slab(zpe_ref[...].astype(F32), cos, sin, _MLA_ROT)
    kpe_ref[...] = kpe
    kpe_out_ref[...] = kpe[:, MLA_NOPE:MLA_NOPE + MLA_ROPE]


def mla_q_prep(z, g_q, g_kv, w_uq, cos, sin, tm, name):
    m = z.shape[0]
    nt = cos.shape[0] // tm
    row = lambda w: pl.BlockSpec((1, w), lambda i: (0, 0))
    return pl.pallas_call(
        _mla_q_kernel,
        grid=(m // tm,),
        in_specs=[
            pl.BlockSpec((tm, MLA_Q_RANK), lambda i: (i, Z_Q // MLA_Q_RANK)),
            pl.BlockSpec((tm, MLA_KV_RANK), lambda i: (i, Z_KV // MLA_KV_RANK)),
            pl.BlockSpec((tm, HEAD_SLAB), lambda i: (i, Z_PE // HEAD_SLAB)),
            row(MLA_Q_RANK), row(MLA_KV_RANK),
            pl.BlockSpec(w_uq.shape, lambda i: (0, 0)),
            pl.BlockSpec((tm, HEAD_SLAB), lambda i: (i % nt, 0)),
            pl.BlockSpec((tm, HEAD_SLAB), lambda i: (i % nt, 0)),
        ],
        out_specs=[
            pl.BlockSpec((tm, MLA_HEADS * HEAD_SLAB), lambda i: (i, 0)),
            pl.BlockSpec((tm, MLA_KV_RANK), lambda i: (i, 0)),
            pl.BlockSpec((tm, HEAD_SLAB), lambda i: (i, 0)),
            pl.BlockSpec((tm, MLA_ROPE), lambda i: (i, 0)),
        ],
        out_shape=[
            jax.ShapeDtypeStruct((m, MLA_HEADS * HEAD_SLAB), BF16),
            jax.ShapeDtypeStruct((m, MLA_KV_RANK), F32),
            jax.ShapeDtypeStruct((m, HEAD_SLAB), F32),
            jax.ShapeDtypeStruct((m, MLA_ROPE), F32),
        ],
        compiler_params=_cparams("parallel"),
        name=name,
    )(z, z, z, g_q.reshape(1, -1), g_kv.reshape(1, -1), w_uq, cos, sin)


def _mla_kv_kernel(ckv_ref, kpe_ref, wk_ref, wv_ref, k_ref, v_ref):
    c = ckv_ref[...].astype(BF16)
    kn = _dot(c, wk_ref[...])
    kpe = kpe_ref[...]
    for h in range(MLA_HEADS):
        sl = slice(h * HEAD_SLAB, (h + 1) * HEAD_SLAB)
        k_ref[:, sl] = (kn[:, sl] + kpe).astype(BF16)
    v = _dot(c, wv_ref[...])
    ones_lane = (_lane_iota(v.shape) & (HEAD_SLAB - 1)) == MLA_V
    v_ref[...] = jnp.where(ones_lane, 1.0, v).astype(BF16)


def mla_kv_up(ckv, kpe, w_uk, w_uv, tm, name):
    m = ckv.shape[0]
    return pl.pallas_call(
        _mla_kv_kernel,
        grid=(m // tm,),
        in_specs=[
            pl.BlockSpec((tm, MLA_KV_RANK), lambda i: (i, 0)),
            pl.BlockSpec((tm, HEAD_SLAB), lambda i: (i, 0)),
            pl.BlockSpec(w_uk.shape, lambda i: (0, 0)),
            pl.BlockSpec(w_uv.shape, lambda i: (0, 0)),
        ],
        out_specs=[
            pl.BlockSpec((tm, MLA_HEADS * HEAD_SLAB), lambda i: (i, 0)),
            pl.BlockSpec((tm, MLA_HEADS * HEAD_SLAB), lambda i: (i, 0)),
        ],
        out_shape=[
            jax.ShapeDtypeStruct((m, MLA_HEADS * HEAD_SLAB), BF16),
            jax.ShapeDtypeStruct((m, MLA_HEADS * HEAD_SLAB), BF16),
        ],
        compiler_params=_cparams("parallel"),
        name=name,
    )(ckv, kpe, w_uk, w_uv)


def _head_of_pair(x, hh):
    lane = _lane_iota(x.shape)
    keep = (lane < BAND_HD) if hh == 0 else (lane >= BAND_HD)
    return jnp.where(keep, x, jnp.zeros_like(x))


_PAIR_SLABS = [slice(hh * HEAD_SLAB, (hh + 1) * HEAD_SLAB) for hh in range(2)]


def _mla_out(acc):
    return acc[:MLA_V, :] / acc[MLA_V:MLA_V + 1, :]


def _mla_attn_kernel(q_ref, k_ref, v_ref, o_ref, acc_ref, sa_ref, sb_ref, *, t):
    qi = pl.program_id(2)
    qs = [q_ref[0, :, sl] for sl in _PAIR_SLABS]

    def rows_of(kb):
        return pl.ds(kb * t if isinstance(kb, int) else pl.multiple_of(kb * t, t), t)

    def scores(kb, s_ref):
        for hh, sl in enumerate(_PAIR_SLABS):
            s_ref[hh] = _dot_nt(k_ref[0, rows_of(kb), sl], qs[hh])

    def consume(kb, s_ref, ms, mask):
        new_m = []
        for hh, sl in enumerate(_PAIR_SLABS):
            s = s_ref[hh]
            if mask is not None:
                s = jnp.where(mask, s, NEG_INF)
            m = jnp.maximum(ms[hh], jnp.max(s, axis=0, keepdims=True))
            alpha = jnp.exp(ms[hh] - m)
            p = jnp.exp(s - m).astype(BF16)
            acc_ref[hh] = alpha * acc_ref[hh] + _dot_tn(v_ref[0, rows_of(kb), sl], p)
            new_m.append(m)
        return tuple(new_m)

    def finish():
        outs = [_mla_out(acc_ref[hh]) for hh in range(2)]
        o_ref[0] = jnp.concatenate(outs, axis=0).T.astype(o_ref.dtype)

    acc_ref[...] = jnp.zeros_like(acc_ref)
    ms = tuple(jnp.full((1, t), NEG_INF, F32) for _ in range(2))
    shift = CHUNK.bit_length() - 1
    key_chunk = jnp.right_shift(lax.broadcasted_iota(jnp.int32, (t, t), 0), shift)
    qry_chunk = jnp.right_shift(lax.broadcasted_iota(jnp.int32, (t, t), 1), shift)
    diag = key_chunk <= qry_chunk
    scores(0, sa_ref)

    def pair(j, ms):
        kb = 2 * j
        scores(kb + 1, sb_ref)
        ms = consume(kb, sa_ref, ms, None)
        scores(kb + 2, sa_ref)
        return consume(kb + 1, sb_ref, ms, None)

    ms = lax.fori_loop(0, qi // 2, pair, ms)

    @pl.when(qi % 2 == 0)
    def _():
        consume(qi, sa_ref, ms, diag)
        finish()

    @pl.when(qi % 2 == 1)
    def _():
        scores(qi, sb_ref)
        consume(qi, sb_ref, consume(qi - 1, sa_ref, ms, None), diag)
        finish()


def _mla_attn_seg_kernel(q_ref, k1_ref, v1_ref, k2_ref, v2_ref, o_ref, *, n2_valid):
    outs = []
    for sl in _PAIR_SLABS:
        q = q_ref[0, :, sl]
        s1 = _dot_nt(k1_ref[0, :, sl], q)
        s2 = _dot_nt(k2_ref[0, :, sl], q)
        s2 = jnp.where(lax.broadcasted_iota(jnp.int32, s2.shape, 0) < n2_valid, s2, NEG_INF)
        m = jnp.maximum(jnp.max(s1, axis=0, keepdims=True), jnp.max(s2, axis=0, keepdims=True))
        acc = (_dot_tn(v1_ref[0, :, sl], jnp.exp(s1 - m).astype(BF16))
               + _dot_tn(v2_ref[0, :, sl], jnp.exp(s2 - m).astype(BF16)))
        outs.append(_mla_out(acc))
    o_ref[0] = jnp.concatenate(outs, axis=0).T.astype(o_ref.dtype)


def mla_attention_seg(q, k1, v1, k2, v2, n2_valid, name):
    b, sq, _ = q.shape
    blk = lambda a: pl.BlockSpec((1, a.shape[1], 2 * HEAD_SLAB), lambda b_, h: (b_, 0, h))
    return pl.pallas_call(
        functools.partial(_mla_attn_seg_kernel, n2_valid=n2_valid),
        grid=(b, MLA_HEADS // 2),
        in_specs=[blk(q), blk(k1), blk(v1), blk(k2), blk(v2)],
        out_specs=pl.BlockSpec((1, sq, 2 * MLA_V), lambda b_, h: (b_, 0, h)),
        out_shape=jax.ShapeDtypeStruct((b, sq, MLA_HEADS * MLA_V), BF16),
        compiler_params=_cparams("parallel", "parallel"),
        name=name,
    )(q, k1, v1, k2, v2)


def mla_attention(q, k, v, t, name):
    b, s, _ = q.shape
    assert s % t == 0 and k.shape[1] == s
    whole = pl.BlockSpec((1, s, 2 * HEAD_SLAB), lambda b_, h, i: (b_, 0, h))
    return pl.pallas_call(
        functools.partial(_mla_attn_kernel, t=t),
        grid=(b, MLA_HEADS // 2, s // t),
        in_specs=[pl.BlockSpec((1, t, 2 * HEAD_SLAB), lambda b_, h, i: (b_, i, h)), whole, whole],
        out_specs=pl.BlockSpec((1, t, 2 * MLA_V), lambda b_, h, i: (b_, i, h)),
        out_shape=jax.ShapeDtypeStruct((b, s, MLA_HEADS * MLA_V), BF16),
        scratch_shapes=[pltpu.VMEM((2, HEAD_SLAB, t), F32), pltpu.VMEM((2, t, t), F32),
                        pltpu.VMEM((2, t, t), F32)],
        compiler_params=_cparams("parallel", "parallel", "arbitrary"),
        name=name,
    )(q, k, v)


def _band_attn_kernel(q_ref, k_ref, v_ref, bias_ref, o_ref, sa_ref, sb_ref, *, tq, nq, span,
                      sliding):
    def geom(i):
        if not sliding:
            return 0, span, 0
        if isinstance(i, int):
            n = min(i + 1, span // tq) * tq
            return max(i + 1 - span // tq, 0) * tq, n, span - n
        return pl.multiple_of((i + 1 - span // tq) * tq, tq), span, 0

    def q_rows(i):
        return pl.ds(i * tq if isinstance(i, int) else pl.multiple_of(i * tq, tq), tq)

    def scores(i, s_ref):
        start, n, boff = geom(i)
        q = q_ref[0, q_rows(i), :] * (BAND_HD ** -0.5)
        k = k_ref[0, pl.ds(start, n), :]
        for hh in range(2):
            s_ref[hh, :n, :] = _dot_nt(k, _head_of_pair(q, hh)) + bias_ref[0, hh, boff:boff + n, :]

    def consume(i, s_ref):
        start, n, _ = geom(i)
        v = v_ref[0, pl.ds(start, n), :]
        outs = []
        for hh in range(2):
            s = s_ref[hh, :n, :]
            m = jnp.max(s, axis=0, keepdims=True)
            p = jnp.exp(s - m)
            l = jnp.sum(p, axis=0, keepdims=True)
            outs.append(_dot_tn(v, p.astype(BF16)) / l)
        first = lax.broadcasted_iota(jnp.int32, outs[0].shape, 0) < BAND_HD
        o_ref[0, q_rows(i), :] = jnp.where(first, outs[0], outs[1]).T.astype(o_ref.dtype)

    scores(0, sa_ref)
    if nq == 1:
        consume(0, sa_ref)
        return
    scores(1, sb_ref)
    consume(0, sa_ref)
    scores(2, sa_ref)
    consume(1, sb_ref)

    def pair(j, carry):
        i = 2 * j + 2
        scores(i + 1, sb_ref)
        consume(i, sa_ref)
        scores(i + 2, sa_ref)
        consume(i + 1, sb_ref)
        return carry

    lax.fori_loop(0, (nq - 2) // 2 - 1, pair, 0)
    scores(nq - 1, sb_ref)
    consume(nq - 2, sa_ref)
    consume(nq - 1, sb_ref)


def band_attention(q, q_col, k, k_col, v, v_col, bias, tq, sliding, name):
    b, sq = q.shape[:2]
    sk = k.shape[1]
    nq = sq // tq
    span = bias.shape[-2]
    assert (sliding and sk == sq and nq >= 4 and nq % 2 == 0) or (nq == 1 and sk == span)
    col = lambda c: (lambda hp, b_: (b_, 0, c + hp))
    return pl.pallas_call(
        functools.partial(_band_attn_kernel, tq=tq, nq=nq, span=span, sliding=sliding),
        grid=(BAND_HEADS // 2, b),
        in_specs=[
            pl.BlockSpec((1, sq, LANE), col(q_col)),
            pl.BlockSpec((1, sk, LANE), col(k_col)),
            pl.BlockSpec((1, sk, LANE), col(v_col)),
            pl.BlockSpec((1, 2, span, tq), lambda hp, b_: (0, hp, 0, 0)),
        ],
        out_specs=pl.BlockSpec((1, sq, LANE), col(0)),
        out_shape=jax.ShapeDtypeStruct((b, sq, BAND_HEADS * BAND_HD), BF16),
        scratch_shapes=[pltpu.VMEM((2, span, tq), F32), pltpu.VMEM((2, span, tq), F32)],
        compiler_params=_cparams("parallel", "arbitrary"),
        name=name,
    )(q, k, v, bias.reshape((1,) + bias.shape))


def _retention_kernel(rq_ref, rk_ref, rv_ref, rg_ref, cq_ref, sq_ref, ck_ref, sk_ref,
                      dmat_ref, rowdec_ref, kw_ref, sdec_ref, gn_ref, init_ref,
                      y_ref, state_out_ref, state_ref):
    c = pl.program_id(1)

    @pl.when(c == 0)
    def _():
        state_ref[...] = init_ref[0]

    n_pair = RET_HEADS // 2
    q = _rope_slab(rq_ref[0].astype(F32), cq_ref[...], sq_ref[...], _RET_ROT)
    k = _rope_slab(rk_ref[0].astype(F32), ck_ref[...], sk_ref[...], _RET_ROT)
    qb = q.astype(BF16)
    kb = k.astype(BF16)
    kwb = (k * kw_ref[...]).astype(BF16)
    gn = gn_ref[...]
    row_is_first = lax.broadcasted_iota(jnp.int32, (LANE, RET_DV), 0) < RET_DK
    for p in range(n_pair):
        psl = slice(p * LANE, (p + 1) * LANE)
        st = state_ref[p]
        stb = st.astype(BF16)
        kv = []
        for hh in range(2):
            h = 2 * p + hh
            vsl = slice(h * RET_DV, (h + 1) * RET_DV)
            v = rv_ref[0, :, vsl]
            qh = _head_of_pair(qb[:, psl], hh)
            s = _dot_nt(qh, kb[:, psl]) * dmat_ref[h]
            o = _dot(s.astype(BF16), v) + _dot(qh, stb) * rowdec_ref[:, vsl]
            mu = jnp.mean(o, axis=-1, keepdims=True)
            d = o - mu
            yn = d * lax.rsqrt(jnp.mean(d * d, axis=-1, keepdims=True) + EPS)
            g = rg_ref[0, :, vsl].astype(F32)
            y_ref[0, :, vsl] = (yn * gn[:, vsl] * (g * _sigmoid(g))).astype(y_ref.dtype)
            kv.append(_dot_tn(kwb[:, psl], v))
        state_ref[p] = st * sdec_ref[p] + jnp.where(row_is_first, kv[0], kv[1])

    @pl.when(c == pl.num_programs(1) - 1)
    def _():
        state_out_ref[0] = state_ref[...]


def retention(z, tabs, g_gn, init_state, blk, name):
    b, s = z.shape[:2]
    nc = s // blk
    n_pair = RET_HEADS // 2
    zblk = lambda w, col: pl.BlockSpec((1, blk, w), lambda b_, c: (b_, c, col // w))
    tab = lambda w: pl.BlockSpec((blk, w), lambda b_, c: (c, 0))
    const = lambda a: pl.BlockSpec(a.shape, lambda b_, c: (0,) * a.ndim)
    qk_w = RET_HEADS * RET_DK
    v_w = RET_HEADS * RET_DV
    return pl.pallas_call(
        _retention_kernel,
        grid=(b, nc),
        in_specs=[
            zblk(qk_w, Z_RQ), zblk(qk_w, Z_RK), zblk(v_w, Z_RV), zblk(v_w, Z_RG),
            tab(qk_w), tab(qk_w), tab(qk_w), tab(qk_w),
            const(tabs["dmat"]), const(tabs["rowdec"]), const(tabs["kw"]), const(tabs["sdec"]),
            pl.BlockSpec((1, v_w), lambda b_, c: (0, 0)),
            pl.BlockSpec((1, n_pair, LANE, RET_DV), lambda b_, c: (b_, 0, 0, 0)),
        ],
        out_specs=[
            pl.BlockSpec((1, blk, v_w), lambda b_, c: (b_, c, 0)),
            pl.BlockSpec((1, n_pair, LANE, RET_DV), lambda b_, c: (b_, 0, 0, 0)),
        ],
        out_shape=[
            jax.ShapeDtypeStruct((b, s, v_w), BF16),
            jax.ShapeDtypeStruct((b, n_pair, LANE, RET_DV), F32),
        ],
        scratch_shapes=[pltpu.VMEM((n_pair, LANE, RET_DV), F32)],
        compiler_params=_cparams("parallel", "arbitrary"),
        name=name,
    )(z, z, z, z, tabs["cos_q"], tabs["sin_q"], tabs["cos_k"], tabs["sin_k"],
      tabs["dmat"], tabs["rowdec"], tabs["kw"], tabs["sdec"], g_gn.reshape(1, -1), init_state)


def _mix_out_kernel(oa_ref, yb_ref, oc_ref, zg0_ref, zg1_ref, zg2_ref, bg_ref, wa_ref, wb_ref,
                    wc_ref, wo_ref, g_ref, x_ref, o_ref):
    d = x_ref.shape[-1]
    merged = None
    for n, (br_ref, w_ref, zg_ref) in enumerate(
            ((oa_ref, wa_ref, zg0_ref), (yb_ref, wb_ref, zg1_ref), (oc_ref, wc_ref, zg2_ref))):
        gate = _sigmoid(zg_ref[...].astype(F32) + bg_ref[:, n * d:(n + 1) * d])
        term = gate * _dot(br_ref[...], w_ref[...])
        merged = term if merged is None else merged + term
    y = _dot(merged.astype(BF16), wo_ref[...])
    o_ref[...] = x_ref[...] + _rms(y, g_ref[...])


def mix_out(o_a, y_b, o_c, z, b_gate, w_a, w_b, w_c, w_out, g_post, x, tm, name):
    m, d = x.shape
    e = o_a.shape[1]
    act = pl.BlockSpec((tm, e), lambda i: (i, 0))
    zg = lambda n: pl.BlockSpec((tm, d), lambda i: (i, Z_G // d + n))
    const = lambda a: pl.BlockSpec(a.shape, lambda i: (0, 0))
    return pl.pallas_call(
        _mix_out_kernel,
        grid=(m // tm,),
        in_specs=[act, act, act, zg(0), zg(1), zg(2),
                  pl.BlockSpec((1, 3 * d), lambda i: (0, 0)),
                  const(w_a), const(w_b), const(w_c), const(w_out),
                  pl.BlockSpec((1, d), lambda i: (0, 0)),
                  pl.BlockSpec((tm, d), lambda i: (i, 0))],
        out_specs=pl.BlockSpec((tm, d), lambda i: (i, 0)),
        out_shape=jax.ShapeDtypeStruct((m, d), F32),
        compiler_params=_cparams("parallel"),
        name=name,
    )(o_a, y_b, o_c, z, z, z, b_gate.reshape(1, -1), w_a, w_b, w_c, w_out,
      g_post.reshape(1, -1), x)


def _mem_attn_kernel(x_ref, mk_ref, mv_ref, gpre_ref, wq_ref, wo_ref, gpost_ref, o_ref):
    x = x_ref[0]
    u = _rms(x, gpre_ref[...]).astype(BF16)
    q = _dot(u, wq_ref[...]).astype(BF16)
    outs = []
    for h in range(MEM_HEADS):
        sl = slice(h * MEM_HD, (h + 1) * MEM_HD)
        s = _dot_nt(q[:, sl], mk_ref[0, :, sl].astype(BF16)) * (MEM_HD ** -0.5)
        m = jnp.max(s, axis=-1, keepdims=True)
        p = jnp.exp(s - m)
        l = jnp.sum(p, axis=-1, keepdims=True)
        outs.append(_dot(p.astype(BF16), mv_ref[0, :, sl].astype(BF16)) / l)
    o = jnp.concatenate(outs, axis=-1).astype(BF16)
    o_ref[0] = x + _rms(_dot(o, wo_ref[...]), gpost_ref[...])


def mem_attention(x, mk, mv, g_pre, w_q, w_o, g_post, tm, name):
    b, s, d = x.shape
    const = lambda a: pl.BlockSpec(a.shape, lambda b_, i: (0, 0))
    vec = pl.BlockSpec((1, d), lambda b_, i: (0, 0))
    mem = pl.BlockSpec((1,) + mk.shape[1:], lambda b_, i: (b_, 0, 0))
    return pl.pallas_call(
        _mem_attn_kernel,
        grid=(b, s // tm),
        in_specs=[pl.BlockSpec((1, tm, d), lambda b_, i: (b_, i, 0)), mem, mem,
                  vec, const(w_q), const(w_o), vec],
        out_specs=pl.BlockSpec((1, tm, d), lambda b_, i: (b_, i, 0)),
        out_shape=jax.ShapeDtypeStruct((b, s, d), F32),
        compiler_params=_cparams("parallel", "parallel"),
        name=name,
    )(x, mk, mv, g_pre.reshape(1, -1), w_q, w_o, g_post.reshape(1, -1))


def _mlp_kernel(x_ref, gpre_ref, wu_ref, wd_ref, gpost_ref, o_ref, xn_ref, acc_ref):
    j = pl.program_id(1)

    @pl.when(j == 0)
    def _():
        xn_ref[...] = _rms(x_ref[...], gpre_ref[...]).astype(BF16)
        acc_ref[...] = jnp.zeros_like(acc_ref)

    h = jnp.square(jnp.maximum(_dot(xn_ref[...], wu_ref[...]), 0.0))
    acc_ref[...] += _dot(h.astype(BF16), wd_ref[...])

    @pl.when(j == pl.num_programs(1) - 1)
    def _():
        o_ref[...] = x_ref[...] + _rms(acc_ref[...], gpost_ref[...])


def mlp(x, g_pre, w_up, w_down, g_post, tm, tf, name):
    m, d = x.shape
    f = w_up.shape[1]
    vec = pl.BlockSpec((1, d), lambda i, j: (0, 0))
    return pl.pallas_call(
        _mlp_kernel,
        grid=(m // tm, f // tf),
        in_specs=[pl.BlockSpec((tm, d), lambda i, j: (i, 0)), vec,
                  pl.BlockSpec((d, tf), lambda i, j: (0, j)),
                  pl.BlockSpec((tf, d), lambda i, j: (j, 0)), vec],
        out_specs=pl.BlockSpec((tm, d), lambda i, j: (i, 0)),
        out_shape=jax.ShapeDtypeStruct((m, d), F32),
        scratch_shapes=[pltpu.VMEM((tm, d), BF16), pltpu.VMEM((tm, d), F32)],
        compiler_params=_cparams("parallel", "arbitrary"),
        name=name,
    )(x, g_pre.reshape(1, -1), w_up, w_down, g_post.reshape(1, -1))


def _rope_angles(pos, half):
    inv = ROPE_THETA ** (-jnp.arange(half, dtype=F32) / half)
    ang = pos.astype(F32)[:, None] * inv[None, :]
    return jnp.cos(ang), jnp.sin(ang)


def _mla_rope_tables(pos):
    cos, sin = _rope_angles(pos, MLA_ROPE // 2)
    t = pos.shape[0]
    one = jnp.ones((t, MLA_NOPE), F32)
    zero64 = jnp.zeros((t, MLA_NOPE), F32)
    pad = jnp.zeros((t, HEAD_SLAB - MLA_NOPE - MLA_ROPE), F32)
    return (jnp.concatenate([one, cos, cos, pad], axis=1),
            jnp.concatenate([zero64, -sin, sin, pad], axis=1))


def _ret_tables(pos, blk, n_real):
    cos, sin = _rope_angles(pos, RET_DK // 2)
    cos_q = jnp.tile(jnp.concatenate([cos, cos], axis=1), (1, RET_HEADS))
    sin_q = jnp.tile(jnp.concatenate([-sin, sin], axis=1), (1, RET_HEADS))
    k_scale = RET_DK ** -0.5
    log_g = jnp.log1p(-jnp.exp2(-5.0 - jnp.arange(RET_HEADS, dtype=F32)))
    idx = jnp.arange(blk, dtype=F32)
    diff = idx[:, None] - idx[None, :]
    dmat = jnp.where(diff >= 0, jnp.exp(log_g[:, None, None] * jnp.maximum(diff, 0.0)), 0.0)
    rowdec = jnp.exp(log_g[None, :] * (idx[:, None] + 1.0))
    w = jnp.where(idx[:, None] < n_real,
                  jnp.exp(log_g[None, :] * jnp.maximum(n_real - 1.0 - idx[:, None], 0.0)), 0.0)
    sdec = jnp.exp(log_g * n_real)
    n_pair = RET_HEADS // 2
    return dict(
        cos_q=cos_q, sin_q=sin_q, cos_k=cos_q * k_scale, sin_k=sin_q * k_scale,
        dmat=dmat,
        rowdec=jnp.repeat(rowdec, RET_DV, axis=1),
        kw=jnp.repeat(w, RET_DK, axis=1),
        sdec=jnp.broadcast_to(jnp.repeat(sdec, RET_DK).reshape(n_pair, LANE, 1),
                              (n_pair, LANE, RET_DV)),
    )


def _band_bias(table, tq, span, q_off, allowed):
    sub = 8
    assert span % sub == 0
    length = tq + span - 1
    d = np.arange(length)
    ext = table[:, np.clip(d + q_off - span + 1, -MAX_REL, MAX_REL) + MAX_REL].astype(F32)
    shifted = jnp.stack([ext[:, sub - 1 - r:length - r] for r in range(sub)], axis=1)
    tile = jnp.concatenate([shifted[:, :, span - sub * (a + 1):span - sub * (a + 1) + tq]
                            for a in range(span // sub)], axis=1)
    return jnp.where(allowed[None], tile, NEG_INF)


def _layer_weights(l, w_in, w_mla_uq, w_mla_ukv, w_br_a, w_br_b, w_br_c, w_out, w_mem_q, w_mem_k,
                   w_mem_v, w_mem_o, w_up, w_down):
    d = w_in.shape[1]
    parts, start = [], 0
    for n in (MLA_Q_RANK, MLA_KV_RANK, MLA_ROPE, 256, 256, 512, 512, 512, 512, 512, 3 * d):
        parts.append(w_in[l, :, start:start + n])
        start += n
    zq, zkv, zpe, rq, rk, rv, rg, cq, ck, cv, zg = parts
    zeros = lambda n: jnp.zeros((d, n), w_in.dtype)
    w_in_l = jnp.concatenate(
        [zg, cq, ck, cv, rv, rg, rq, rk, zkv, zeros(MLA_NOPE), zpe,
         zeros(HEAD_SLAB - MLA_NOPE - MLA_ROPE), zq], axis=1).astype(BF16)
    assert w_in_l.shape[1] == Z_WIDTH
    pad_head = lambda w: jnp.pad(w, ((0, 0), (0, 0), (0, HEAD_SLAB - w.shape[-1])))
    flat = lambda w: w.reshape(w.shape[0], -1).astype(BF16)
    return dict(
        w_in=w_in_l,
        w_uq=flat(pad_head(w_mla_uq[l])),
        w_uk=flat(pad_head(w_mla_ukv[l][..., :MLA_NOPE])),
        w_uv=flat(pad_head(w_mla_ukv[l][..., MLA_NOPE:])),
        w_a=w_br_a[l].astype(BF16), w_b=w_br_b[l].astype(BF16), w_c=w_br_c[l].astype(BF16),
        w_out=w_out[l].astype(BF16),
        w_mq=flat(w_mem_q[l]),
        w_mkv=jnp.concatenate([flat(w_mem_k[l]), flat(w_mem_v[l])], axis=1),
        w_mo=w_mem_o[l].reshape(-1, d).astype(BF16),
        w_up=w_up[l].astype(BF16), w_down=w_down[l].astype(BF16),
    )


def _tile(n, pref):
    t = min(n, pref)
    while n % t:
        t -= LANE
    return t


def _trunk_layer(x, w, P, l, tabs, mem_k, mem_v, past, tag):
    b, s, d = x.shape
    m = b * s
    x2 = x.reshape(m, d)
    tm = _tile(m, 1024)
    tm2 = _tile(m, 512)
    z = norm_matmul(x2, P["g_pre_mix"][l], w["w_in"], BF16, tm, 2304, f"in_proj_{tag}")
    q, ckv, kpe, kpe_out = mla_q_prep(z, P["g_mla_q"][l], P["g_mla_kv"][l], w["w_uq"],
                             tabs["mla_cos"], tabs["mla_sin"], tm2, f"mla_q_{tag}")
    z3 = z.reshape(b, s, Z_WIDTH)
    q3 = q.reshape(b, s, -1)
    if past is None:
        k, v = mla_kv_up(ckv, kpe, w["w_uk"], w["w_uv"], tm, f"mla_kv_{tag}")
        o_a = mla_attention(q3, k.reshape(b, s, -1), v.reshape(b, s, -1), _tile(s, 512),
                            f"mla_attn_{tag}")
        init = jnp.zeros((b, RET_HEADS // 2, LANE, RET_DV), F32)
        y_b, state = retention(z3, tabs["ret"], P["g_ret_gn"][l], init, tabs["ret_blk"],
                               f"retention_{tag}")
        o_c = band_attention(z3, Z_CQ // LANE, z3, Z_CK // LANE, z3, Z_CV // LANE,
                             tabs["band_bias"][l], tabs["band_tq"], True, f"band_{tag}")
        n_real = s
        band_k = z3[:, s - BAND_WINDOW:, Z_CK:Z_CK + 512]
        band_v = z3[:, s - BAND_WINDOW:, Z_CV:Z_CV + 512]
    else:
        c_ckv, c_kpe, s_ret, c_bk, c_bv = past
        n_real = CHUNK
        n_past = c_ckv.shape[1]
        kpe_pad = jnp.pad(c_kpe, ((0, 0), (0, 0), (MLA_NOPE, HEAD_SLAB - MLA_NOPE - MLA_ROPE)))
        k_c, v_c = mla_kv_up(c_ckv.reshape(b * n_past, -1), kpe_pad.reshape(b * n_past, -1),
                             w["w_uk"], w["w_uv"], _tile(b * n_past, 1024), f"mla_kv_cache_{tag}")
        k_n, v_n = mla_kv_up(ckv, kpe, w["w_uk"], w["w_uv"], tm, f"mla_kv_{tag}")
        per_b = lambda a: a.reshape(b, -1, a.shape[-1])
        o_a = mla_attention_seg(q3, per_b(k_c), per_b(v_c), per_b(k_n), per_b(v_n), n_real,
                                f"mla_attn_{tag}")
        init = s_ret.astype(F32).reshape(b, RET_HEADS // 2, LANE, RET_DV)
        y_b, state = retention(z3, tabs["ret"], P["g_ret_gn"][l], init, s, f"retention_{tag}")
        band_k = z3[:, :n_real, Z_CK:Z_CK + 512]
        band_v = z3[:, :n_real, Z_CV:Z_CV + 512]
        span = tabs["band_bias"][l].shape[-2]
        w_band = c_bk.shape[1]
        catb = lambda c, n: jnp.pad(
            jnp.concatenate([c.reshape(b, w_band, -1).astype(BF16), n], axis=1),
            ((0, 0), (0, span - w_band - n_real), (0, 0)))
        o_c = band_attention(z3, Z_CQ // LANE, catb(c_bk, band_k), 0, catb(c_bv, band_v), 0,
                             tabs["band_bias"][l], s, False, f"band_{tag}")
    x2 = mix_out(o_a.reshape(m, -1), y_b.reshape(m, -1), o_c.reshape(m, -1), z, P["b_gate"][l],
                 w["w_a"], w["w_b"], w["w_c"], w["w_out"], P["g_post_mix"][l], x2, tm2,
                 f"mix_out_{tag}")
    x3 = mem_attention(x2.reshape(b, s, d), mem_k, mem_v, P["g_pre_mem"][l], w["w_mq"], w["w_mo"],
                       P["g_post_mem"][l], _tile(s, 512), f"mem_attn_{tag}")
    x4 = mlp(x3.reshape(m, d), P["g_pre_ff"][l], w["w_up"], w["w_down"], P["g_post_ff"][l],
             tm, 1024, f"mlp_{tag}")
    new = (ckv.reshape(b, s, -1)[:, :n_real],
           kpe_out.reshape(b, s, -1)[:, :n_real],
           state.reshape(b, RET_HEADS, RET_DK, RET_DV),
           band_k.astype(F32).reshape(b, -1, BAND_HEADS, BAND_HD),
           band_v.astype(F32).reshape(b, -1, BAND_HEADS, BAND_HD))
    return x4.reshape(b, s, d), new


def kernel(x_prompt, x_sample, cache_mla_ckv, cache_mla_kpe, state_ret, cache_band_k, cache_band_v, cache_mem_k, cache_mem_v, mem_prompt, g_pre_mix, w_in, g_mla_q, w_mla_uq, g_mla_kv, w_mla_ukv, g_ret_gn, band_rel_bias, w_br_a, w_br_b, w_br_c, b_gate, w_out, g_post_mix, g_pre_mem, g_mem, w_mem_q, w_mem_k, w_mem_v, w_mem_o, g_post_mem, g_pre_ff, w_up, w_down, g_post_ff):
    P = dict(g_pre_mix=g_pre_mix, g_mla_q=g_mla_q, g_mla_kv=g_mla_kv, g_ret_gn=g_ret_gn,
             b_gate=b_gate, g_post_mix=g_post_mix, g_pre_mem=g_pre_mem, g_post_mem=g_post_mem,
             g_pre_ff=g_pre_ff, g_post_ff=g_post_ff)
    depth = w_in.shape[0]
    bp, sp, d = x_prompt.shape
    bs, ss, _ = x_sample.shape
    n_past = cache_mla_ckv.shape[2]
    w_band = cache_band_k.shape[2]
    assert ss == CHUNK and sp % 512 == 0 and sp >= BAND_WINDOW
    s_pad = 2 * CHUNK

    pos_p = jnp.arange(sp)
    pos_s = n_past + jnp.arange(s_pad)
    ret_blk = 256
    band_tq = 256
    cos_p, sin_p = _mla_rope_tables(pos_p)
    cos_s, sin_s = _mla_rope_tables(pos_s)

    span_p = 3 * band_tq
    jj = np.arange(span_p)[:, None]
    ii = np.arange(band_tq)[None, :]
    band_ok = (jj // CHUNK >= ii // CHUNK) & (jj // CHUNK <= ii // CHUNK + BAND_PREV_CHUNKS)
    bias_p = [_band_bias(band_rel_bias[l], band_tq, span_p, 2 * band_tq, band_ok)
              for l in range(depth)]
    span_s = -(-(w_band + CHUNK) // LANE) * LANE
    mask_s = np.broadcast_to(np.arange(span_s)[:, None] < w_band + CHUNK, (span_s, s_pad))
    bias_s = [_band_bias(band_rel_bias[l], s_pad, span_s, w_band, mask_s) for l in range(depth)]

    tabs_p = dict(mla_cos=cos_p, mla_sin=sin_p, ret=_ret_tables(pos_p, ret_blk, ret_blk),
                  ret_blk=ret_blk, band_bias=bias_p, band_tq=band_tq)
    tabs_s = dict(mla_cos=jnp.tile(cos_s, (bs, 1)), mla_sin=jnp.tile(sin_s, (bs, 1)),
                  ret=_ret_tables(pos_s, s_pad, CHUNK), band_bias=bias_s)

    xp = x_prompt
    xs = jnp.pad(x_sample, ((0, 0), (0, s_pad - ss), (0, 0)))
    mem2 = mem_prompt.reshape(-1, d)
    new_p = [[] for _ in range(7)]
    new_s = [[] for _ in range(5)]
    for l in range(depth):
        w = _layer_weights(l, w_in, w_mla_uq, w_mla_ukv, w_br_a, w_br_b, w_br_c, w_out, w_mem_q,
                           w_mem_k, w_mem_v, w_mem_o, w_up, w_down)
        mkv = norm_matmul(mem2, g_mem[l], w["w_mkv"], F32, _tile(mem2.shape[0], 1024), 512,
                          f"mem_kv_{l}")
        e = MEM_HEADS * MEM_HD
        mk = mkv[:, :e].reshape(bp, -1, e)
        mv = mkv[:, e:].reshape(bp, -1, e)
        xp, st_p = _trunk_layer(xp, w, P, l, tabs_p, mk, mv, None, f"p{l}")
        xs, st_s = _trunk_layer(xs, w, P, l, tabs_s, cache_mem_k[l].reshape(bs, -1, e),
                                cache_mem_v[l].reshape(bs, -1, e),
                                (cache_mla_ckv[l], cache_mla_kpe[l], state_ret[l], cache_band_k[l],
                                 cache_band_v[l]), f"s{l}")
        mem_shape = (bp, -1, MEM_HEADS, MEM_HD)
        for acc, t in zip(new_p, st_p + (mk.reshape(mem_shape), mv.reshape(mem_shape))):
            acc.append(t)
        for acc, t in zip(new_s, st_s):
            acc.append(t)
    stack = lambda ts: jnp.stack(ts, axis=0)
    return (xp, xs[:, :ss],
            stack(new_p[0]), stack(new_p[1]), stack(new_p[2]), stack(new_p[3]), stack(new_p[4]),
            stack(new_p[5]), stack(new_p[6]),
            stack(new_s[0]), stack(new_s[1]), stack(new_s[2]), stack(new_s[3]), stack(new_s[4]))
```

```python
import functools

import numpy as np
import jax
import jax.numpy as jnp
from jax import lax
from jax.experimental import pallas as pl
from jax.experimental.pallas import tpu as pltpu

F32 = jnp.float32
BF16 = jnp.bfloat16

CHUNK = 64
MLA_HEADS = 8
MLA_Q_RANK = 384
MLA_KV_RANK = 256
MLA_NOPE = 64
MLA_ROPE = 32
MLA_V = 64
MLA_SCALE = (MLA_NOPE + MLA_ROPE) ** -0.5
RET_HEADS = 4
RET_DK = 64
RET_DV = 128
BAND_HEADS = 8
BAND_HD = 64
BAND_PREV_CHUNKS = 8
BAND_WINDOW = BAND_PREV_CHUNKS * CHUNK
MAX_REL = 128
MEM_HEADS = 4
MEM_HD = 128
ROPE_THETA = 10000.0
EPS = 1e-6
NEG_INF = -1e30
LOG2E = 1.4426950408889634

LANE = 128
HEAD_SLAB = 128
VMEM_LIMIT = 48 * 1024 * 1024

Z_G = 0
Z_CQ = 3072
Z_CK = 3584
Z_CV = 4096
Z_RV = 4608
Z_RG = 5120
Z_RQ = 5632
Z_RK = 5888
Z_KV = 6144
Z_PE = 6400
Z_Q = 6528
Z_WIDTH = 6912


def _cparams(*sem):
    return pltpu.CompilerParams(dimension_semantics=sem, vmem_limit_bytes=VMEM_LIMIT)


def _rms(x, g):
    return x * lax.rsqrt(jnp.mean(x * x, axis=-1, keepdims=True) + EPS) * g


def _dot(a, b):
    return jnp.dot(a, b, preferred_element_type=F32)


def _dot_nt(a, b):
    return lax.dot_general(a, b, (((1,), (1,)), ((), ())), preferred_element_type=F32)


def _dot_tn(a, b):
    return lax.dot_general(a, b, (((0,), (0,)), ((), ())), preferred_element_type=F32)


def _sigmoid(x):
    return 0.5 * jnp.tanh(0.5 * x) + 0.5


def _lane_iota(shape):
    return lax.broadcasted_iota(jnp.int32, shape, len(shape) - 1)


def _norm_matmul_kernel(x_ref, g_ref, w_ref, o_ref, xn_ref):
    @pl.when(pl.program_id(1) == 0)
    def _():
        xn = _rms(x_ref[...].astype(F32), g_ref[...]).astype(BF16)
        xn_ref[...] = xn
        o_ref[...] = _dot(xn, w_ref[...]).astype(o_ref.dtype)

    @pl.when(pl.program_id(1) > 0)
    def _():
        o_ref[...] = _dot(xn_ref[...], w_ref[...]).astype(o_ref.dtype)


def norm_matmul(x, g, w, out_dtype, tm, tn, name):
    m, k = x.shape
    n = w.shape[1]
    assert m % tm == 0 and n % tn == 0, (m, tm, n, tn)
    return pl.pallas_call(
        _norm_matmul_kernel,
        grid=(m // tm, n // tn),
        in_specs=[
            pl.BlockSpec((tm, k), lambda i, j: (i, 0)),
            pl.BlockSpec((1, k), lambda i, j: (0, 0)),
            pl.BlockSpec((k, tn), lambda i, j: (0, j)),
        ],
        out_specs=pl.BlockSpec((tm, tn), lambda i, j: (i, j)),
        out_shape=jax.ShapeDtypeStruct((m, n), out_dtype),
        scratch_shapes=[pltpu.VMEM((tm, k), BF16)],
        compiler_params=_cparams("parallel", "arbitrary"),
        name=name,
    )(x, g.reshape(1, k), w)


def _rope_slab(x, cos, sin, rot):
    first_end, half, period = rot
    width = x.shape[1]
    right = pltpu.roll(x, width - half, 1)
    left = pltpu.roll(x, half, 1)
    partner = jnp.where((_lane_iota(x.shape) & (period - 1)) < first_end, right, left)
    return x * cos + partner * sin


_MLA_ROT = (MLA_NOPE + MLA_ROPE // 2, MLA_ROPE // 2, LANE)
_RET_ROT = (RET_DK // 2, RET_DK // 2, RET_DK)


def _mla_q_kernel(zq_ref, zkv_ref, zpe_ref, gq_ref, gkv_ref, wq_ref, cos_ref, sin_ref,
                  q_ref, ckv_ref, kpe_ref, kpe_out_ref):
    cos = cos_ref[...]
    sin = sin_ref[...]
    qn = _rms(zq_ref[...].astype(F32), gq_ref[...]).astype(BF16)
    q = _dot(qn, wq_ref[...])
    cos_q = cos * (MLA_SCALE * LOG2E)
    sin_q = sin * (MLA_SCALE * LOG2E)
    for h in range(MLA_HEADS):
        sl = slice(h * HEAD_SLAB, (h + 1) * HEAD_SLAB)
        pt = slice((MLA_HEADS + h) * HEAD_SLAB, (MLA_HEADS + h + 1) * HEAD_SLAB)
        q_ref[:, sl] = (q[:, sl] * cos_q + q[:, pt] * sin_q).astype(BF16)
    ckv_ref[...] = _rms(zkv_ref[...].astype(F32), gkv_ref[...])
    kpe = _rope_slab(zpe_ref[...].astype(F32), cos, sin, _MLA_ROT)
    kpe_ref[...] = kpe
    kpe_out_ref[...] = kpe[:, MLA_NOPE:MLA_NOPE + MLA_ROPE]


def mla_q_prep(z, g_q, g_kv, w_uq, cos, sin, tm, name):
    m = z.shape[0]
    nt = cos.shape[0] // tm
    row = lambda w: pl.BlockSpec((1, w), lambda i: (0, 0))
    return pl.pallas_call(
        _mla_q_kernel,
        grid=(m // tm,),
        in_specs=[
            pl.BlockSpec((tm, MLA_Q_RANK), lambda i: (i, Z_Q // MLA_Q_RANK)),
            pl.BlockSpec((tm, MLA_KV_RANK), lambda i: (i, Z_KV // MLA_KV_RANK)),
            pl.BlockSpec((tm, HEAD_SLAB), lambda i: (i, Z_PE // HEAD_SLAB)),
            row(MLA_Q_RANK), row(MLA_KV_RANK),
            pl.BlockSpec(w_uq.shape, lambda i: (0, 0)),
            pl.BlockSpec((tm, HEAD_SLAB), lambda i: (i % nt, 0)),
            pl.BlockSpec((tm, HEAD_SLAB), lambda i: (i % nt, 0)),
        ],
        out_specs=[
            pl.BlockSpec((tm, MLA_HEADS * HEAD_SLAB), lambda i: (i, 0)),
            pl.BlockSpec((tm, MLA_KV_RANK), lambda i: (i, 0)),
            pl.BlockSpec((tm, HEAD_SLAB), lambda i: (i, 0)),
            pl.BlockSpec((tm, MLA_ROPE), lambda i: (i, 0)),
        ],
        out_shape=[
            jax.ShapeDtypeStruct((m, MLA_HEADS * HEAD_SLAB), BF16),
            jax.ShapeDtypeStruct((m, MLA_KV_RANK), F32),
            jax.ShapeDtypeStruct((m, HEAD_SLAB), F32),
            jax.ShapeDtypeStruct((m, MLA_ROPE), F32),
        ],
        compiler_params=_cparams("parallel"),
        name=name,
    )(z, z, z, g_q.reshape(1, -1), g_kv.reshape(1, -1), w_uq, cos, sin)


def _mla_kv_kernel(ckv_ref, kpe_ref, wk_ref, wv_ref, k_ref, v_ref):
    c = ckv_ref[...].astype(BF16)
    kn = _dot(c, wk_ref[...])
    kpe = kpe_ref[...]
    for h in range(MLA_HEADS):
        sl = slice(h * HEAD_SLAB, (h + 1) * HEAD_SLAB)
        k_ref[:, sl] = (kn[:, sl] + kpe).astype(BF16)
    v = _dot(c, wv_ref[...])
    ones_lane = (_lane_iota(v.shape) & (HEAD_SLAB - 1)) == MLA_V
    v_ref[...] = jnp.where(ones_lane, 1.0, v).astype(BF16)


def mla_kv_up(ckv, kpe, w_uk, w_uv, tm, name):
    m = ckv.shape[0]
    return pl.pallas_call(
        _mla_kv_kernel,
        grid=(m // tm,),
        in_specs=[
            pl.BlockSpec((tm, MLA_KV_RANK), lambda i: (i, 0)),
            pl.BlockSpec((tm, HEAD_SLAB), lambda i: (i, 0)),
            pl.BlockSpec(w_uk.shape, lambda i: (0, 0)),
            pl.BlockSpec(w_uv.shape, lambda i: (0, 0)),
        ],
        out_specs=[
            pl.BlockSpec((tm, MLA_HEADS * HEAD_SLAB), lambda i: (i, 0)),
            pl.BlockSpec((tm, MLA_HEADS * HEAD_SLAB), lambda i: (i, 0)),
        ],
        out_shape=[
            jax.ShapeDtypeStruct((m, MLA_HEADS * HEAD_SLAB), BF16),
            jax.ShapeDtypeStruct((m, MLA_HEADS * HEAD_SLAB), BF16),
        ],
        compiler_params=_cparams("parallel"),
        name=name,
    )(ckv, kpe, w_uk, w_uv)


def _head_of_pair(x, hh):
    lane = _lane_iota(x.shape)
    keep = (lane < BAND_HD) if hh == 0 else (lane >= BAND_HD)
    return jnp.where(keep, x, jnp.zeros_like(x))


_PAIR_SLABS = [slice(hh * HEAD_SLAB, (hh + 1) * HEAD_SLAB) for hh in range(2)]


def _mla_out(acc):
    return acc[:MLA_V, :] / acc[MLA_V:MLA_V + 1, :]


def _mla_attn_kernel(q_ref, k_ref, v_ref, o_ref, acc_ref, sa_ref, sb_ref, *, t):
    qi = pl.program_id(2)
    qs = [q_ref[0, :, sl] for sl in _PAIR_SLABS]

    def rows_of(kb):
        return pl.ds(kb * t if isinstance(kb, int) else pl.multiple_of(kb * t, t), t)

    def scores(kb, s_ref):
        for hh, sl in enumerate(_PAIR_SLABS):
            s_ref[hh] = _dot_nt(k_ref[0, rows_of(kb), sl], qs[hh])

    def consume(kb, s_ref, ms, mask):
        new_m = []
        for hh, sl in enumerate(_PAIR_SLABS):
            s = s_ref[hh]
            if mask is not None:
                s = jnp.where(mask, s, NEG_INF)
            m = jnp.maximum(ms[hh], jnp.max(s, axis=0, keepdims=True))
            alpha = jnp.exp2(ms[hh] - m)
            p = jnp.exp2(s - m).astype(BF16)
            acc_ref[hh] = alpha * acc_ref[hh] + _dot_tn(v_ref[0, rows_of(kb), sl], p)
            new_m.append(m)
        return tuple(new_m)

    def finish():
        outs = [_mla_out(acc_ref[hh]) for hh in range(2)]
        o_ref[0] = jnp.concatenate(outs, axis=0).T.astype(o_ref.dtype)

    acc_ref[...] = jnp.zeros_like(acc_ref)
    ms = tuple(jnp.full((1, t), NEG_INF, F32) for _ in range(2))
    shift = CHUNK.bit_length() - 1
    key_chunk = jnp.right_shift(lax.broadcasted_iota(jnp.int32, (t, t), 0), shift)
    qry_chunk = jnp.right_shift(lax.broadcasted_iota(jnp.int32, (t, t), 1), shift)
    diag = key_chunk <= qry_chunk
    scores(0, sa_ref)

    def pair(j, ms):
        kb = 2 * j
        scores(kb + 1, sb_ref)
        ms = consume(kb, sa_ref, ms, None)
        scores(kb + 2, sa_ref)
        return consume(kb + 1, sb_ref, ms, None)

    ms = lax.fori_loop(0, qi // 2, pair, ms)

    @pl.when(qi % 2 == 0)
    def _():
        consume(qi, sa_ref, ms, diag)
        finish()

    @pl.when(qi % 2 == 1)
    def _():
        scores(qi, sb_ref)
        consume(qi, sb_ref, consume(qi - 1, sa_ref, ms, None), diag)
        finish()


def _mla_attn_seg_kernel(q_ref, k1_ref, v1_ref, k2_ref, v2_ref, o_ref, *, n2_valid):
    outs = []
    for sl in _PAIR_SLABS:
        q = q_ref[0, :, sl]
        s1 = _dot_nt(k1_ref[0, :, sl], q)
        s2 = _dot_nt(k2_ref[0, :, sl], q)
        s2 = jnp.where(lax.broadcasted_iota(jnp.int32, s2.shape, 0) < n2_valid, s2, NEG_INF)
        m = jnp.maximum(jnp.max(s1, axis=0, keepdims=True), jnp.max(s2, axis=0, keepdims=True))
        acc = (_dot_tn(v1_ref[0, :, sl], jnp.exp2(s1 - m).astype(BF16))
               + _dot_tn(v2_ref[0, :, sl], jnp.exp2(s2 - m).astype(BF16)))
        outs.append(_mla_out(acc))
    o_ref[0] = jnp.concatenate(outs, axis=0).T.astype(o_ref.dtype)


def mla_attention_seg(q, k1, v1, k2, v2, n2_valid, name):
    b, sq, _ = q.shape
    blk = lambda a: pl.BlockSpec((1, a.shape[1], 2 * HEAD_SLAB), lambda b_, h: (b_, 0, h))
    return pl.pallas_call(
        functools.partial(_mla_attn_seg_kernel, n2_valid=n2_valid),
        grid=(b, MLA_HEADS // 2),
        in_specs=[blk(q), blk(k1), blk(v1), blk(k2), blk(v2)],
        out_specs=pl.BlockSpec((1, sq, 2 * MLA_V), lambda b_, h: (b_, 0, h)),
        out_shape=jax.ShapeDtypeStruct((b, sq, MLA_HEADS * MLA_V), BF16),
        compiler_params=_cparams("parallel", "parallel"),
        name=name,
    )(q, k1, v1, k2, v2)


def mla_attention(q, k, v, t, name):
    b, s, _ = q.shape
    assert s % t == 0 and k.shape[1] == s
    whole = pl.BlockSpec((1, s, 2 * HEAD_SLAB), lambda b_, h, i: (b_, 0, h))
    return pl.pallas_call(
        functools.partial(_mla_attn_kernel, t=t),
        grid=(b, MLA_HEADS // 2, s // t),
        in_specs=[pl.BlockSpec((1, t, 2 * HEAD_SLAB), lambda b_, h, i: (b_, i, h)), whole, whole],
        out_specs=pl.BlockSpec((1, t, 2 * MLA_V), lambda b_, h, i: (b_, i, h)),
        out_shape=jax.ShapeDtypeStruct((b, s, MLA_HEADS * MLA_V), BF16),
        scratch_shapes=[pltpu.VMEM((2, HEAD_SLAB, t), F32), pltpu.VMEM((2, t, t), F32),
                        pltpu.VMEM((2, t, t), F32)],
        compiler_params=_cparams("parallel", "parallel", "arbitrary"),
        name=name,
    )(q, k, v)


def _band_attn_kernel(q_ref, k_ref, v_ref, bias_ref, o_ref, sa_ref, sb_ref, *, tq, nq, span,
                      sliding):
    def geom(i):
        if not sliding:
            return 0, span, 0
        if isinstance(i, int):
            n = min(i + 1, span // tq) * tq
            return max(i + 1 - span // tq, 0) * tq, n, span - n
        return pl.multiple_of((i + 1 - span // tq) * tq, tq), span, 0

    def q_rows(i):
        return pl.ds(i * tq if isinstance(i, int) else pl.multiple_of(i * tq, tq), tq)

    def scores(i, s_ref):
        start, n, boff = geom(i)
        q = q_ref[0, q_rows(i), :]
        k = k_ref[0, pl.ds(start, n), :]
        for hh in range(2):
            s_ref[hh, :n, :] = _dot_nt(k, _head_of_pair(q, hh)) + bias_ref[0, hh, boff:boff + n, :]

    def consume(i, s_ref):
        start, n, _ = geom(i)
        v = v_ref[0, pl.ds(start, n), :]
        outs = []
        for hh in range(2):
            s = s_ref[hh, :n, :]
            m = jnp.max(s, axis=0, keepdims=True)
            p = jnp.exp2(s - m)
            l = jnp.sum(p, axis=0, keepdims=True)
            outs.append(_dot_tn(v, p.astype(BF16)) / l)
        first = lax.broadcasted_iota(jnp.int32, outs[0].shape, 0) < BAND_HD
        o_ref[0, q_rows(i), :] = jnp.where(first, outs[0], outs[1]).T.astype(o_ref.dtype)

    scores(0, sa_ref)
    if nq == 1:
        consume(0, sa_ref)
        return
    scores(1, sb_ref)
    consume(0, sa_ref)
    scores(2, sa_ref)
    consume(1, sb_ref)

    def pair(j, carry):
        i = 2 * j + 2
        scores(i + 1, sb_ref)
        consume(i, sa_ref)
        scores(i + 2, sa_ref)
        consume(i + 1, sb_ref)
        return carry

    lax.fori_loop(0, (nq - 2) // 2 - 1, pair, 0)
    scores(nq - 1, sb_ref)
    consume(nq - 2, sa_ref)
    consume(nq - 1, sb_ref)


def band_attention(q, q_col, k, k_col, v, v_col, bias, tq, sliding, name):
    b, sq = q.shape[:2]
    sk = k.shape[1]
    nq = sq // tq
    span = bias.shape[-2]
    assert (sliding and sk == sq and nq >= 4 and nq % 2 == 0) or (nq == 1 and sk == span)
    col = lambda c: (lambda hp, b_: (b_, 0, c + hp))
    return pl.pallas_call(
        functools.partial(_band_attn_kernel, tq=tq, nq=nq, span=span, sliding=sliding),
        grid=(BAND_HEADS // 2, b),
        in_specs=[
            pl.BlockSpec((1, sq, LANE), col(q_col)),
            pl.BlockSpec((1, sk, LANE), col(k_col)),
            pl.BlockSpec((1, sk, LANE), col(v_col)),
            pl.BlockSpec((1, 2, span, tq), lambda hp, b_: (0, hp, 0, 0)),
        ],
        out_specs=pl.BlockSpec((1, sq, LANE), col(0)),
        out_shape=jax.ShapeDtypeStruct((b, sq, BAND_HEADS * BAND_HD), BF16),
        scratch_shapes=[pltpu.VMEM((2, span, tq), F32), pltpu.VMEM((2, span, tq), F32)],
        compiler_params=_cparams("parallel", "arbitrary"),
        name=name,
    )(q, k, v, bias.reshape((1,) + bias.shape))


def _retention_kernel(rq_ref, rk_ref, rv_ref, rg_ref, cq_ref, sq_ref, ck_ref, sk_ref,
                      dmat_ref, rowdec_ref, kw_ref, sdec_ref, gn_ref, init_ref,
                      y_ref, state_out_ref, state_ref):
    c = pl.program_id(1)

    @pl.when(c == 0)
    def _():
        state_ref[...] = init_ref[...]

    n_pair = RET_HEADS // 2
    gn = gn_ref[...]
    row_is_first = lax.broadcasted_iota(jnp.int32, (LANE, RET_DV), 0) < RET_DK
    for bi in range(rq_ref.shape[0]):
        q = _rope_slab(rq_ref[bi].astype(F32), cq_ref[...], sq_ref[...], _RET_ROT)
        k = _rope_slab(rk_ref[bi].astype(F32), ck_ref[...], sk_ref[...], _RET_ROT)
        qb = q.astype(BF16)
        kb = k.astype(BF16)
        kwb = (k * kw_ref[...]).astype(BF16)
        for p in range(n_pair):
            psl = slice(p * LANE, (p + 1) * LANE)
            st = state_ref[bi, p]
            stb = st.astype(BF16)
            kv = []
            for hh in range(2):
                h = 2 * p + hh
                vsl = slice(h * RET_DV, (h + 1) * RET_DV)
                v = rv_ref[bi, :, vsl]
                qh = _head_of_pair(qb[:, psl], hh)
                s = _dot_nt(qh, kb[:, psl]) * dmat_ref[h]
                o = _dot(s.astype(BF16), v) + _dot(qh, stb) * rowdec_ref[:, vsl]
                mu = jnp.mean(o, axis=-1, keepdims=True)
                d = o - mu
                yn = d * lax.rsqrt(jnp.mean(d * d, axis=-1, keepdims=True) + EPS)
                g = rg_ref[bi, :, vsl].astype(F32)
                y_ref[bi, :, vsl] = (yn * gn[:, vsl] * (g * _sigmoid(g))).astype(y_ref.dtype)
                kv.append(_dot_tn(kwb[:, psl], v))
            state_ref[bi, p] = st * sdec_ref[p] + jnp.where(row_is_first, kv[0], kv[1])

    @pl.when(c == pl.num_programs(1) - 1)
    def _():
        state_out_ref[...] = state_ref[...]


def retention(z, tabs, g_gn, init_state, blk, name):
    b, s = z.shape[:2]
    nc = s // blk
    nb = 2 if b % 2 == 0 else 1
    n_pair = RET_HEADS // 2
    zblk = lambda w, col: pl.BlockSpec((nb, blk, w), lambda b_, c: (b_, c, col // w))
    tab = lambda w: pl.BlockSpec((blk, w), lambda b_, c: (c, 0))
    const = lambda a: pl.BlockSpec(a.shape, lambda b_, c: (0,) * a.ndim)
    qk_w = RET_HEADS * RET_DK
    v_w = RET_HEADS * RET_DV
    return pl.pallas_call(
        _retention_kernel,
        grid=(b // nb, nc),
        in_specs=[
            zblk(qk_w, Z_RQ), zblk(qk_w, Z_RK), zblk(v_w, Z_RV), zblk(v_w, Z_RG),
            tab(qk_w), tab(qk_w), tab(qk_w), tab(qk_w),
            const(tabs["dmat"]), const(tabs["rowdec"]), const(tabs["kw"]), const(tabs["sdec"]),
            pl.BlockSpec((1, v_w), lambda b_, c: (0, 0)),
            pl.BlockSpec((nb, n_pair, LANE, RET_DV), lambda b_, c: (b_, 0, 0, 0)),
        ],
        out_specs=[
            pl.BlockSpec((nb, blk, v_w), lambda b_, c: (b_, c, 0)),
            pl.BlockSpec((nb, n_pair, LANE, RET_DV), lambda b_, c: (b_, 0, 0, 0)),
        ],
        out_shape=[
            jax.ShapeDtypeStruct((b, s, v_w), BF16),
            jax.ShapeDtypeStruct((b, n_pair, LANE, RET_DV), F32),
        ],
        scratch_shapes=[pltpu.VMEM((nb, n_pair, LANE, RET_DV), F32)],
        compiler_params=_cparams("parallel", "arbitrary"),
        name=name,
    )(z, z, z, z, tabs["cos_q"], tabs["sin_q"], tabs["cos_k"], tabs["sin_k"],
      tabs["dmat"], tabs["rowdec"], tabs["kw"], tabs["sdec"], g_gn.reshape(1, -1), init_state)


def _mix_out_kernel(oa_ref, yb_ref, oc_ref, zg0_ref, zg1_ref, zg2_ref, bg_ref, wa_ref, wb_ref,
                    wc_ref, wo_ref, g_ref, x_ref, o_ref):
    d = x_ref.shape[-1]
    merged = None
    for n, (br_ref, w_ref, zg_ref) in enumerate(
            ((oa_ref, wa_ref, zg0_ref), (yb_ref, wb_ref, zg1_ref), (oc_ref, wc_ref, zg2_ref))):
        gate = _sigmoid(zg_ref[...].astype(F32) + bg_ref[:, n * d:(n + 1) * d])
        term = gate * _dot(br_ref[...], w_ref[...])
        merged = term if merged is None else merged + term
    y = _dot(merged.astype(BF16), wo_ref[...])
    o_ref[...] = x_ref[...] + _rms(y, g_ref[...])


def mix_out(o_a, y_b, o_c, z, b_gate, w_a, w_b, w_c, w_out, g_post, x, tm, name):
    m, d = x.shape
    e = o_a.shape[1]
    act = pl.BlockSpec((tm, e), lambda i: (i, 0))
    zg = lambda n: pl.BlockSpec((tm, d), lambda i: (i, Z_G // d + n))
    const = lambda a: pl.BlockSpec(a.shape, lambda i: (0, 0))
    return pl.pallas_call(
        _mix_out_kernel,
        grid=(m // tm,),
        in_specs=[act, act, act, zg(0), zg(1), zg(2),
                  pl.BlockSpec((1, 3 * d), lambda i: (0, 0)),
                  const(w_a), const(w_b), const(w_c), const(w_out),
                  pl.BlockSpec((1, d), lambda i: (0, 0)),
                  pl.BlockSpec((tm, d), lambda i: (i, 0))],
        out_specs=pl.BlockSpec((tm, d), lambda i: (i, 0)),
        out_shape=jax.ShapeDtypeStruct((m, d), F32),
        compiler_params=_cparams("parallel"),
        name=name,
    )(o_a, y_b, o_c, z, z, z, b_gate.reshape(1, -1), w_a, w_b, w_c, w_out,
      g_post.reshape(1, -1), x)


def _mem_attn_kernel(x_ref, mk_ref, mv_ref, gpre_ref, wq_ref, wo_ref, gpost_ref, o_ref):
    x = x_ref[0]
    u = _rms(x, gpre_ref[...]).astype(BF16)
    q = _dot(u, wq_ref[...]).astype(BF16)
    outs = []
    for h in range(MEM_HEADS):
        sl = slice(h * MEM_HD, (h + 1) * MEM_HD)
        s = _dot_nt(q[:, sl], mk_ref[0, :, sl].astype(BF16)) * (MEM_HD ** -0.5 * LOG2E)
        m = jnp.max(s, axis=-1, keepdims=True)
        p = jnp.exp2(s - m)
        l = jnp.sum(p, axis=-1, keepdims=True)
        outs.append(_dot(p.astype(BF16), mv_ref[0, :, sl].astype(BF16)) / l)
    o = jnp.concatenate(outs, axis=-1).astype(BF16)
    o_ref[0] = x + _rms(_dot(o, wo_ref[...]), gpost_ref[...])


def mem_attention(x, mk, mv, g_pre, w_q, w_o, g_post, tm, name):
    b, s, d = x.shape
    const = lambda a: pl.BlockSpec(a.shape, lambda b_, i: (0, 0))
    vec = pl.BlockSpec((1, d), lambda b_, i: (0, 0))
    mem = pl.BlockSpec((1,) + mk.shape[1:], lambda b_, i: (b_, 0, 0))
    return pl.pallas_call(
        _mem_attn_kernel,
        grid=(b, s // tm),
        in_specs=[pl.BlockSpec((1, tm, d), lambda b_, i: (b_, i, 0)), mem, mem,
                  vec, const(w_q), const(w_o), vec],
        out_specs=pl.BlockSpec((1, tm, d), lambda b_, i: (b_, i, 0)),
        out_shape=jax.ShapeDtypeStruct((b, s, d), F32),
        compiler_params=_cparams("parallel", "parallel"),
        name=name,
    )(x, mk, mv, g_pre.reshape(1, -1), w_q, w_o, g_post.reshape(1, -1))


def _mlp_kernel(x_ref, gpre_ref, wu_ref, wd_ref, gpost_ref, o_ref, xn_ref, acc_ref):
    j = pl.program_id(1)
    last = pl.num_programs(1) - 1

    def partial_out(xn):
        h = jnp.square(jnp.maximum(_dot(xn, wu_ref[...]), 0.0))
        return _dot(h.astype(BF16), wd_ref[...])

    @pl.when(j == 0)
    def _():
        xn = _rms(x_ref[...], gpre_ref[...]).astype(BF16)
        xn_ref[...] = xn
        acc_ref[...] = partial_out(xn)

    @pl.when(jnp.logical_and(j > 0, j < last))
    def _():
        acc_ref[...] += partial_out(xn_ref[...])

    @pl.when(j == last)
    def _():
        y = acc_ref[...] + partial_out(xn_ref[...])
        o_ref[...] = x_ref[...] + _rms(y, gpost_ref[...])


def mlp(x, g_pre, w_up, w_down, g_post, tm, tf, name):
    m, d = x.shape
    f = w_up.shape[1]
    vec = pl.BlockSpec((1, d), lambda i, j: (0, 0))
    return pl.pallas_call(
        _mlp_kernel,
        grid=(m // tm, f // tf),
        in_specs=[pl.BlockSpec((tm, d), lambda i, j: (i, 0)), vec,
                  pl.BlockSpec((d, tf), lambda i, j: (0, j)),
                  pl.BlockSpec((tf, d), lambda i, j: (j, 0)), vec],
        out_specs=pl.BlockSpec((tm, d), lambda i, j: (i, 0)),
        out_shape=jax.ShapeDtypeStruct((m, d), F32),
        scratch_shapes=[pltpu.VMEM((tm, d), BF16), pltpu.VMEM((tm, d), F32)],
        compiler_params=_cparams("parallel", "arbitrary"),
        name=name,
    )(x, g_pre.reshape(1, -1), w_up, w_down, g_post.reshape(1, -1))


def _rope_angles(pos, half):
    inv = ROPE_THETA ** (-jnp.arange(half, dtype=F32) / half)
    ang = pos.astype(F32)[:, None] * inv[None, :]
    return jnp.cos(ang), jnp.sin(ang)


def _mla_rope_tables(pos):
    cos, sin = _rope_angles(pos, MLA_ROPE // 2)
    t = pos.shape[0]
    one = jnp.ones((t, MLA_NOPE), F32)
    zero64 = jnp.zeros((t, MLA_NOPE), F32)
    pad = jnp.zeros((t, HEAD_SLAB - MLA_NOPE - MLA_ROPE), F32)
    return (jnp.concatenate([one, cos, cos, pad], axis=1),
            jnp.concatenate([zero64, -sin, sin, pad], axis=1))


def _ret_tables(pos, blk, n_real):
    cos, sin = _rope_angles(pos, RET_DK // 2)
    cos_q = jnp.tile(jnp.concatenate([cos, cos], axis=1), (1, RET_HEADS))
    sin_q = jnp.tile(jnp.concatenate([-sin, sin], axis=1), (1, RET_HEADS))
    k_scale = RET_DK ** -0.5
    log_g = jnp.log1p(-jnp.exp2(-5.0 - jnp.arange(RET_HEADS, dtype=F32)))
    idx = jnp.arange(blk, dtype=F32)
    diff = idx[:, None] - idx[None, :]
    dmat = jnp.where(diff >= 0, jnp.exp(log_g[:, None, None] * jnp.maximum(diff, 0.0)), 0.0)
    rowdec = jnp.exp(log_g[None, :] * (idx[:, None] + 1.0))
    w = jnp.where(idx[:, None] < n_real,
                  jnp.exp(log_g[None, :] * jnp.maximum(n_real - 1.0 - idx[:, None], 0.0)), 0.0)
    sdec = jnp.exp(log_g * n_real)
    n_pair = RET_HEADS // 2
    return dict(
        cos_q=cos_q, sin_q=sin_q, cos_k=cos_q * k_scale, sin_k=sin_q * k_scale,
        dmat=dmat,
        rowdec=jnp.repeat(rowdec, RET_DV, axis=1),
        kw=jnp.repeat(w, RET_DK, axis=1),
        sdec=jnp.broadcast_to(jnp.repeat(sdec, RET_DK).reshape(n_pair, LANE, 1),
                              (n_pair, LANE, RET_DV)),
    )


def _band_bias(table, tq, span, q_off, allowed):
    sub = 8
    assert span % sub == 0
    length = tq + span - 1
    d = np.arange(length)
    ext = table[:, np.clip(d + q_off - span + 1, -MAX_REL, MAX_REL) + MAX_REL].astype(F32) * LOG2E
    shifted = jnp.stack([ext[:, sub - 1 - r:length - r] for r in range(sub)], axis=1)
    tile = jnp.concatenate([shifted[:, :, span - sub * (a + 1):span - sub * (a + 1) + tq]
                            for a in range(span // sub)], axis=1)
    return jnp.where(allowed[None], tile, NEG_INF)


def _layer_weights(l, w_in, w_mla_uq, w_mla_ukv, w_br_a, w_br_b, w_br_c, w_out, w_mem_q, w_mem_k,
                   w_mem_v, w_mem_o, w_up, w_down):
    d = w_in.shape[1]
    parts, start = [], 0
    for n in (MLA_Q_RANK, MLA_KV_RANK, MLA_ROPE, 256, 256, 512, 512, 512, 512, 512, 3 * d):
        parts.append(w_in[l, :, start:start + n])
        start += n
    zq, zkv, zpe, rq, rk, rv, rg, cq, ck, cv, zg = parts
    cq = cq * (BAND_HD ** -0.5 * LOG2E)
    zeros = lambda n: jnp.zeros((d, n), w_in.dtype)
    w_in_l = jnp.concatenate(
        [zg, cq, ck, cv, rv, rg, rq, rk, zkv, zeros(MLA_NOPE), zpe,
         zeros(HEAD_SLAB - MLA_NOPE - MLA_ROPE), zq], axis=1).astype(BF16)
    assert w_in_l.shape[1] == Z_WIDTH
    pad_head = lambda w: jnp.pad(w, ((0, 0), (0, 0), (0, HEAD_SLAB - w.shape[-1])))
    flat = lambda w: w.reshape(w.shape[0], -1).astype(BF16)
    return dict(
        w_in=w_in_l,
        w_uq=jnp.concatenate([flat(pad_head(w_mla_uq[l])), flat(pad_head(jnp.concatenate(
            [jnp.zeros_like(w_mla_uq[l][..., :MLA_NOPE]),
             w_mla_uq[l][..., MLA_NOPE + MLA_ROPE // 2:],
             w_mla_uq[l][..., MLA_NOPE:MLA_NOPE + MLA_ROPE // 2]], axis=-1)))], axis=1),
        w_uk=flat(pad_head(w_mla_ukv[l][..., :MLA_NOPE])),
        w_uv=flat(pad_head(w_mla_ukv[l][..., MLA_NOPE:])),
        w_a=w_br_a[l].astype(BF16), w_b=w_br_b[l].astype(BF16), w_c=w_br_c[l].astype(BF16),
        w_out=w_out[l].astype(BF16),
        w_mq=flat(w_mem_q[l]),
        w_mkv=jnp.concatenate([flat(w_mem_k[l]), flat(w_mem_v[l])], axis=1),
        w_mo=w_mem_o[l].reshape(-1, d).astype(BF16),
        w_up=w_up[l].astype(BF16), w_down=w_down[l].astype(BF16),
    )


def _tile(n, pref):
    t = min(n, pref)
    while n % t:
        t -= LANE
    return t


def _trunk_layer(x, w, P, l, tabs, mem_k, mem_v, past, tag):
    b, s, d = x.shape
    m = b * s
    x2 = x.reshape(m, d)
    tm = _tile(m, 1024)
    tm2 = _tile(m, 512)
    z = norm_matmul(x2, P["g_pre_mix"][l], w["w_in"], BF16, tm, 2304, f"in_proj_{tag}")
    q, ckv, kpe, kpe_out = mla_q_prep(z, P["g_mla_q"][l], P["g_mla_kv"][l], w["w_uq"],
                             tabs["mla_cos"], tabs["mla_sin"], tm2, f"mla_q_{tag}")
    z3 = z.reshape(b, s, Z_WIDTH)
    q3 = q.reshape(b, s, -1)
    if past is None:
        k, v = mla_kv_up(ckv, kpe, w["w_uk"], w["w_uv"], tm, f"mla_kv_{tag}")
        o_a = mla_attention(q3, k.reshape(b, s, -1), v.reshape(b, s, -1), _tile(s, 512),
                            f"mla_attn_{tag}")
        init = jnp.zeros((b, RET_HEADS // 2, LANE, RET_DV), F32)
        y_b, state = retention(z3, tabs["ret"], P["g_ret_gn"][l], init, tabs["ret_blk"],
                               f"retention_{tag}")
        o_c = band_attention(z3, Z_CQ // LANE, z3, Z_CK // LANE, z3, Z_CV // LANE,
                             tabs["band_bias"][l], tabs["band_tq"], True, f"band_{tag}")
        n_real = s
        band_k = z3[:, s - BAND_WINDOW:, Z_CK:Z_CK + 512]
        band_v = z3[:, s - BAND_WINDOW:, Z_CV:Z_CV + 512]
    else:
        c_ckv, c_kpe, s_ret, c_bk, c_bv = past
        n_real = CHUNK
        n_past = c_ckv.shape[1]
        kpe_pad = jnp.pad(c_kpe, ((0, 0), (0, 0), (MLA_NOPE, HEAD_SLAB - MLA_NOPE - MLA_ROPE)))
        k_c, v_c = mla_kv_up(c_ckv.reshape(b * n_past, -1), kpe_pad.reshape(b * n_past, -1),
                             w["w_uk"], w["w_uv"], _tile(b * n_past, 1024), f"mla_kv_cache_{tag}")
        k_n, v_n = mla_kv_up(ckv, kpe, w["w_uk"], w["w_uv"], tm, f"mla_kv_{tag}")
        per_b = lambda a: a.reshape(b, -1, a.shape[-1])
        o_a = mla_attention_seg(q3, per_b(k_c), per_b(v_c), per_b(k_n), per_b(v_n), n_real,
                                f"mla_attn_{tag}")
        init = s_ret.astype(F32).reshape(b, RET_HEADS // 2, LANE, RET_DV)
        y_b, state = retention(z3, tabs["ret"], P["g_ret_gn"][l], init, s, f"retention_{tag}")
        band_k = z3[:, :n_real, Z_CK:Z_CK + 512]
        band_v = z3[:, :n_real, Z_CV:Z_CV + 512]
        span = tabs["band_bias"][l].shape[-2]
        w_band = c_bk.shape[1]
        catb = lambda c, n: jnp.pad(
            jnp.concatenate([c.reshape(b, w_band, -1).astype(BF16), n], axis=1),
            ((0, 0), (0, span - w_band - n_real), (0, 0)))
        o_c = band_attention(z3, Z_CQ // LANE, catb(c_bk, band_k), 0, catb(c_bv, band_v), 0,
                             tabs["band_bias"][l], s, False, f"band_{tag}")
    x2 = mix_out(o_a.reshape(m, -1), y_b.reshape(m, -1), o_c.reshape(m, -1), z, P["b_gate"][l],
                 w["w_a"], w["w_b"], w["w_c"], w["w_out"], P["g_post_mix"][l], x2, tm2,
                 f"mix_out_{tag}")
    x3 = mem_attention(x2.reshape(b, s, d), mem_k, mem_v, P["g_pre_mem"][l], w["w_mq"], w["w_mo"],
                       P["g_post_mem"][l], _tile(s, 512), f"mem_attn_{tag}")
    x4 = mlp(x3.reshape(m, d), P["g_pre_ff"][l], w["w_up"], w["w_down"], P["g_post_ff"][l],
             tm, 1024, f"mlp_{tag}")
    new = (ckv.reshape(b, s, -1)[:, :n_real],
           kpe_out.reshape(b, s, -1)[:, :n_real],
           state.reshape(b, RET_HEADS, RET_DK, RET_DV),
           band_k.astype(F32).reshape(b, -1, BAND_HEADS, BAND_HD),
           band_v.astype(F32).reshape(b, -1, BAND_HEADS, BAND_HD))
    return x4.reshape(b, s, d), new


def kernel(x_prompt, x_sample, cache_mla_ckv, cache_mla_kpe, state_ret, cache_band_k, cache_band_v, cache_mem_k, cache_mem_v, mem_prompt, g_pre_mix, w_in, g_mla_q, w_mla_uq, g_mla_kv, w_mla_ukv, g_ret_gn, band_rel_bias, w_br_a, w_br_b, w_br_c, b_gate, w_out, g_post_mix, g_pre_mem, g_mem, w_mem_q, w_mem_k, w_mem_v, w_mem_o, g_post_mem, g_pre_ff, w_up, w_down, g_post_ff):
    P = dict(g_pre_mix=g_pre_mix, g_mla_q=g_mla_q, g_mla_kv=g_mla_kv, g_ret_gn=g_ret_gn,
             b_gate=b_gate, g_post_mix=g_post_mix, g_pre_mem=g_pre_mem, g_post_mem=g_post_mem,
             g_pre_ff=g_pre_ff, g_post_ff=g_post_ff)
    depth = w_in.shape[0]
    bp, sp, d = x_prompt.shape
    bs, ss, _ = x_sample.shape
    n_past = cache_mla_ckv.shape[2]
    w_band = cache_band_k.shape[2]
    assert ss == CHUNK and sp % 512 == 0 and sp >= BAND_WINDOW
    s_pad = 2 * CHUNK

    pos_p = jnp.arange(sp)
    pos_s = n_past + jnp.arange(s_pad)
    ret_blk = 256
    band_tq = 256
    cos_p, sin_p = _mla_rope_tables(pos_p)
    cos_s, sin_s = _mla_rope_tables(pos_s)

    span_p = 3 * band_tq
    jj = np.arange(span_p)[:, None]
    ii = np.arange(band_tq)[None, :]
    band_ok = (jj // CHUNK >= ii // CHUNK) & (jj // CHUNK <= ii // CHUNK + BAND_PREV_CHUNKS)
    bias_p = [_band_bias(band_rel_bias[l], band_tq, span_p, 2 * band_tq, band_ok)
              for l in range(depth)]
    span_s = -(-(w_band + CHUNK) // LANE) * LANE
    mask_s = np.broadcast_to(np.arange(span_s)[:, None] < w_band + CHUNK, (span_s, s_pad))
    bias_s = [_band_bias(band_rel_bias[l], s_pad, span_s, w_band, mask_s) for l in range(depth)]

    tabs_p = dict(mla_cos=cos_p, mla_sin=sin_p, ret=_ret_tables(pos_p, ret_blk, ret_blk),
                  ret_blk=ret_blk, band_bias=bias_p, band_tq=band_tq)
    tabs_s = dict(mla_cos=jnp.tile(cos_s, (bs, 1)), mla_sin=jnp.tile(sin_s, (bs, 1)),
                  ret=_ret_tables(pos_s, s_pad, CHUNK), band_bias=bias_s)

    xp = x_prompt
    xs = jnp.pad(x_sample, ((0, 0), (0, s_pad - ss), (0, 0)))
    mem2 = mem_prompt.reshape(-1, d)
    new_p = [[] for _ in range(7)]
    new_s = [[] for _ in range(5)]
    for l in range(depth):
        w = _layer_weights(l, w_in, w_mla_uq, w_mla_ukv, w_br_a, w_br_b, w_br_c, w_out, w_mem_q,
                           w_mem_k, w_mem_v, w_mem_o, w_up, w_down)
        mkv = norm_matmul(mem2, g_mem[l], w["w_mkv"], F32, _tile(mem2.shape[0], 1024), 512,
                          f"mem_kv_{l}")
        e = MEM_HEADS * MEM_HD
        mk = mkv[:, :e].reshape(bp, -1, e)
        mv = mkv[:, e:].reshape(bp, -1, e)
        xp, st_p = _trunk_layer(xp, w, P, l, tabs_p, mk, mv, None, f"p{l}")
        xs, st_s = _trunk_layer(xs, w, P, l, tabs_s, cache_mem_k[l].reshape(bs, -1, e),
                                cache_mem_v[l].reshape(bs, -1, e),
                                (cache_mla_ckv[l], cache_mla_kpe[l], state_ret[l], cache_band_k[l],
                                 cache_band_v[l]), f"s{l}")
        mem_shape = (bp, -1, MEM_HEADS, MEM_HD)
        for acc, t in zip(new_p, st_p + (mk.reshape(mem_shape), mv.reshape(mem_shape))):
            acc.append(t)
        for acc, t in zip(new_s, st_s):
            acc.append(t)
    stack = lambda ts: jnp.stack(ts, axis=0)
    return (xp, xs[:, :ss],
            stack(new_p[0]), stack(new_p[1]), stack(new_p[2]), stack(new_p[3]), stack(new_p[4]),
            stack(new_p[5]), stack(new_p[6]),
            stack(new_s[0]), stack(new_s[1]), stack(new_s[2]), stack(new_s[3]), stack(new_s[4]))
```

```python
import functools

import numpy as np
import jax
import jax.numpy as jnp
from jax import lax
from jax.experimental import pallas as pl
from jax.experimental.pallas import tpu as pltpu

F32 = jnp.float32
BF16 = jnp.bfloat16

CHUNK = 64
MLA_HEADS = 8
MLA_Q_RANK = 384
MLA_KV_RANK = 256
MLA_NOPE = 64
MLA_ROPE = 32
MLA_V = 64
MLA_SCALE = (MLA_NOPE + MLA_ROPE) ** -0.5
RET_HEADS = 4
RET_DK = 64
RET_DV = 128
BAND_HEADS = 8
BAND_HD = 64
BAND_PREV_CHUNKS = 8
BAND_WINDOW = BAND_PREV_CHUNKS * CHUNK
MAX_REL = 128
MEM_HEADS = 4
MEM_HD = 128
ROPE_THETA = 10000.0
EPS = 1e-6
NEG_INF = -1e30
LOG2E = 1.4426950408889634

LANE = 128
HEAD_SLAB = 128
VMEM_LIMIT = 48 * 1024 * 1024

Z_G = 0
Z_CQ = 3072
Z_CK = 3584
Z_CV = 4096
Z_RV = 4608
Z_RG = 5120
Z_RQ = 5632
Z_RK = 5888
Z_KV = 6144
Z_PE = 6400
Z_Q = 6528
Z_WIDTH = 6912


def _cparams(*sem):
    return pltpu.CompilerParams(dimension_semantics=sem, vmem_limit_bytes=VMEM_LIMIT)


def _rms(x, g):
    return x * lax.rsqrt(jnp.mean(x * x, axis=-1, keepdims=True) + EPS) * g


def _dot(a, b):
    return jnp.dot(a, b, preferred_element_type=F32)


def _dot_nt(a, b):
    return lax.dot_general(a, b, (((1,), (1,)), ((), ())), preferred_element_type=F32)


def _dot_tn(a, b):
    return lax.dot_general(a, b, (((0,), (0,)), ((), ())), preferred_element_type=F32)


def _sigmoid(x):
    return 0.5 * jnp.tanh(0.5 * x) + 0.5


def _lane_iota(shape):
    return lax.broadcasted_iota(jnp.int32, shape, len(shape) - 1)


def _norm_matmul_kernel(x_ref, g_ref, w_ref, o_ref, xn_ref):
    @pl.when(pl.program_id(1) == 0)
    def _():
        xn = _rms(x_ref[...].astype(F32), g_ref[...]).astype(BF16)
        xn_ref[...] = xn
        o_ref[...] = _dot(xn, w_ref[...]).astype(o_ref.dtype)

    @pl.when(pl.program_id(1) > 0)
    def _():
        o_ref[...] = _dot(xn_ref[...], w_ref[...]).astype(o_ref.dtype)


def norm_matmul(x, g, w, out_dtype, tm, tn, name):
    m, k = x.shape
    n = w.shape[1]
    assert m % tm == 0 and n % tn == 0, (m, tm, n, tn)
    return pl.pallas_call(
        _norm_matmul_kernel,
        grid=(m // tm, n // tn),
        in_specs=[
            pl.BlockSpec((tm, k), lambda i, j: (i, 0)),
            pl.BlockSpec((1, k), lambda i, j: (0, 0)),
            pl.BlockSpec((k, tn), lambda i, j: (0, j)),
        ],
        out_specs=pl.BlockSpec((tm, tn), lambda i, j: (i, j)),
        out_shape=jax.ShapeDtypeStruct((m, n), out_dtype),
        scratch_shapes=[pltpu.VMEM((tm, k), BF16)],
        compiler_params=_cparams("parallel", "arbitrary"),
        name=name,
    )(x, g.reshape(1, k), w)


def _rope_slab(x, cos, sin, rot):
    first_end, half, period = rot
    width = x.shape[1]
    right = pltpu.roll(x, width - half, 1)
    left = pltpu.roll(x, half, 1)
    partner = jnp.where((_lane_iota(x.shape) & (period - 1)) < first_end, right, left)
    return x * cos + partner * sin


_MLA_ROT = (MLA_NOPE + MLA_ROPE // 2, MLA_ROPE // 2, LANE)
_RET_ROT = (RET_DK // 2, RET_DK // 2, RET_DK)


def _mla_q_kernel(zq_ref, zkv_ref, zpe_ref, gq_ref, gkv_ref, wq_ref, cos_ref, sin_ref,
                  q_ref, ckv_ref, kpe_ref, kpe_out_ref):
    cos = cos_ref[...]
    sin = sin_ref[...]
    qn = _rms(zq_ref[...].astype(F32), gq_ref[...]).astype(BF16)
    q = _dot(qn, wq_ref[...])
    cos_q = cos * (MLA_SCALE * LOG2E)
    sin_q = sin * (MLA_SCALE * LOG2E)
    for h in range(MLA_HEADS):
        sl = slice(h * HEAD_SLAB, (h + 1) * HEAD_SLAB)
        pt = slice((MLA_HEADS + h) * HEAD_SLAB, (MLA_HEADS + h + 1) * HEAD_SLAB)
        q_ref[:, sl] = (q[:, sl] * cos_q + q[:, pt] * sin_q).astype(BF16)
    ckv_ref[...] = _rms(zkv_ref[...].astype(F32), gkv_ref[...])
    kpe = _rope_slab(zpe_ref[...].astype(F32), cos, sin, _MLA_ROT)
    kpe_ref[...] = kpe
    kpe_out_ref[...] = kpe[:, MLA_NOPE:MLA_NOPE + MLA_ROPE]


def mla_q_prep(z, g_q, g_kv, w_uq, cos, sin, tm, name):
    m = z.shape[0]
    nt = cos.shape[0] // tm
    row = lambda w: pl.BlockSpec((1, w), lambda i: (0, 0))
    return pl.pallas_call(
        _mla_q_kernel,
        grid=(m // tm,),
        in_specs=[
            pl.BlockSpec((tm, MLA_Q_RANK), lambda i: (i, Z_Q // MLA_Q_RANK)),
            pl.BlockSpec((tm, MLA_KV_RANK), lambda i: (i, Z_KV // MLA_KV_RANK)),
            pl.BlockSpec((tm, HEAD_SLAB), lambda i: (i, Z_PE // HEAD_SLAB)),
            row(MLA_Q_RANK), row(MLA_KV_RANK),
            pl.BlockSpec(w_uq.shape, lambda i: (0, 0)),
            pl.BlockSpec((tm, HEAD_SLAB), lambda i: (i % nt, 0)),
            pl.BlockSpec((tm, HEAD_SLAB), lambda i: (i % nt, 0)),
        ],
        out_specs=[
            pl.BlockSpec((tm, MLA_HEADS * HEAD_SLAB), lambda i: (i, 0)),
            pl.BlockSpec((tm, MLA_KV_RANK), lambda i: (i, 0)),
            pl.BlockSpec((tm, HEAD_SLAB), lambda i: (i, 0)),
            pl.BlockSpec((tm, MLA_ROPE), lambda i: (i, 0)),
        ],
        out_shape=[
            jax.ShapeDtypeStruct((m, MLA_HEADS * HEAD_SLAB), BF16),
            jax.ShapeDtypeStruct((m, MLA_KV_RANK), F32),
            jax.ShapeDtypeStruct((m, HEAD_SLAB), F32),
            jax.ShapeDtypeStruct((m, MLA_ROPE), F32),
        ],
        compiler_params=_cparams("parallel"),
        name=name,
    )(z, z, z, g_q.reshape(1, -1), g_kv.reshape(1, -1), w_uq, cos, sin)


def _mla_kv_kernel(ckv_ref, kpe_ref, wk_ref, wv_ref, k_ref, v_ref):
    c = ckv_ref[...].astype(BF16)
    kn = _dot(c, wk_ref[...])
    kpe = kpe_ref[...]
    for h in range(MLA_HEADS):
        sl = slice(h * HEAD_SLAB, (h + 1) * HEAD_SLAB)
        k_ref[:, sl] = (kn[:, sl] + kpe).astype(BF16)
    v = _dot(c, wv_ref[...])
    ones_lane = (_lane_iota(v.shape) & (HEAD_SLAB - 1)) == MLA_V
    v_ref[...] = jnp.where(ones_lane, 1.0, v).astype(BF16)


def mla_kv_up(ckv, kpe, w_uk, w_uv, tm, name):
    m = ckv.shape[0]
    return pl.pallas_call(
        _mla_kv_kernel,
        grid=(m // tm,),
        in_specs=[
            pl.BlockSpec((tm, MLA_KV_RANK), lambda i: (i, 0)),
            pl.BlockSpec((tm, HEAD_SLAB), lambda i: (i, 0)),
            pl.BlockSpec(w_uk.shape, lambda i: (0, 0)),
            pl.BlockSpec(w_uv.shape, lambda i: (0, 0)),
        ],
        out_specs=[
            pl.BlockSpec((tm, MLA_HEADS * HEAD_SLAB), lambda i: (i, 0)),
            pl.BlockSpec((tm, MLA_HEADS * HEAD_SLAB), lambda i: (i, 0)),
        ],
        out_shape=[
            jax.ShapeDtypeStruct((m, MLA_HEADS * HEAD_SLAB), BF16),
            jax.ShapeDtypeStruct((m, MLA_HEADS * HEAD_SLAB), BF16),
        ],
        compiler_params=_cparams("parallel"),
        name=name,
    )(ckv, kpe, w_uk, w_uv)


def _head_of_pair(x, hh):
    lane = _lane_iota(x.shape)
    keep = (lane < BAND_HD) if hh == 0 else (lane >= BAND_HD)
    return jnp.where(keep, x, jnp.zeros_like(x))


_PAIR_SLABS = [slice(hh * HEAD_SLAB, (hh + 1) * HEAD_SLAB) for hh in range(2)]


def _mla_out(acc):
    return acc[:MLA_V, :] / acc[MLA_V:MLA_V + 1, :]


def _mla_attn_kernel(q_ref, k_ref, v_ref, o_ref, acc_ref, sa_ref, sb_ref, ma_ref, mb_ref, *, t):
    qi = pl.program_id(2)
    qs = [q_ref[0, :, sl] for sl in _PAIR_SLABS]
    sa_ref, sb_ref = (sa_ref, ma_ref), (sb_ref, mb_ref)

    def rows_of(kb):
        return pl.ds(kb * t if isinstance(kb, int) else pl.multiple_of(kb * t, t), t)

    def scores(kb, buf):
        for hh, sl in enumerate(_PAIR_SLABS):
            s = _dot_nt(k_ref[0, rows_of(kb), sl], qs[hh])
            buf[0][hh] = s
            buf[1][hh] = jnp.max(s, axis=0, keepdims=True)

    def consume(kb, buf, ms, mask):
        new_m = []
        for hh, sl in enumerate(_PAIR_SLABS):
            s = buf[0][hh]
            if mask is None:
                blk_max = buf[1][hh]
            else:
                s = jnp.where(mask, s, NEG_INF)
                blk_max = jnp.max(s, axis=0, keepdims=True)
            m = jnp.maximum(ms[hh], blk_max)
            alpha = jnp.exp2(ms[hh] - m)
            p = jnp.exp2(s - m).astype(BF16)
            acc_ref[hh] = alpha * acc_ref[hh] + _dot_tn(v_ref[0, rows_of(kb), sl], p)
            new_m.append(m)
        return tuple(new_m)

    def finish():
        outs = [_mla_out(acc_ref[hh]) for hh in range(2)]
        o_ref[0] = jnp.concatenate(outs, axis=0).T.astype(o_ref.dtype)

    acc_ref[...] = jnp.zeros_like(acc_ref)
    ms = tuple(jnp.full((1, t), NEG_INF, F32) for _ in range(2))
    shift = CHUNK.bit_length() - 1
    key_chunk = jnp.right_shift(lax.broadcasted_iota(jnp.int32, (t, t), 0), shift)
    qry_chunk = jnp.right_shift(lax.broadcasted_iota(jnp.int32, (t, t), 1), shift)
    diag = key_chunk <= qry_chunk
    scores(0, sa_ref)

    def pair(j, ms):
        kb = 2 * j
        scores(kb + 1, sb_ref)
        ms = consume(kb, sa_ref, ms, None)
        scores(kb + 2, sa_ref)
        return consume(kb + 1, sb_ref, ms, None)

    ms = lax.fori_loop(0, qi // 2, pair, ms)

    @pl.when(qi % 2 == 0)
    def _():
        consume(qi, sa_ref, ms, diag)
        finish()

    @pl.when(qi % 2 == 1)
    def _():
        scores(qi, sb_ref)
        consume(qi, sb_ref, consume(qi - 1, sa_ref, ms, None), diag)
        finish()


def _mla_attn_seg_kernel(q_ref, k1_ref, v1_ref, k2_ref, v2_ref, o_ref, *, n2_valid):
    for pair in range(MLA_HEADS // 2):
        outs = []
        for hh in range(2):
            h = 2 * pair + hh
            sl = slice(h * HEAD_SLAB, (h + 1) * HEAD_SLAB)
            q = q_ref[0, :, sl]
            s1 = _dot_nt(k1_ref[0, :, sl], q)
            s2 = _dot_nt(k2_ref[0, :, sl], q)
            s2 = jnp.where(lax.broadcasted_iota(jnp.int32, s2.shape, 0) < n2_valid, s2, NEG_INF)
            m = jnp.maximum(jnp.max(s1, axis=0, keepdims=True), jnp.max(s2, axis=0, keepdims=True))
            acc = (_dot_tn(v1_ref[0, :, sl], jnp.exp2(s1 - m).astype(BF16))
                   + _dot_tn(v2_ref[0, :, sl], jnp.exp2(s2 - m).astype(BF16)))
            outs.append(_mla_out(acc))
        o_ref[0, :, pair * LANE:(pair + 1) * LANE] = (
            jnp.concatenate(outs, axis=0).T.astype(o_ref.dtype))


def mla_attention_seg(q, k1, v1, k2, v2, n2_valid, name):
    b, sq, _ = q.shape
    blk = lambda a: pl.BlockSpec((1,) + a.shape[1:], lambda b_: (b_, 0, 0))
    return pl.pallas_call(
        functools.partial(_mla_attn_seg_kernel, n2_valid=n2_valid),
        grid=(b,),
        in_specs=[blk(q), blk(k1), blk(v1), blk(k2), blk(v2)],
        out_specs=pl.BlockSpec((1, sq, MLA_HEADS * MLA_V), lambda b_: (b_, 0, 0)),
        out_shape=jax.ShapeDtypeStruct((b, sq, MLA_HEADS * MLA_V), BF16),
        compiler_params=_cparams("parallel"),
        name=name,
    )(q, k1, v1, k2, v2)


def mla_attention(q, k, v, t, name):
    b, s, _ = q.shape
    assert s % t == 0 and k.shape[1] == s
    whole = pl.BlockSpec((1, s, 2 * HEAD_SLAB), lambda b_, h, i: (b_, 0, h))
    return pl.pallas_call(
        functools.partial(_mla_attn_kernel, t=t),
        grid=(b, MLA_HEADS // 2, s // t),
        in_specs=[pl.BlockSpec((1, t, 2 * HEAD_SLAB), lambda b_, h, i: (b_, i, h)), whole, whole],
        out_specs=pl.BlockSpec((1, t, 2 * MLA_V), lambda b_, h, i: (b_, i, h)),
        out_shape=jax.ShapeDtypeStruct((b, s, MLA_HEADS * MLA_V), BF16),
        scratch_shapes=[pltpu.VMEM((2, HEAD_SLAB, t), F32), pltpu.VMEM((2, t, t), F32),
                        pltpu.VMEM((2, t, t), F32), pltpu.VMEM((2, 1, t), F32),
                        pltpu.VMEM((2, 1, t), F32)],
        compiler_params=_cparams("parallel", "parallel", "arbitrary"),
        name=name,
    )(q, k, v)


def _band_attn_kernel(q_ref, k_ref, v_ref, bias_ref, o_ref, sa_ref, sb_ref, ma_ref, mb_ref, *,
                      tq, nq, span, sliding):
    def geom(i):
        if not sliding:
            return 0, span, 0
        if isinstance(i, int):
            n = min(i + 1, span // tq) * tq
            return max(i + 1 - span // tq, 0) * tq, n, span - n
        return pl.multiple_of((i + 1 - span // tq) * tq, tq), span, 0

    def q_rows(i):
        return pl.ds(i * tq if isinstance(i, int) else pl.multiple_of(i * tq, tq), tq)

    sa_ref, sb_ref = (sa_ref, ma_ref), (sb_ref, mb_ref)

    def scores(i, buf):
        start, n, boff = geom(i)
        q = q_ref[0, q_rows(i), :]
        k = k_ref[0, pl.ds(start, n), :]
        for hh in range(2):
            s = _dot_nt(k, _head_of_pair(q, hh)) + bias_ref[0, hh, boff:boff + n, :]
            buf[0][hh, :n, :] = s
            buf[1][hh] = jnp.max(s, axis=0, keepdims=True)

    def consume(i, buf):
        start, n, _ = geom(i)
        v = v_ref[0, pl.ds(start, n), :]
        outs = []
        for hh in range(2):
            s = buf[0][hh, :n, :]
            m = buf[1][hh]
            p = jnp.exp2(s - m)
            l = jnp.sum(p, axis=0, keepdims=True)
            outs.append(_dot_tn(v, p.astype(BF16)) / l)
        first = lax.broadcasted_iota(jnp.int32, outs[0].shape, 0) < BAND_HD
        o_ref[0, q_rows(i), :] = jnp.where(first, outs[0], outs[1]).T.astype(o_ref.dtype)

    scores(0, sa_ref)
    if nq == 1:
        consume(0, sa_ref)
        return
    scores(1, sb_ref)
    consume(0, sa_ref)
    scores(2, sa_ref)
    consume(1, sb_ref)

    def pair(j, carry):
        i = 2 * j + 2
        scores(i + 1, sb_ref)
        consume(i, sa_ref)
        scores(i + 2, sa_ref)
        consume(i + 1, sb_ref)
        return carry

    lax.fori_loop(0, (nq - 2) // 2 - 1, pair, 0)
    scores(nq - 1, sb_ref)
    consume(nq - 2, sa_ref)
    consume(nq - 1, sb_ref)


def band_attention(q, q_col, k, k_col, v, v_col, bias, tq, sliding, name):
    b, sq = q.shape[:2]
    sk = k.shape[1]
    nq = sq // tq
    span = bias.shape[-2]
    assert (sliding and sk == sq and nq >= 4 and nq % 2 == 0) or (nq == 1 and sk == span)
    col = lambda c: (lambda hp, b_: (b_, 0, c + hp))
    return pl.pallas_call(
        functools.partial(_band_attn_kernel, tq=tq, nq=nq, span=span, sliding=sliding),
        grid=(BAND_HEADS // 2, b),
        in_specs=[
            pl.BlockSpec((1, sq, LANE), col(q_col)),
            pl.BlockSpec((1, sk, LANE), col(k_col)),
            pl.BlockSpec((1, sk, LANE), col(v_col)),
            pl.BlockSpec((1, 2, span, tq), lambda hp, b_: (0, hp, 0, 0)),
        ],
        out_specs=pl.BlockSpec((1, sq, LANE), col(0)),
        out_shape=jax.ShapeDtypeStruct((b, sq, BAND_HEADS * BAND_HD), BF16),
        scratch_shapes=[pltpu.VMEM((2, span, tq), F32), pltpu.VMEM((2, span, tq), F32),
                        pltpu.VMEM((2, 1, tq), F32), pltpu.VMEM((2, 1, tq), F32)],
        compiler_params=_cparams("parallel", "arbitrary"),
        name=name,
    )(q, k, v, bias.reshape((1,) + bias.shape))


def _retention_kernel(rq_ref, rk_ref, rv_ref, rg_ref, cq_ref, sq_ref, ck_ref, sk_ref,
                      dmat_ref, rowdec_ref, kw_ref, sdec_ref, gn_ref, init_ref,
                      y_ref, state_out_ref, state_ref):
    c = pl.program_id(1)

    @pl.when(c == 0)
    def _():
        state_ref[...] = init_ref[...]

    n_pair = RET_HEADS // 2
    gn = gn_ref[...]
    row_is_first = lax.broadcasted_iota(jnp.int32, (LANE, RET_DV), 0) < RET_DK
    for bi in range(rq_ref.shape[0]):
        q = _rope_slab(rq_ref[bi].astype(F32), cq_ref[...], sq_ref[...], _RET_ROT)
        k = _rope_slab(rk_ref[bi].astype(F32), ck_ref[...], sk_ref[...], _RET_ROT)
        qb = q.astype(BF16)
        kb = k.astype(BF16)
        kwb = (k * kw_ref[...]).astype(BF16)
        for p in range(n_pair):
            psl = slice(p * LANE, (p + 1) * LANE)
            st = state_ref[bi, p]
            stb = st.astype(BF16)
            kv = []
            for hh in range(2):
                h = 2 * p + hh
                vsl = slice(h * RET_DV, (h + 1) * RET_DV)
                v = rv_ref[bi, :, vsl]
                qh = _head_of_pair(qb[:, psl], hh)
                s = _dot_nt(qh, kb[:, psl]) * dmat_ref[h]
                o = _dot(s.astype(BF16), v) + _dot(qh, stb) * rowdec_ref[:, vsl]
                mu = jnp.mean(o, axis=-1, keepdims=True)
                d = o - mu
                yn = d * lax.rsqrt(jnp.mean(d * d, axis=-1, keepdims=True) + EPS)
                g = rg_ref[bi, :, vsl].astype(F32)
                y_ref[bi, :, vsl] = (yn * gn[:, vsl] * (g * _sigmoid(g))).astype(y_ref.dtype)
                kv.append(_dot_tn(kwb[:, psl], v))
            state_ref[bi, p] = st * sdec_ref[p] + jnp.where(row_is_first, kv[0], kv[1])

    @pl.when(c == pl.num_programs(1) - 1)
    def _():
        state_out_ref[...] = state_ref[...]


def retention(z, tabs, g_gn, init_state, blk, name):
    b, s = z.shape[:2]
    nc = s // blk
    nb = 2 if b % 2 == 0 else 1
    n_pair = RET_HEADS // 2
    zblk = lambda w, col: pl.BlockSpec((nb, blk, w), lambda b_, c: (b_, c, col // w))
    tab = lambda w: pl.BlockSpec((blk, w), lambda b_, c: (c, 0))
    const = lambda a: pl.BlockSpec(a.shape, lambda b_, c: (0,) * a.ndim)
    qk_w = RET_HEADS * RET_DK
    v_w = RET_HEADS * RET_DV
    return pl.pallas_call(
        _retention_kernel,
        grid=(b // nb, nc),
        in_specs=[
            zblk(qk_w, Z_RQ), zblk(qk_w, Z_RK), zblk(v_w, Z_RV), zblk(v_w, Z_RG),
            tab(qk_w), tab(qk_w), tab(qk_w), tab(qk_w),
            const(tabs["dmat"]), const(tabs["rowdec"]), const(tabs["kw"]), const(tabs["sdec"]),
            pl.BlockSpec((1, v_w), lambda b_, c: (0, 0)),
            pl.BlockSpec((nb, n_pair, LANE, RET_DV), lambda b_, c: (b_, 0, 0, 0)),
        ],
        out_specs=[
            pl.BlockSpec((nb, blk, v_w), lambda b_, c: (b_, c, 0)),
            pl.BlockSpec((nb, n_pair, LANE, RET_DV), lambda b_, c: (b_, 0, 0, 0)),
        ],
        out_shape=[
            jax.ShapeDtypeStruct((b, s, v_w), BF16),
            jax.ShapeDtypeStruct((b, n_pair, LANE, RET_DV), F32),
        ],
        scratch_shapes=[pltpu.VMEM((nb, n_pair, LANE, RET_DV), F32)],
        compiler_params=_cparams("parallel", "arbitrary"),
        name=name,
    )(z, z, z, z, tabs["cos_q"], tabs["sin_q"], tabs["cos_k"], tabs["sin_k"],
      tabs["dmat"], tabs["rowdec"], tabs["kw"], tabs["sdec"], g_gn.reshape(1, -1), init_state)


def _mix_out_kernel(oa_ref, yb_ref, oc_ref, zg0_ref, zg1_ref, zg2_ref, bg_ref, wa_ref, wb_ref,
                    wc_ref, wo_ref, g_ref, x_ref, o_ref):
    d = x_ref.shape[-1]
    merged = None
    for n, (br_ref, w_ref, zg_ref) in enumerate(
            ((oa_ref, wa_ref, zg0_ref), (yb_ref, wb_ref, zg1_ref), (oc_ref, wc_ref, zg2_ref))):
        gate = _sigmoid(zg_ref[...].astype(F32) + bg_ref[:, n * d:(n + 1) * d])
        term = gate * _dot(br_ref[...], w_ref[...])
        merged = term if merged is None else merged + term
    y = _dot(merged.astype(BF16), wo_ref[...])
    o_ref[...] = x_ref[...] + _rms(y, g_ref[...])


def mix_out(o_a, y_b, o_c, z, b_gate, w_a, w_b, w_c, w_out, g_post, x, tm, name):
    m, d = x.shape
    e = o_a.shape[1]
    act = pl.BlockSpec((tm, e), lambda i: (i, 0))
    zg = lambda n: pl.BlockSpec((tm, d), lambda i: (i, Z_G // d + n))
    const = lambda a: pl.BlockSpec(a.shape, lambda i: (0, 0))
    return pl.pallas_call(
        _mix_out_kernel,
        grid=(m // tm,),
        in_specs=[act, act, act, zg(0), zg(1), zg(2),
                  pl.BlockSpec((1, 3 * d), lambda i: (0, 0)),
                  const(w_a), const(w_b), const(w_c), const(w_out),
                  pl.BlockSpec((1, d), lambda i: (0, 0)),
                  pl.BlockSpec((tm, d), lambda i: (i, 0))],
        out_specs=pl.BlockSpec((tm, d), lambda i: (i, 0)),
        out_shape=jax.ShapeDtypeStruct((m, d), F32),
        compiler_params=_cparams("parallel"),
        name=name,
    )(o_a, y_b, o_c, z, z, z, b_gate.reshape(1, -1), w_a, w_b, w_c, w_out,
      g_post.reshape(1, -1), x)


def _mem_attn_kernel(x_ref, mk_ref, mv_ref, gpre_ref, wq_ref, wo_ref, gpost_ref, o_ref):
    x = x_ref[0]
    u = _rms(x, gpre_ref[...]).astype(BF16)
    q = _dot(u, wq_ref[...]).astype(BF16)
    outs = []
    for h in range(MEM_HEADS):
        sl = slice(h * MEM_HD, (h + 1) * MEM_HD)
        s = _dot_nt(q[:, sl], mk_ref[0, :, sl].astype(BF16)) * (MEM_HD ** -0.5 * LOG2E)
        m = jnp.max(s, axis=-1, keepdims=True)
        p = jnp.exp2(s - m)
        l = jnp.sum(p, axis=-1, keepdims=True)
        outs.append(_dot(p.astype(BF16), mv_ref[0, :, sl].astype(BF16)) / l)
    o = jnp.concatenate(outs, axis=-1).astype(BF16)
    o_ref[0] = x + _rms(_dot(o, wo_ref[...]), gpost_ref[...])


def mem_attention(x, mk, mv, g_pre, w_q, w_o, g_post, tm, name):
    b, s, d = x.shape
    const = lambda a: pl.BlockSpec(a.shape, lambda b_, i: (0, 0))
    vec = pl.BlockSpec((1, d), lambda b_, i: (0, 0))
    mem = pl.BlockSpec((1,) + mk.shape[1:], lambda b_, i: (b_, 0, 0))
    return pl.pallas_call(
        _mem_attn_kernel,
        grid=(b, s // tm),
        in_specs=[pl.BlockSpec((1, tm, d), lambda b_, i: (b_, i, 0)), mem, mem,
                  vec, const(w_q), const(w_o), vec],
        out_specs=pl.BlockSpec((1, tm, d), lambda b_, i: (b_, i, 0)),
        out_shape=jax.ShapeDtypeStruct((b, s, d), F32),
        compiler_params=_cparams("parallel", "parallel"),
        name=name,
    )(x, mk, mv, g_pre.reshape(1, -1), w_q, w_o, g_post.reshape(1, -1))


def _mlp_kernel(x_ref, gpre_ref, wu_ref, wd_ref, gpost_ref, o_ref, xn_ref, acc_ref):
    j = pl.program_id(1)
    last = pl.num_programs(1) - 1

    def partial_out(xn):
        h = jnp.square(jnp.maximum(_dot(xn, wu_ref[...]), 0.0))
        return _dot(h.astype(BF16), wd_ref[...])

    @pl.when(j == 0)
    def _():
        xn = _rms(x_ref[...], gpre_ref[...]).astype(BF16)
        xn_ref[...] = xn
        acc_ref[...] = partial_out(xn)

    @pl.when(jnp.logical_and(j > 0, j < last))
    def _():
        acc_ref[...] += partial_out(xn_ref[...])

    @pl.when(j == last)
    def _():
        y = acc_ref[...] + partial_out(xn_ref[...])
        o_ref[...] = x_ref[...] + _rms(y, gpost_ref[...])


def mlp(x, g_pre, w_up, w_down, g_post, tm, tf, name):
    m, d = x.shape
    f = w_up.shape[1]
    vec = pl.BlockSpec((1, d), lambda i, j: (0, 0))
    return pl.pallas_call(
        _mlp_kernel,
        grid=(m // tm, f // tf),
        in_specs=[pl.BlockSpec((tm, d), lambda i, j: (i, 0)), vec,
                  pl.BlockSpec((d, tf), lambda i, j: (0, j)),
                  pl.BlockSpec((tf, d), lambda i, j: (j, 0)), vec],
        out_specs=pl.BlockSpec((tm, d), lambda i, j: (i, 0)),
        out_shape=jax.ShapeDtypeStruct((m, d), F32),
        scratch_shapes=[pltpu.VMEM((tm, d), BF16), pltpu.VMEM((tm, d), F32)],
        compiler_params=_cparams("parallel", "arbitrary"),
        name=name,
    )(x, g_pre.reshape(1, -1), w_up, w_down, g_post.reshape(1, -1))


def _rope_angles(pos, half):
    inv = ROPE_THETA ** (-jnp.arange(half, dtype=F32) / half)
    ang = pos.astype(F32)[:, None] * inv[None, :]
    return jnp.cos(ang), jnp.sin(ang)


def _mla_rope_tables(pos):
    cos, sin = _rope_angles(pos, MLA_ROPE // 2)
    t = pos.shape[0]
    one = jnp.ones((t, MLA_NOPE), F32)
    zero64 = jnp.zeros((t, MLA_NOPE), F32)
    pad = jnp.zeros((t, HEAD_SLAB - MLA_NOPE - MLA_ROPE), F32)
    return (jnp.concatenate([one, cos, cos, pad], axis=1),
            jnp.concatenate([zero64, -sin, sin, pad], axis=1))


def _ret_tables(pos, blk, n_real):
    cos, sin = _rope_angles(pos, RET_DK // 2)
    cos_q = jnp.tile(jnp.concatenate([cos, cos], axis=1), (1, RET_HEADS))
    sin_q = jnp.tile(jnp.concatenate([-sin, sin], axis=1), (1, RET_HEADS))
    k_scale = RET_DK ** -0.5
    log_g = jnp.log1p(-jnp.exp2(-5.0 - jnp.arange(RET_HEADS, dtype=F32)))
    idx = jnp.arange(blk, dtype=F32)
    diff = idx[:, None] - idx[None, :]
    dmat = jnp.where(diff >= 0, jnp.exp(log_g[:, None, None] * jnp.maximum(diff, 0.0)), 0.0)
    rowdec = jnp.exp(log_g[None, :] * (idx[:, None] + 1.0))
    w = jnp.where(idx[:, None] < n_real,
                  jnp.exp(log_g[None, :] * jnp.maximum(n_real - 1.0 - idx[:, None], 0.0)), 0.0)
    sdec = jnp.exp(log_g * n_real)
    n_pair = RET_HEADS // 2
    return dict(
        cos_q=cos_q, sin_q=sin_q, cos_k=cos_q * k_scale, sin_k=sin_q * k_scale,
        dmat=dmat,
        rowdec=jnp.repeat(rowdec, RET_DV, axis=1),
        kw=jnp.repeat(w, RET_DK, axis=1),
        sdec=jnp.broadcast_to(jnp.repeat(sdec, RET_DK).reshape(n_pair, LANE, 1),
                              (n_pair, LANE, RET_DV)),
    )


def _band_bias(table, tq, span, q_off, allowed):
    sub = 8
    assert span % sub == 0
    length = tq + span - 1
    d = np.arange(length)
    ext = table[:, np.clip(d + q_off - span + 1, -MAX_REL, MAX_REL) + MAX_REL].astype(F32) * LOG2E
    shifted = jnp.stack([ext[:, sub - 1 - r:length - r] for r in range(sub)], axis=1)
    tile = jnp.concatenate([shifted[:, :, span - sub * (a + 1):span - sub * (a + 1) + tq]
                            for a in range(span // sub)], axis=1)
    return jnp.where(allowed[None], tile, NEG_INF)


def _layer_weights(l, w_in, w_mla_uq, w_mla_ukv, w_br_a, w_br_b, w_br_c, w_out, w_mem_q, w_mem_k,
                   w_mem_v, w_mem_o, w_up, w_down):
    d = w_in.shape[1]
    parts, start = [], 0
    for n in (MLA_Q_RANK, MLA_KV_RANK, MLA_ROPE, 256, 256, 512, 512, 512, 512, 512, 3 * d):
        parts.append(w_in[l, :, start:start + n])
        start += n
    zq, zkv, zpe, rq, rk, rv, rg, cq, ck, cv, zg = parts
    cq = cq * (BAND_HD ** -0.5 * LOG2E)
    zeros = lambda n: jnp.zeros((d, n), w_in.dtype)
    w_in_l = jnp.concatenate(
        [zg, cq, ck, cv, rv, rg, rq, rk, zkv, zeros(MLA_NOPE), zpe,
         zeros(HEAD_SLAB - MLA_NOPE - MLA_ROPE), zq], axis=1).astype(BF16)
    assert w_in_l.shape[1] == Z_WIDTH
    pad_head = lambda w: jnp.pad(w, ((0, 0), (0, 0), (0, HEAD_SLAB - w.shape[-1])))
    flat = lambda w: w.reshape(w.shape[0], -1).astype(BF16)
    return dict(
        w_in=w_in_l,
        w_uq=jnp.concatenate([flat(pad_head(w_mla_uq[l])), flat(pad_head(jnp.concatenate(
            [jnp.zeros_like(w_mla_uq[l][..., :MLA_NOPE]),
             w_mla_uq[l][..., MLA_NOPE + MLA_ROPE // 2:],
             w_mla_uq[l][..., MLA_NOPE:MLA_NOPE + MLA_ROPE // 2]], axis=-1)))], axis=1),
        w_uk=flat(pad_head(w_mla_ukv[l][..., :MLA_NOPE])),
        w_uv=flat(pad_head(w_mla_ukv[l][..., MLA_NOPE:])),
        w_a=w_br_a[l].astype(BF16), w_b=w_br_b[l].astype(BF16), w_c=w_br_c[l].astype(BF16),
        w_out=w_out[l].astype(BF16),
        w_mq=flat(w_mem_q[l]),
        w_mkv=jnp.concatenate([flat(w_mem_k[l]), flat(w_mem_v[l])], axis=1),
        w_mo=w_mem_o[l].reshape(-1, d).astype(BF16),
        w_up=w_up[l].astype(BF16), w_down=w_down[l].astype(BF16),
    )


def _tile(n, pref):
    t = min(n, pref)
    while n % t:
        t -= LANE
    return t


def _trunk_layer(x, w, P, l, tabs, mem_k, mem_v, past, tag):
    b, s, d = x.shape
    m = b * s
    x2 = x.reshape(m, d)
    tm = _tile(m, 1024)
    tm2 = _tile(m, 512)
    z = norm_matmul(x2, P["g_pre_mix"][l], w["w_in"], BF16, tm, 2304, f"in_proj_{tag}")
    q, ckv, kpe, kpe_out = mla_q_prep(z, P["g_mla_q"][l], P["g_mla_kv"][l], w["w_uq"],
                             tabs["mla_cos"], tabs["mla_sin"], tm2, f"mla_q_{tag}")
    z3 = z.reshape(b, s, Z_WIDTH)
    q3 = q.reshape(b, s, -1)
    if past is None:
        k, v = mla_kv_up(ckv, kpe, w["w_uk"], w["w_uv"], tm, f"mla_kv_{tag}")
        o_a = mla_attention(q3, k.reshape(b, s, -1), v.reshape(b, s, -1), _tile(s, 512),
                            f"mla_attn_{tag}")
        init = jnp.zeros((b, RET_HEADS // 2, LANE, RET_DV), F32)
        y_b, state = retention(z3, tabs["ret"], P["g_ret_gn"][l], init, tabs["ret_blk"],
                               f"retention_{tag}")
        o_c = band_attention(z3, Z_CQ // LANE, z3, Z_CK // LANE, z3, Z_CV // LANE,
                             tabs["band_bias"][l], tabs["band_tq"], True, f"band_{tag}")
        n_real = s
        band_k = z3[:, s - BAND_WINDOW:, Z_CK:Z_CK + 512]
        band_v = z3[:, s - BAND_WINDOW:, Z_CV:Z_CV + 512]
    else:
        c_ckv, c_kpe, s_ret, c_bk, c_bv = past
        n_real = CHUNK
        n_past = c_ckv.shape[1]
        kpe_pad = jnp.pad(c_kpe, ((0, 0), (0, 0), (MLA_NOPE, HEAD_SLAB - MLA_NOPE - MLA_ROPE)))
        k_c, v_c = mla_kv_up(c_ckv.reshape(b * n_past, -1), kpe_pad.reshape(b * n_past, -1),
                             w["w_uk"], w["w_uv"], _tile(b * n_past, 1024), f"mla_kv_cache_{tag}")
        k_n, v_n = mla_kv_up(ckv, kpe, w["w_uk"], w["w_uv"], tm, f"mla_kv_{tag}")
        per_b = lambda a: a.reshape(b, -1, a.shape[-1])
        o_a = mla_attention_seg(q3, per_b(k_c), per_b(v_c), per_b(k_n), per_b(v_n), n_real,
                                f"mla_attn_{tag}")
        init = s_ret.astype(F32).reshape(b, RET_HEADS // 2, LANE, RET_DV)
        y_b, state = retention(z3, tabs["ret"], P["g_ret_gn"][l], init, s, f"retention_{tag}")
        band_k = z3[:, :n_real, Z_CK:Z_CK + 512]
        band_v = z3[:, :n_real, Z_CV:Z_CV + 512]
        span = tabs["band_bias"][l].shape[-2]
        w_band = c_bk.shape[1]
        catb = lambda c, n: jnp.pad(
            jnp.concatenate([c.reshape(b, w_band, -1).astype(BF16), n], axis=1),
            ((0, 0), (0, span - w_band - n_real), (0, 0)))
        o_c = band_attention(z3, Z_CQ // LANE, catb(c_bk, band_k), 0, catb(c_bv, band_v), 0,
                             tabs["band_bias"][l], s, False, f"band_{tag}")
    x2 = mix_out(o_a.reshape(m, -1), y_b.reshape(m, -1), o_c.reshape(m, -1), z, P["b_gate"][l],
                 w["w_a"], w["w_b"], w["w_c"], w["w_out"], P["g_post_mix"][l], x2, tm2,
                 f"mix_out_{tag}")
    x3 = mem_attention(x2.reshape(b, s, d), mem_k, mem_v, P["g_pre_mem"][l], w["w_mq"], w["w_mo"],
                       P["g_post_mem"][l], _tile(s, 512), f"mem_attn_{tag}")
    x4 = mlp(x3.reshape(m, d), P["g_pre_ff"][l], w["w_up"], w["w_down"], P["g_post_ff"][l],
             tm, 1024, f"mlp_{tag}")
    new = (ckv.reshape(b, s, -1)[:, :n_real],
           kpe_out.reshape(b, s, -1)[:, :n_real],
           state.reshape(b, RET_HEADS, RET_DK, RET_DV),
           band_k.astype(F32).reshape(b, -1, BAND_HEADS, BAND_HD),
           band_v.astype(F32).reshape(b, -1, BAND_HEADS, BAND_HD))
    return x4.reshape(b, s, d), new


def kernel(x_prompt, x_sample, cache_mla_ckv, cache_mla_kpe, state_ret, cache_band_k, cache_band_v, cache_mem_k, cache_mem_v, mem_prompt, g_pre_mix, w_in, g_mla_q, w_mla_uq, g_mla_kv, w_mla_ukv, g_ret_gn, band_rel_bias, w_br_a, w_br_b, w_br_c, b_gate, w_out, g_post_mix, g_pre_mem, g_mem, w_mem_q, w_mem_k, w_mem_v, w_mem_o, g_post_mem, g_pre_ff, w_up, w_down, g_post_ff):
    P = dict(g_pre_mix=g_pre_mix, g_mla_q=g_mla_q, g_mla_kv=g_mla_kv, g_ret_gn=g_ret_gn,
             b_gate=b_gate, g_post_mix=g_post_mix, g_pre_mem=g_pre_mem, g_post_mem=g_post_mem,
             g_pre_ff=g_pre_ff, g_post_ff=g_post_ff)
    depth = w_in.shape[0]
    bp, sp, d = x_prompt.shape
    bs, ss, _ = x_sample.shape
    n_past = cache_mla_ckv.shape[2]
    w_band = cache_band_k.shape[2]
    assert ss == CHUNK and sp % 512 == 0 and sp >= BAND_WINDOW
    s_pad = 2 * CHUNK

    pos_p = jnp.arange(sp)
    pos_s = n_past + jnp.arange(s_pad)
    ret_blk = 256
    band_tq = 256
    cos_p, sin_p = _mla_rope_tables(pos_p)
    cos_s, sin_s = _mla_rope_tables(pos_s)

    span_p = 3 * band_tq
    jj = np.arange(span_p)[:, None]
    ii = np.arange(band_tq)[None, :]
    band_ok = (jj // CHUNK >= ii // CHUNK) & (jj // CHUNK <= ii // CHUNK + BAND_PREV_CHUNKS)
    bias_p = [_band_bias(band_rel_bias[l], band_tq, span_p, 2 * band_tq, band_ok)
              for l in range(depth)]
    span_s = -(-(w_band + CHUNK) // LANE) * LANE
    mask_s = np.broadcast_to(np.arange(span_s)[:, None] < w_band + CHUNK, (span_s, s_pad))
    bias_s = [_band_bias(band_rel_bias[l], s_pad, span_s, w_band, mask_s) for l in range(depth)]

    tabs_p = dict(mla_cos=cos_p, mla_sin=sin_p, ret=_ret_tables(pos_p, ret_blk, ret_blk),
                  ret_blk=ret_blk, band_bias=bias_p, band_tq=band_tq)
    tabs_s = dict(mla_cos=jnp.tile(cos_s, (bs, 1)), mla_sin=jnp.tile(sin_s, (bs, 1)),
                  ret=_ret_tables(pos_s, s_pad, CHUNK), band_bias=bias_s)

    xp = x_prompt
    xs = jnp.pad(x_sample, ((0, 0), (0, s_pad - ss), (0, 0)))
    mem2 = mem_prompt.reshape(-1, d)
    new_p = [[] for _ in range(7)]
    new_s = [[] for _ in range(5)]
    for l in range(depth):
        w = _layer_weights(l, w_in, w_mla_uq, w_mla_ukv, w_br_a, w_br_b, w_br_c, w_out, w_mem_q,
                           w_mem_k, w_mem_v, w_mem_o, w_up, w_down)
        mkv = norm_matmul(mem2, g_mem[l], w["w_mkv"], F32, _tile(mem2.shape[0], 1024), 512,
                          f"mem_kv_{l}")
        e = MEM_HEADS * MEM_HD
        mk = mkv[:, :e].reshape(bp, -1, e)
        mv = mkv[:, e:].reshape(bp, -1, e)
        xp, st_p = _trunk_layer(xp, w, P, l, tabs_p, mk, mv, None, f"p{l}")
        xs, st_s = _trunk_layer(xs, w, P, l, tabs_s, cache_mem_k[l].reshape(bs, -1, e),
                                cache_mem_v[l].reshape(bs, -1, e),
                                (cache_mla_ckv[l], cache_mla_kpe[l], state_ret[l], cache_band_k[l],
                                 cache_band_v[l]), f"s{l}")
        mem_shape = (bp, -1, MEM_HEADS, MEM_HD)
        for acc, t in zip(new_p, st_p + (mk.reshape(mem_shape), mv.reshape(mem_shape))):
            acc.append(t)
        for acc, t in zip(new_s, st_s):
            acc.append(t)
    stack = lambda ts: jnp.stack(ts, axis=0)
    return (xp, xs[:, :ss],
            stack(new_p[0]), stack(new_p[1]), stack(new_p[2]), stack(new_p[3]), stack(new_p[4]),
            stack(new_p[5]), stack(new_p[6]),
            stack(new_s[0]), stack(new_s[1]), stack(new_s[2]), stack(new_s[3]), stack(new_s[4]))
```

```python
import functools

import numpy as np
import jax
import jax.numpy as jnp
from jax import lax
from jax.experimental import pallas as pl
from jax.experimental.pallas import tpu as pltpu

F32 = jnp.float32
BF16 = jnp.bfloat16

CHUNK = 64
MLA_HEADS = 8
MLA_Q_RANK = 384
MLA_KV_RANK = 256
MLA_NOPE = 64
MLA_ROPE = 32
MLA_V = 64
MLA_SCALE = (MLA_NOPE + MLA_ROPE) ** -0.5
RET_HEADS = 4
RET_DK = 64
RET_DV = 128
BAND_HEADS = 8
BAND_HD = 64
BAND_PREV_CHUNKS = 8
BAND_WINDOW = BAND_PREV_CHUNKS * CHUNK
MAX_REL = 128
MEM_HEADS = 4
MEM_HD = 128
ROPE_THETA = 10000.0
EPS = 1e-6
NEG_INF = -1e30
LOG2E = 1.4426950408889634

LANE = 128
HEAD_SLAB = 128
VMEM_LIMIT = 48 * 1024 * 1024

Z_G = 0
Z_CQ = 3072
Z_CK = 3584
Z_CV = 4096
Z_RV = 4608
Z_RG = 5120
Z_RQ = 5632
Z_RK = 5888
Z_KV = 6144
Z_PE = 6400
Z_Q = 6528
Z_WIDTH = 6912


def _cparams(*sem, **kw):
    return pltpu.CompilerParams(dimension_semantics=sem, vmem_limit_bytes=VMEM_LIMIT, **kw)


def _rms(x, g):
    return x * lax.rsqrt(jnp.mean(x * x, axis=-1, keepdims=True) + EPS) * g


def _dot(a, b):
    return jnp.dot(a, b, preferred_element_type=F32)


def _dot_nt(a, b):
    return lax.dot_general(a, b, (((1,), (1,)), ((), ())), preferred_element_type=F32)


def _dot_tn(a, b):
    return lax.dot_general(a, b, (((0,), (0,)), ((), ())), preferred_element_type=F32)


def _sigmoid(x):
    return 0.5 * jnp.tanh(0.5 * x) + 0.5


def _lane_iota(shape):
    return lax.broadcasted_iota(jnp.int32, shape, len(shape) - 1)


def _norm_matmul_kernel(x_ref, g_ref, w_ref, o_ref, xn_ref):
    @pl.when(pl.program_id(1) == 0)
    def _():
        xn = _rms(x_ref[...].astype(F32), g_ref[...]).astype(BF16)
        xn_ref[...] = xn
        o_ref[...] = _dot(xn, w_ref[...]).astype(o_ref.dtype)

    @pl.when(pl.program_id(1) > 0)
    def _():
        o_ref[...] = _dot(xn_ref[...], w_ref[...]).astype(o_ref.dtype)


def norm_matmul(x, g, w, out_dtype, tm, tn, name):
    m, k = x.shape
    n = w.shape[1]
    assert m % tm == 0 and n % tn == 0, (m, tm, n, tn)
    return pl.pallas_call(
        _norm_matmul_kernel,
        grid=(m // tm, n // tn),
        in_specs=[
            pl.BlockSpec((tm, k), lambda i, j: (i, 0)),
            pl.BlockSpec((1, k), lambda i, j: (0, 0)),
            pl.BlockSpec((k, tn), lambda i, j: (0, j)),
        ],
        out_specs=pl.BlockSpec((tm, tn), lambda i, j: (i, j)),
        out_shape=jax.ShapeDtypeStruct((m, n), out_dtype),
        scratch_shapes=[pltpu.VMEM((tm, k), BF16)],
        compiler_params=_cparams("parallel", "arbitrary"),
        name=name,
    )(x, g.reshape(1, k), w)


def _rope_slab(x, cos, sin, rot):
    first_end, half, period = rot
    width = x.shape[1]
    right = pltpu.roll(x, width - half, 1)
    left = pltpu.roll(x, half, 1)
    partner = jnp.where((_lane_iota(x.shape) & (period - 1)) < first_end, right, left)
    return x * cos + partner * sin


_MLA_ROT = (MLA_NOPE + MLA_ROPE // 2, MLA_ROPE // 2, LANE)
_RET_ROT = (RET_DK // 2, RET_DK // 2, RET_DK)


def _mla_q_kernel(zq_ref, zkv_ref, zpe_ref, gq_ref, gkv_ref, wq_ref, cos_ref, sin_ref,
                  q_ref, ckv_ref, kpe_ref, kpe_out_ref):
    cos = cos_ref[...]
    sin = sin_ref[...]
    qn = _rms(zq_ref[...].astype(F32), gq_ref[...]).astype(BF16)
    q = _dot(qn, wq_ref[...])
    cos_q = cos * (MLA_SCALE * LOG2E)
    sin_q = sin * (MLA_SCALE * LOG2E)
    for h in range(MLA_HEADS):
        sl = slice(h * HEAD_SLAB, (h + 1) * HEAD_SLAB)
        pt = slice((MLA_HEADS + h) * HEAD_SLAB, (MLA_HEADS + h + 1) * HEAD_SLAB)
        q_ref[:, sl] = (q[:, sl] * cos_q + q[:, pt] * sin_q).astype(BF16)
    ckv_ref[...] = _rms(zkv_ref[...].astype(F32), gkv_ref[...])
    kpe = _rope_slab(zpe_ref[...].astype(F32), cos, sin, _MLA_ROT)
    kpe_ref[...] = kpe
    kpe_out_ref[...] = kpe[:, MLA_NOPE:MLA_NOPE + MLA_ROPE]


def mla_q_prep(z, g_q, g_kv, w_uq, cos, sin, tm, name):
    m = z.shape[0]
    nt = cos.shape[0] // tm
    row = lambda w: pl.BlockSpec((1, w), lambda i: (0, 0))
    return pl.pallas_call(
        _mla_q_kernel,
        grid=(m // tm,),
        in_specs=[
            pl.BlockSpec((tm, MLA_Q_RANK), lambda i: (i, Z_Q // MLA_Q_RANK)),
            pl.BlockSpec((tm, MLA_KV_RANK), lambda i: (i, Z_KV // MLA_KV_RANK)),
            pl.BlockSpec((tm, HEAD_SLAB), lambda i: (i, Z_PE // HEAD_SLAB)),
            row(MLA_Q_RANK), row(MLA_KV_RANK),
            pl.BlockSpec(w_uq.shape, lambda i: (0, 0)),
            pl.BlockSpec((tm, HEAD_SLAB), lambda i: (i % nt, 0)),
            pl.BlockSpec((tm, HEAD_SLAB), lambda i: (i % nt, 0)),
        ],
        out_specs=[
            pl.BlockSpec((tm, MLA_HEADS * HEAD_SLAB), lambda i: (i, 0)),
            pl.BlockSpec((tm, MLA_KV_RANK), lambda i: (i, 0)),
            pl.BlockSpec((tm, HEAD_SLAB), lambda i: (i, 0)),
            pl.BlockSpec((tm, MLA_ROPE), lambda i: (i, 0)),
        ],
        out_shape=[
            jax.ShapeDtypeStruct((m, MLA_HEADS * HEAD_SLAB), BF16),
            jax.ShapeDtypeStruct((m, MLA_KV_RANK), F32),
            jax.ShapeDtypeStruct((m, HEAD_SLAB), F32),
            jax.ShapeDtypeStruct((m, MLA_ROPE), F32),
        ],
        compiler_params=_cparams("parallel"),
        name=name,
    )(z, z, z, g_q.reshape(1, -1), g_kv.reshape(1, -1), w_uq, cos, sin)


def _mla_kv_kernel(ckv_ref, kpe_ref, wk_ref, wv_ref, k_ref, v_ref):
    c = ckv_ref[...].astype(BF16)
    kn = _dot(c, wk_ref[...])
    kpe = kpe_ref[...]
    for h in range(MLA_HEADS):
        sl = slice(h * HEAD_SLAB, (h + 1) * HEAD_SLAB)
        k_ref[:, sl] = (kn[:, sl] + kpe).astype(BF16)
    v = _dot(c, wv_ref[...])
    ones_lane = (_lane_iota(v.shape) & (HEAD_SLAB - 1)) == MLA_V
    v_ref[...] = jnp.where(ones_lane, 1.0, v).astype(BF16)


def mla_kv_up(ckv, kpe, w_uk, w_uv, tm, name):
    m = ckv.shape[0]
    return pl.pallas_call(
        _mla_kv_kernel,
        grid=(m // tm,),
        in_specs=[
            pl.BlockSpec((tm, MLA_KV_RANK), lambda i: (i, 0)),
            pl.BlockSpec((tm, HEAD_SLAB), lambda i: (i, 0)),
            pl.BlockSpec(w_uk.shape, lambda i: (0, 0)),
            pl.BlockSpec(w_uv.shape, lambda i: (0, 0)),
        ],
        out_specs=[
            pl.BlockSpec((tm, MLA_HEADS * HEAD_SLAB), lambda i: (i, 0)),
            pl.BlockSpec((tm, MLA_HEADS * HEAD_SLAB), lambda i: (i, 0)),
        ],
        out_shape=[
            jax.ShapeDtypeStruct((m, MLA_HEADS * HEAD_SLAB), BF16),
            jax.ShapeDtypeStruct((m, MLA_HEADS * HEAD_SLAB), BF16),
        ],
        compiler_params=_cparams("parallel"),
        name=name,
    )(ckv, kpe, w_uk, w_uv)


def _head_of_pair(x, hh):
    lane = _lane_iota(x.shape)
    keep = (lane < BAND_HD) if hh == 0 else (lane >= BAND_HD)
    return jnp.where(keep, x, jnp.zeros_like(x))


MLA_GROUP = 4
BAND_GROUP = 4
_PAIR_SLABS = [slice(hh * HEAD_SLAB, (hh + 1) * HEAD_SLAB) for hh in range(MLA_GROUP)]


def _mla_out(acc):
    return acc[:MLA_V, :] / acc[MLA_V:MLA_V + 1, :]


def _mla_attn_kernel(q_ref, k_ref, v_ref, o_ref, acc_ref, sa_ref, sb_ref, ma_ref, mb_ref, *, t, nq):
    sa_ref, sb_ref = (sa_ref, ma_ref), (sb_ref, mb_ref)
    shift = CHUNK.bit_length() - 1
    key_chunk = jnp.right_shift(lax.broadcasted_iota(jnp.int32, (t, t), 0), shift)
    qry_chunk = jnp.right_shift(lax.broadcasted_iota(jnp.int32, (t, t), 1), shift)
    diag = key_chunk <= qry_chunk

    def rows_of(kb):
        return pl.ds(kb * t if isinstance(kb, int) else pl.multiple_of(kb * t, t), t)

    def tile(qi, carry):
        _mla_attn_tile(qi, q_ref, k_ref, v_ref, o_ref, acc_ref, sa_ref, sb_ref, diag, rows_of, t)
        return carry

    lax.fori_loop(0, nq, tile, 0)


def _mla_attn_tile(qi, q_ref, k_ref, v_ref, o_ref, acc_ref, sa_ref, sb_ref, diag, rows_of, t):
    qs = [q_ref[0, rows_of(qi), sl] for sl in _PAIR_SLABS]

    def scores(kb, buf):
        for hh, sl in enumerate(_PAIR_SLABS):
            s = _dot_nt(k_ref[0, rows_of(kb), sl], qs[hh])
            buf[0][hh, :, :t] = s
            buf[1][hh] = jnp.max(s, axis=0, keepdims=True)

    def consume(kb, buf, ms, mask):
        new_m = []
        for hh, sl in enumerate(_PAIR_SLABS):
            s = buf[0][hh, :, :t]
            if mask is None:
                blk_max = buf[1][hh]
            else:
                s = jnp.where(mask, s, NEG_INF)
                blk_max = jnp.max(s, axis=0, keepdims=True)
            m = jnp.maximum(ms[hh], blk_max)
            alpha = jnp.exp2(ms[hh] - m)
            p = jnp.exp2(s - m).astype(BF16)
            acc_ref[hh] = alpha * acc_ref[hh] + _dot_tn(v_ref[0, rows_of(kb), sl], p)
            new_m.append(m)
        return tuple(new_m)

    def finish():
        outs = [_mla_out(acc_ref[hh]) for hh in range(MLA_GROUP)]
        o_ref[0, rows_of(qi), :] = jnp.concatenate(outs, axis=0).T.astype(o_ref.dtype)

    acc_ref[...] = jnp.zeros_like(acc_ref)
    ms = tuple(jnp.full((1, t), NEG_INF, F32) for _ in range(MLA_GROUP))
    scores(0, sa_ref)

    def pair(j, ms):
        kb = 2 * j
        scores(kb + 1, sb_ref)
        ms = consume(kb, sa_ref, ms, None)
        scores(kb + 2, sa_ref)
        return consume(kb + 1, sb_ref, ms, None)

    ms = lax.fori_loop(0, qi // 2, pair, ms)

    @pl.when(qi % 2 == 0)
    def _():
        consume(qi, sa_ref, ms, diag)
        finish()

    @pl.when(qi % 2 == 1)
    def _():
        scores(qi, sb_ref)
        consume(qi, sb_ref, consume(qi - 1, sa_ref, ms, None), diag)
        finish()


def _mla_attn_seg_kernel(q_ref, k1_ref, v1_ref, k2_ref, v2_ref, o_ref, *, n2_valid):
    for pair in range(MLA_HEADS // 2):
        outs = []
        for hh in range(2):
            h = 2 * pair + hh
            sl = slice(h * HEAD_SLAB, (h + 1) * HEAD_SLAB)
            q = q_ref[0, :, sl]
            s1 = _dot_nt(k1_ref[0, :, sl], q)
            s2 = _dot_nt(k2_ref[0, :, sl], q)
            s2 = jnp.where(lax.broadcasted_iota(jnp.int32, s2.shape, 0) < n2_valid, s2, NEG_INF)
            m = jnp.maximum(jnp.max(s1, axis=0, keepdims=True), jnp.max(s2, axis=0, keepdims=True))
            acc = (_dot_tn(v1_ref[0, :, sl], jnp.exp2(s1 - m).astype(BF16))
                   + _dot_tn(v2_ref[0, :, sl], jnp.exp2(s2 - m).astype(BF16)))
            outs.append(_mla_out(acc))
        o_ref[0, :, pair * LANE:(pair + 1) * LANE] = (
            jnp.concatenate(outs, axis=0).T.astype(o_ref.dtype))


def mla_attention_seg(q, k1, v1, k2, v2, n2_valid, name):
    b, sq, _ = q.shape
    blk = lambda a: pl.BlockSpec((1,) + a.shape[1:], lambda b_: (b_, 0, 0))
    return pl.pallas_call(
        functools.partial(_mla_attn_seg_kernel, n2_valid=n2_valid),
        grid=(b,),
        in_specs=[blk(q), blk(k1), blk(v1), blk(k2), blk(v2)],
        out_specs=pl.BlockSpec((1, sq, MLA_HEADS * MLA_V), lambda b_: (b_, 0, 0)),
        out_shape=jax.ShapeDtypeStruct((b, sq, MLA_HEADS * MLA_V), BF16),
        compiler_params=_cparams("parallel"),
        name=name,
    )(q, k1, v1, k2, v2)


def mla_attention(q, k, v, t, name):
    b, s, _ = q.shape
    assert s % t == 0 and k.shape[1] == s
    g = MLA_GROUP
    whole = pl.BlockSpec((1, s, g * HEAD_SLAB), lambda b_, h: (b_, 0, h))
    return pl.pallas_call(
        functools.partial(_mla_attn_kernel, t=t, nq=s // t),
        grid=(b, MLA_HEADS // g),
        in_specs=[whole, whole, whole],
        out_specs=pl.BlockSpec((1, s, g * MLA_V), lambda b_, h: (b_, 0, h)),
        out_shape=jax.ShapeDtypeStruct((b, s, MLA_HEADS * MLA_V), BF16),
        scratch_shapes=[pltpu.VMEM((g, HEAD_SLAB, t), F32), pltpu.VMEM((g, t, t + LANE), F32),
                        pltpu.VMEM((g, t, t + LANE), F32), pltpu.VMEM((g, 1, t), F32),
                        pltpu.VMEM((g, 1, t), F32)],
        compiler_params=_cparams("parallel", "parallel"),
        name=name,
    )(q, k, v)


def _band_attn_kernel(q_ref, k_ref, v_ref, bias_ref, o_ref, sa_ref, sb_ref, ma_ref, mb_ref, *,
                      tq, nq, span, sliding):
    def geom(i):
        if not sliding:
            return 0, span, 0
        if isinstance(i, int):
            n = min(i + 1, span // tq) * tq
            return max(i + 1 - span // tq, 0) * tq, n, span - n
        return pl.multiple_of((i + 1 - span // tq) * tq, tq), span, 0

    def q_rows(i):
        return pl.ds(i * tq if isinstance(i, int) else pl.multiple_of(i * tq, tq), tq)

    sa_ref, sb_ref = (sa_ref, ma_ref), (sb_ref, mb_ref)

    n_pairs = q_ref.shape[-1] // LANE
    pair_lanes = [slice(p * LANE, (p + 1) * LANE) for p in range(n_pairs)]

    def scores(i, buf):
        start, n, boff = geom(i)
        for pr, lanes in enumerate(pair_lanes):
            q = q_ref[0, q_rows(i), lanes]
            k = k_ref[0, pl.ds(start, n), lanes]
            for hh in range(2):
                h = 2 * pr + hh
                s = _dot_nt(k, _head_of_pair(q, hh)) + bias_ref[0, h, boff:boff + n, :]
                buf[0][h, :n, :tq] = s
                buf[1][h] = jnp.max(s, axis=0, keepdims=True)

    def consume(i, buf):
        start, n, _ = geom(i)
        tiles = []
        for pr, lanes in enumerate(pair_lanes):
            v = v_ref[0, pl.ds(start, n), lanes]
            outs = []
            for hh in range(2):
                h = 2 * pr + hh
                p = jnp.exp2(buf[0][h, :n, :tq] - buf[1][h])
                l = jnp.sum(p, axis=0, keepdims=True)
                outs.append(_dot_tn(v, p.astype(BF16)) / l)
            first = lax.broadcasted_iota(jnp.int32, outs[0].shape, 0) < BAND_HD
            tiles.append(jnp.where(first, outs[0], outs[1]))
        o_ref[0, q_rows(i), :] = jnp.concatenate(tiles, axis=0).T.astype(o_ref.dtype)

    scores(0, sa_ref)
    if nq == 1:
        consume(0, sa_ref)
        return
    scores(1, sb_ref)
    consume(0, sa_ref)
    scores(2, sa_ref)
    consume(1, sb_ref)

    def pair(j, carry):
        i = 2 * j + 2
        scores(i + 1, sb_ref)
        consume(i, sa_ref)
        scores(i + 2, sa_ref)
        consume(i + 1, sb_ref)
        return carry

    lax.fori_loop(0, (nq - 2) // 2 - 1, pair, 0)
    scores(nq - 1, sb_ref)
    consume(nq - 2, sa_ref)
    consume(nq - 1, sb_ref)


def band_attention(q, q_col, k, k_col, v, v_col, bias, tq, sliding, name):
    b, sq = q.shape[:2]
    sk = k.shape[1]
    nq = sq // tq
    span = bias.shape[-2]
    assert (sliding and sk == sq and nq >= 4 and nq % 2 == 0) or (nq == 1 and sk == span)
    g = BAND_GROUP
    w = g * BAND_HD
    assert q_col * LANE % w == 0 and k_col * LANE % w == 0 and v_col * LANE % w == 0
    col = lambda c: (lambda hg, b_: (b_, 0, c * LANE // w + hg))
    return pl.pallas_call(
        functools.partial(_band_attn_kernel, tq=tq, nq=nq, span=span, sliding=sliding),
        grid=(BAND_HEADS // g, b),
        in_specs=[
            pl.BlockSpec((1, sq, w), col(q_col)),
            pl.BlockSpec((1, sk, w), col(k_col)),
            pl.BlockSpec((1, sk, w), col(v_col)),
            pl.BlockSpec((1, g, span, tq), lambda hg, b_: (0, hg, 0, 0)),
        ],
        out_specs=pl.BlockSpec((1, sq, w), col(0)),
        out_shape=jax.ShapeDtypeStruct((b, sq, BAND_HEADS * BAND_HD), BF16),
        scratch_shapes=[pltpu.VMEM((g, span, tq + LANE), F32),
                        pltpu.VMEM((g, span, tq + LANE), F32),
                        pltpu.VMEM((g, 1, tq), F32), pltpu.VMEM((g, 1, tq), F32)],
        compiler_params=_cparams("parallel", "arbitrary"),
        name=name,
    )(q, k, v, bias.reshape((1,) + bias.shape))


def _retention_kernel(rq_ref, rk_ref, rv_ref, rg_ref, cq_ref, sq_ref, ck_ref, sk_ref,
                      dmat_ref, rowdec_ref, kw_ref, sdec_ref, gn_ref, init_ref,
                      y_ref, state_out_ref, state_ref):
    c = pl.program_id(1)

    @pl.when(c == 0)
    def _():
        state_ref[...] = init_ref[...]

    n_pair = RET_HEADS // 2
    gn = gn_ref[...]
    row_is_first = lax.broadcasted_iota(jnp.int32, (LANE, RET_DV), 0) < RET_DK
    for bi in range(rq_ref.shape[0]):
        q = _rope_slab(rq_ref[bi].astype(F32), cq_ref[...], sq_ref[...], _RET_ROT)
        k = _rope_slab(rk_ref[bi].astype(F32), ck_ref[...], sk_ref[...], _RET_ROT)
        qb = q.astype(BF16)
        kb = k.astype(BF16)
        kwb = (k * kw_ref[...]).astype(BF16)
        for p in range(n_pair):
            psl = slice(p * LANE, (p + 1) * LANE)
            st = state_ref[bi, p]
            stb = st.astype(BF16)
            kv = []
            for hh in range(2):
                h = 2 * p + hh
                vsl = slice(h * RET_DV, (h + 1) * RET_DV)
                v = rv_ref[bi, :, vsl]
                qh = _head_of_pair(qb[:, psl], hh)
                s = _dot_nt(qh, kb[:, psl]) * dmat_ref[h]
                o = _dot(s.astype(BF16), v) + _dot(qh, stb) * rowdec_ref[:, vsl]
                mu = jnp.mean(o, axis=-1, keepdims=True)
                d = o - mu
                yn = d * lax.rsqrt(jnp.mean(d * d, axis=-1, keepdims=True) + EPS)
                g = rg_ref[bi, :, vsl].astype(F32)
                y_ref[bi, :, vsl] = (yn * gn[:, vsl] * (g * _sigmoid(g))).astype(y_ref.dtype)
                kv.append(_dot_tn(kwb[:, psl], v))
            state_ref[bi, p] = st * sdec_ref[p] + jnp.where(row_is_first, kv[0], kv[1])

    @pl.when(c == pl.num_programs(1) - 1)
    def _():
        state_out_ref[...] = state_ref[...]


def retention(z, tabs, g_gn, init_state, blk, name):
    b, s = z.shape[:2]
    nc = s // blk
    nb = 2 if b % 2 == 0 else 1
    n_pair = RET_HEADS // 2
    zblk = lambda w, col: pl.BlockSpec((nb, blk, w), lambda b_, c: (b_, c, col // w))
    tab = lambda w: pl.BlockSpec((blk, w), lambda b_, c: (c, 0))
    const = lambda a: pl.BlockSpec(a.shape, lambda b_, c: (0,) * a.ndim)
    qk_w = RET_HEADS * RET_DK
    v_w = RET_HEADS * RET_DV
    return pl.pallas_call(
        _retention_kernel,
        grid=(b // nb, nc),
        in_specs=[
            zblk(qk_w, Z_RQ), zblk(qk_w, Z_RK), zblk(v_w, Z_RV), zblk(v_w, Z_RG),
            tab(qk_w), tab(qk_w), tab(qk_w), tab(qk_w),
            const(tabs["dmat"]), const(tabs["rowdec"]), const(tabs["kw"]), const(tabs["sdec"]),
            pl.BlockSpec((1, v_w), lambda b_, c: (0, 0)),
            pl.BlockSpec((nb, n_pair, LANE, RET_DV), lambda b_, c: (b_, 0, 0, 0)),
        ],
        out_specs=[
            pl.BlockSpec((nb, blk, v_w), lambda b_, c: (b_, c, 0)),
            pl.BlockSpec((nb, n_pair, LANE, RET_DV), lambda b_, c: (b_, 0, 0, 0)),
        ],
        out_shape=[
            jax.ShapeDtypeStruct((b, s, v_w), BF16),
            jax.ShapeDtypeStruct((b, n_pair, LANE, RET_DV), F32),
        ],
        scratch_shapes=[pltpu.VMEM((nb, n_pair, LANE, RET_DV), F32)],
        compiler_params=_cparams("parallel", "arbitrary"),
        name=name,
    )(z, z, z, z, tabs["cos_q"], tabs["sin_q"], tabs["cos_k"], tabs["sin_k"],
      tabs["dmat"], tabs["rowdec"], tabs["kw"], tabs["sdec"], g_gn.reshape(1, -1), init_state)


def _mix_out_kernel(oa_ref, yb_ref, oc_ref, zg0_ref, zg1_ref, zg2_ref, bg_ref, wa_ref, wb_ref,
                    wc_ref, wo_ref, g_ref, x_ref, o_ref):
    d = x_ref.shape[-1]
    merged = None
    for n, (br_ref, w_ref, zg_ref) in enumerate(
            ((oa_ref, wa_ref, zg0_ref), (yb_ref, wb_ref, zg1_ref), (oc_ref, wc_ref, zg2_ref))):
        gate = _sigmoid(zg_ref[...].astype(F32) + bg_ref[:, n * d:(n + 1) * d])
        term = gate * _dot(br_ref[...], w_ref[...])
        merged = term if merged is None else merged + term
    y = _dot(merged.astype(BF16), wo_ref[...])
    o_ref[...] = x_ref[...] + _rms(y, g_ref[...])


def mix_out(o_a, y_b, o_c, z, b_gate, w_a, w_b, w_c, w_out, g_post, x, tm, name):
    m, d = x.shape
    e = o_a.shape[1]
    act = pl.BlockSpec((tm, e), lambda i: (i, 0))
    zg = lambda n: pl.BlockSpec((tm, d), lambda i: (i, Z_G // d + n))
    const = lambda a: pl.BlockSpec(a.shape, lambda i: (0, 0))
    return pl.pallas_call(
        _mix_out_kernel,
        grid=(m // tm,),
        in_specs=[act, act, act, zg(0), zg(1), zg(2),
                  pl.BlockSpec((1, 3 * d), lambda i: (0, 0)),
                  const(w_a), const(w_b), const(w_c), const(w_out),
                  pl.BlockSpec((1, d), lambda i: (0, 0)),
                  pl.BlockSpec((tm, d), lambda i: (i, 0))],
        out_specs=pl.BlockSpec((tm, d), lambda i: (i, 0)),
        out_shape=jax.ShapeDtypeStruct((m, d), F32),
        compiler_params=_cparams("parallel"),
        name=name,
    )(o_a, y_b, o_c, z, z, z, b_gate.reshape(1, -1), w_a, w_b, w_c, w_out,
      g_post.reshape(1, -1), x)


def _mem_attn_kernel(x_ref, mk_ref, mv_ref, gpre_ref, wq_ref, wo_ref, gpost_ref, o_ref):
    x = x_ref[0]
    u = _rms(x, gpre_ref[...]).astype(BF16)
    q = _dot(u, wq_ref[...]).astype(BF16)
    outs = []
    for h in range(MEM_HEADS):
        sl = slice(h * MEM_HD, (h + 1) * MEM_HD)
        s = _dot_nt(q[:, sl], mk_ref[0, :, sl].astype(BF16)) * (MEM_HD ** -0.5 * LOG2E)
        m = jnp.max(s, axis=-1, keepdims=True)
        p = jnp.exp2(s - m)
        l = jnp.sum(p, axis=-1, keepdims=True)
        outs.append(_dot(p.astype(BF16), mv_ref[0, :, sl].astype(BF16)) / l)
    o = jnp.concatenate(outs, axis=-1).astype(BF16)
    o_ref[0] = x + _rms(_dot(o, wo_ref[...]), gpost_ref[...])


def mem_attention(x, mk, mv, g_pre, w_q, w_o, g_post, tm, name):
    b, s, d = x.shape
    const = lambda a: pl.BlockSpec(a.shape, lambda b_, i: (0, 0))
    vec = pl.BlockSpec((1, d), lambda b_, i: (0, 0))
    mem = pl.BlockSpec((1,) + mk.shape[1:], lambda b_, i: (b_, 0, 0))
    return pl.pallas_call(
        _mem_attn_kernel,
        grid=(b, s // tm),
        in_specs=[pl.BlockSpec((1, tm, d), lambda b_, i: (b_, i, 0)), mem, mem,
                  vec, const(w_q), const(w_o), vec],
        out_specs=pl.BlockSpec((1, tm, d), lambda b_, i: (b_, i, 0)),
        out_shape=jax.ShapeDtypeStruct((b, s, d), F32),
        compiler_params=_cparams("parallel", "parallel"),
        name=name,
    )(x, mk, mv, g_pre.reshape(1, -1), w_q, w_o, g_post.reshape(1, -1))


def _mlp_kernel(x_ref, gpre_ref, wu_ref, wd_ref, gpost_ref, o_ref, xn_ref, acc_ref):
    j = pl.program_id(1)
    last = pl.num_programs(1) - 1

    def partial_out(xn):
        h = jnp.square(jnp.maximum(_dot(xn, wu_ref[...]), 0.0))
        return _dot(h.astype(BF16), wd_ref[...])

    @pl.when(j == 0)
    def _():
        xn = _rms(x_ref[...], gpre_ref[...]).astype(BF16)
        xn_ref[...] = xn
        acc_ref[...] = partial_out(xn)

    @pl.when(jnp.logical_and(j > 0, j < last))
    def _():
        acc_ref[...] += partial_out(xn_ref[...])

    @pl.when(j == last)
    def _():
        y = acc_ref[...] + partial_out(xn_ref[...])
        o_ref[...] = x_ref[...] + _rms(y, gpost_ref[...])


def mlp(x, g_pre, w_up, w_down, g_post, tm, tf, name):
    m, d = x.shape
    f = w_up.shape[1]
    vec = pl.BlockSpec((1, d), lambda i, j: (0, 0))
    return pl.pallas_call(
        _mlp_kernel,
        grid=(m // tm, f // tf),
        in_specs=[pl.BlockSpec((tm, d), lambda i, j: (i, 0)), vec,
                  pl.BlockSpec((d, tf), lambda i, j: (0, j)),
                  pl.BlockSpec((tf, d), lambda i, j: (j, 0)), vec],
        out_specs=pl.BlockSpec((tm, d), lambda i, j: (i, 0)),
        out_shape=jax.ShapeDtypeStruct((m, d), F32),
        scratch_shapes=[pltpu.VMEM((tm, d), BF16), pltpu.VMEM((tm, d), F32)],
        compiler_params=_cparams("parallel", "arbitrary"),
        name=name,
    )(x, g_pre.reshape(1, -1), w_up, w_down, g_post.reshape(1, -1))


def _rope_angles(pos, half):
    inv = ROPE_THETA ** (-jnp.arange(half, dtype=F32) / half)
    ang = pos.astype(F32)[:, None] * inv[None, :]
    return jnp.cos(ang), jnp.sin(ang)


def _mla_rope_tables(pos):
    cos, sin = _rope_angles(pos, MLA_ROPE // 2)
    t = pos.shape[0]
    one = jnp.ones((t, MLA_NOPE), F32)
    zero64 = jnp.zeros((t, MLA_NOPE), F32)
    pad = jnp.zeros((t, HEAD_SLAB - MLA_NOPE - MLA_ROPE), F32)
    return (jnp.concatenate([one, cos, cos, pad], axis=1),
            jnp.concatenate([zero64, -sin, sin, pad], axis=1))


def _ret_tables(pos, blk, n_real):
    cos, sin = _rope_angles(pos, RET_DK // 2)
    cos_q = jnp.tile(jnp.concatenate([cos, cos], axis=1), (1, RET_HEADS))
    sin_q = jnp.tile(jnp.concatenate([-sin, sin], axis=1), (1, RET_HEADS))
    k_scale = RET_DK ** -0.5
    log_g = jnp.log1p(-jnp.exp2(-5.0 - jnp.arange(RET_HEADS, dtype=F32)))
    idx = jnp.arange(blk, dtype=F32)
    diff = idx[:, None] - idx[None, :]
    dmat = jnp.where(diff >= 0, jnp.exp(log_g[:, None, None] * jnp.maximum(diff, 0.0)), 0.0)
    rowdec = jnp.exp(log_g[None, :] * (idx[:, None] + 1.0))
    w = jnp.where(idx[:, None] < n_real,
                  jnp.exp(log_g[None, :] * jnp.maximum(n_real - 1.0 - idx[:, None], 0.0)), 0.0)
    sdec = jnp.exp(log_g * n_real)
    n_pair = RET_HEADS // 2
    return dict(
        cos_q=cos_q, sin_q=sin_q, cos_k=cos_q * k_scale, sin_k=sin_q * k_scale,
        dmat=dmat,
        rowdec=jnp.repeat(rowdec, RET_DV, axis=1),
        kw=jnp.repeat(w, RET_DK, axis=1),
        sdec=jnp.broadcast_to(jnp.repeat(sdec, RET_DK).reshape(n_pair, LANE, 1),
                              (n_pair, LANE, RET_DV)),
    )


def _band_bias(table, tq, span, q_off, allowed):
    sub = 8
    assert span % sub == 0
    length = tq + span - 1
    d = np.arange(length)
    ext = table[:, np.clip(d + q_off - span + 1, -MAX_REL, MAX_REL) + MAX_REL].astype(F32) * LOG2E
    shifted = jnp.stack([ext[:, sub - 1 - r:length - r] for r in range(sub)], axis=1)
    tile = jnp.concatenate([shifted[:, :, span - sub * (a + 1):span - sub * (a + 1) + tq]
                            for a in range(span // sub)], axis=1)
    return jnp.where(allowed[None], tile, NEG_INF)


def _layer_weights(l, w_in, w_mla_uq, w_mla_ukv, w_br_a, w_br_b, w_br_c, w_out, w_mem_q, w_mem_k,
                   w_mem_v, w_mem_o, w_up, w_down):
    d = w_in.shape[1]
    parts, start = [], 0
    for n in (MLA_Q_RANK, MLA_KV_RANK, MLA_ROPE, 256, 256, 512, 512, 512, 512, 512, 3 * d):
        parts.append(w_in[l, :, start:start + n])
        start += n
    zq, zkv, zpe, rq, rk, rv, rg, cq, ck, cv, zg = parts
    cq = cq * (BAND_HD ** -0.5 * LOG2E)
    zeros = lambda n: jnp.zeros((d, n), w_in.dtype)
    w_in_l = jnp.concatenate(
        [zg, cq, ck, cv, rv, rg, rq, rk, zkv, zeros(MLA_NOPE), zpe,
         zeros(HEAD_SLAB - MLA_NOPE - MLA_ROPE), zq], axis=1).astype(BF16)
    assert w_in_l.shape[1] == Z_WIDTH
    pad_head = lambda w: jnp.pad(w, ((0, 0), (0, 0), (0, HEAD_SLAB - w.shape[-1])))
    flat = lambda w: w.reshape(w.shape[0], -1).astype(BF16)
    return dict(
        w_in=w_in_l,
        w_uq=jnp.concatenate([flat(pad_head(w_mla_uq[l])), flat(pad_head(jnp.concatenate(
            [jnp.zeros_like(w_mla_uq[l][..., :MLA_NOPE]),
             w_mla_uq[l][..., MLA_NOPE + MLA_ROPE // 2:],
             w_mla_uq[l][..., MLA_NOPE:MLA_NOPE + MLA_ROPE // 2]], axis=-1)))], axis=1),
        w_uk=flat(pad_head(w_mla_ukv[l][..., :MLA_NOPE])),
        w_uv=flat(pad_head(w_mla_ukv[l][..., MLA_NOPE:])),
        w_a=w_br_a[l].astype(BF16), w_b=w_br_b[l].astype(BF16), w_c=w_br_c[l].astype(BF16),
        w_out=w_out[l].astype(BF16),
        w_mq=flat(w_mem_q[l]),
        w_mkv=jnp.concatenate([flat(w_mem_k[l]), flat(w_mem_v[l])], axis=1),
        w_mo=w_mem_o[l].reshape(-1, d).astype(BF16),
        w_up=w_up[l].astype(BF16), w_down=w_down[l].astype(BF16),
    )


def _tile(n, pref):
    t = min(n, pref)
    while n % t:
        t -= LANE
    return t


def _trunk_layer(x, w, P, l, tabs, mem_k, mem_v, past, tag):
    b, s, d = x.shape
    m = b * s
    x2 = x.reshape(m, d)
    tm = _tile(m, 1024)
    tm2 = _tile(m, 512)
    z = norm_matmul(x2, P["g_pre_mix"][l], w["w_in"], BF16, tm, 2304, f"in_proj_{tag}")
    q, ckv, kpe, kpe_out = mla_q_prep(z, P["g_mla_q"][l], P["g_mla_kv"][l], w["w_uq"],
                             tabs["mla_cos"], tabs["mla_sin"], tm2, f"mla_q_{tag}")
    z3 = z.reshape(b, s, Z_WIDTH)
    q3 = q.reshape(b, s, -1)
    if past is None:
        k, v = mla_kv_up(ckv, kpe, w["w_uk"], w["w_uv"], tm, f"mla_kv_{tag}")
        o_a = mla_attention(q3, k.reshape(b, s, -1), v.reshape(b, s, -1), _tile(s, 512),
                            f"mla_attn_{tag}")
        init = jnp.zeros((b, RET_HEADS // 2, LANE, RET_DV), F32)
        y_b, state = retention(z3, tabs["ret"], P["g_ret_gn"][l], init, tabs["ret_blk"],
                               f"retention_{tag}")
        o_c = band_attention(z3, Z_CQ // LANE, z3, Z_CK // LANE, z3, Z_CV // LANE,
                             tabs["band_bias"][l], tabs["band_tq"], True, f"band_{tag}")
        n_real = s
        band_k = z3[:, s - BAND_WINDOW:, Z_CK:Z_CK + 512]
        band_v = z3[:, s - BAND_WINDOW:, Z_CV:Z_CV + 512]
    else:
        c_ckv, c_kpe, s_ret, c_bk, c_bv = past
        n_real = CHUNK
        n_past = c_ckv.shape[1]
        kpe_pad = jnp.pad(c_kpe, ((0, 0), (0, 0), (MLA_NOPE, HEAD_SLAB - MLA_NOPE - MLA_ROPE)))
        k_c, v_c = mla_kv_up(c_ckv.reshape(b * n_past, -1), kpe_pad.reshape(b * n_past, -1),
                             w["w_uk"], w["w_uv"], _tile(b * n_past, 1024), f"mla_kv_cache_{tag}")
        k_n, v_n = mla_kv_up(ckv, kpe, w["w_uk"], w["w_uv"], tm, f"mla_kv_{tag}")
        per_b = lambda a: a.reshape(b, -1, a.shape[-1])
        o_a = mla_attention_seg(q3, per_b(k_c), per_b(v_c), per_b(k_n), per_b(v_n), n_real,
                                f"mla_attn_{tag}")
        init = s_ret.astype(F32).reshape(b, RET_HEADS // 2, LANE, RET_DV)
        y_b, state = retention(z3, tabs["ret"], P["g_ret_gn"][l], init, s, f"retention_{tag}")
        band_k = z3[:, :n_real, Z_CK:Z_CK + 512]
        band_v = z3[:, :n_real, Z_CV:Z_CV + 512]
        span = tabs["band_bias"][l].shape[-2]
        w_band = c_bk.shape[1]
        catb = lambda c, n: jnp.pad(
            jnp.concatenate([c.reshape(b, w_band, -1).astype(BF16), n], axis=1),
            ((0, 0), (0, span - w_band - n_real), (0, 0)))
        o_c = band_attention(z3, Z_CQ // LANE, catb(c_bk, band_k), 0, catb(c_bv, band_v), 0,
                             tabs["band_bias"][l], s, False, f"band_{tag}")
    x2 = mix_out(o_a.reshape(m, -1), y_b.reshape(m, -1), o_c.reshape(m, -1), z, P["b_gate"][l],
                 w["w_a"], w["w_b"], w["w_c"], w["w_out"], P["g_post_mix"][l], x2, tm2,
                 f"mix_out_{tag}")
    x3 = mem_attention(x2.reshape(b, s, d), mem_k, mem_v, P["g_pre_mem"][l], w["w_mq"], w["w_mo"],
                       P["g_post_mem"][l], _tile(s, 512), f"mem_attn_{tag}")
    x4 = mlp(x3.reshape(m, d), P["g_pre_ff"][l], w["w_up"], w["w_down"], P["g_post_ff"][l],
             tm, 1024, f"mlp_{tag}")
    new = (ckv.reshape(b, s, -1)[:, :n_real],
           kpe_out.reshape(b, s, -1)[:, :n_real],
           state.reshape(b, RET_HEADS, RET_DK, RET_DV),
           band_k.astype(F32).reshape(b, -1, BAND_HEADS, BAND_HD),
           band_v.astype(F32).reshape(b, -1, BAND_HEADS, BAND_HD))
    return x4.reshape(b, s, d), new


def kernel(x_prompt, x_sample, cache_mla_ckv, cache_mla_kpe, state_ret, cache_band_k, cache_band_v, cache_mem_k, cache_mem_v, mem_prompt, g_pre_mix, w_in, g_mla_q, w_mla_uq, g_mla_kv, w_mla_ukv, g_ret_gn, band_rel_bias, w_br_a, w_br_b, w_br_c, b_gate, w_out, g_post_mix, g_pre_mem, g_mem, w_mem_q, w_mem_k, w_mem_v, w_mem_o, g_post_mem, g_pre_ff, w_up, w_down, g_post_ff):
    P = dict(g_pre_mix=g_pre_mix, g_mla_q=g_mla_q, g_mla_kv=g_mla_kv, g_ret_gn=g_ret_gn,
             b_gate=b_gate, g_post_mix=g_post_mix, g_pre_mem=g_pre_mem, g_post_mem=g_post_mem,
             g_pre_ff=g_pre_ff, g_post_ff=g_post_ff)
    depth = w_in.shape[0]
    bp, sp, d = x_prompt.shape
    bs, ss, _ = x_sample.shape
    n_past = cache_mla_ckv.shape[2]
    w_band = cache_band_k.shape[2]
    assert ss == CHUNK and sp % 512 == 0 and sp >= BAND_WINDOW
    s_pad = 2 * CHUNK

    pos_p = jnp.arange(sp)
    pos_s = n_past + jnp.arange(s_pad)
    ret_blk = 256
    band_tq = 256
    cos_p, sin_p = _mla_rope_tables(pos_p)
    cos_s, sin_s = _mla_rope_tables(pos_s)

    span_p = 3 * band_tq
    jj = np.arange(span_p)[:, None]
    ii = np.arange(band_tq)[None, :]
    band_ok = (jj // CHUNK >= ii // CHUNK) & (jj // CHUNK <= ii // CHUNK + BAND_PREV_CHUNKS)
    bias_p = [_band_bias(band_rel_bias[l], band_tq, span_p, 2 * band_tq, band_ok)
              for l in range(depth)]
    span_s = -(-(w_band + CHUNK) // LANE) * LANE
    mask_s = np.broadcast_to(np.arange(span_s)[:, None] < w_band + CHUNK, (span_s, s_pad))
    bias_s = [_band_bias(band_rel_bias[l], s_pad, span_s, w_band, mask_s) for l in range(depth)]

    tabs_p = dict(mla_cos=cos_p, mla_sin=sin_p, ret=_ret_tables(pos_p, ret_blk, ret_blk),
                  ret_blk=ret_blk, band_bias=bias_p, band_tq=band_tq)
    tabs_s = dict(mla_cos=jnp.tile(cos_s, (bs, 1)), mla_sin=jnp.tile(sin_s, (bs, 1)),
                  ret=_ret_tables(pos_s, s_pad, CHUNK), band_bias=bias_s)

    xp = x_prompt
    xs = jnp.pad(x_sample, ((0, 0), (0, s_pad - ss), (0, 0)))
    mem2 = mem_prompt.reshape(-1, d)
    new_p = [[] for _ in range(7)]
    new_s = [[] for _ in range(5)]
    for l in range(depth):
        w = _layer_weights(l, w_in, w_mla_uq, w_mla_ukv, w_br_a, w_br_b, w_br_c, w_out, w_mem_q,
                           w_mem_k, w_mem_v, w_mem_o, w_up, w_down)
        mkv = norm_matmul(mem2, g_mem[l], w["w_mkv"], F32, _tile(mem2.shape[0], 1024), 512,
                          f"mem_kv_{l}")
        e = MEM_HEADS * MEM_HD
        mk = mkv[:, :e].reshape(bp, -1, e)
        mv = mkv[:, e:].reshape(bp, -1, e)
        xp, st_p = _trunk_layer(xp, w, P, l, tabs_p, mk, mv, None, f"p{l}")
        xs, st_s = _trunk_layer(xs, w, P, l, tabs_s, cache_mem_k[l].reshape(bs, -1, e),
                                cache_mem_v[l].reshape(bs, -1, e),
                                (cache_mla_ckv[l], cache_mla_kpe[l], state_ret[l], cache_band_k[l],
                                 cache_band_v[l]), f"s{l}")
        mem_shape = (bp, -1, MEM_HEADS, MEM_HD)
        for acc, t in zip(new_p, st_p + (mk.reshape(mem_shape), mv.reshape(mem_shape))):
            acc.append(t)
        for acc, t in zip(new_s, st_s):
            acc.append(t)
    stack = lambda ts: jnp.stack(ts, axis=0)
    return (xp, xs[:, :ss],
            stack(new_p[0]), stack(new_p[1]), stack(new_p[2]), stack(new_p[3]), stack(new_p[4]),
            stack(new_p[5]), stack(new_p[6]),
            stack(new_s[0]), stack(new_s[1]), stack(new_s[2]), stack(new_s[3]), stack(new_s[4]))
```

```python
import functools

import numpy as np
import jax
import jax.numpy as jnp
from jax import lax
from jax.experimental import pallas as pl
from jax.experimental.pallas import tpu as pltpu

F32 = jnp.float32
BF16 = jnp.bfloat16

CHUNK = 64
MLA_HEADS = 8
MLA_Q_RANK = 384
MLA_KV_RANK = 256
MLA_NOPE = 64
MLA_ROPE = 32
MLA_V = 64
MLA_SCALE = (MLA_NOPE + MLA_ROPE) ** -0.5
RET_HEADS = 4
RET_DK = 64
RET_DV = 128
BAND_HEADS = 8
BAND_HD = 64
BAND_PREV_CHUNKS = 8
BAND_WINDOW = BAND_PREV_CHUNKS * CHUNK
MAX_REL = 128
MEM_HEADS = 4
MEM_HD = 128
ROPE_THETA = 10000.0
EPS = 1e-6
NEG_INF = -1e30
LOG2E = 1.4426950408889634

LANE = 128
HEAD_SLAB = 128
VMEM_LIMIT = 48 * 1024 * 1024

Z_G = 0
Z_CQ = 3072
Z_CK = 3584
Z_CV = 4096
Z_RV = 4608
Z_RG = 5120
Z_RQ = 5632
Z_RK = 5888
Z_KV = 6144
Z_PE = 6400
Z_Q = 6528
Z_WIDTH = 6912


def _cparams(*sem, **kw):
    return pltpu.CompilerParams(dimension_semantics=sem, vmem_limit_bytes=VMEM_LIMIT, **kw)


def _rms(x, g):
    return x * lax.rsqrt(jnp.mean(x * x, axis=-1, keepdims=True) + EPS) * g


def _dot(a, b):
    return jnp.dot(a, b, preferred_element_type=F32)


def _dot_nt(a, b):
    return lax.dot_general(a, b, (((1,), (1,)), ((), ())), preferred_element_type=F32)


def _dot_tn(a, b):
    return lax.dot_general(a, b, (((0,), (0,)), ((), ())), preferred_element_type=F32)


def _sigmoid(x):
    return 0.5 * jnp.tanh(0.5 * x) + 0.5


def _lane_iota(shape):
    return lax.broadcasted_iota(jnp.int32, shape, len(shape) - 1)


def _norm_matmul_kernel(x_ref, g_ref, w_ref, o_ref, xn_ref):
    @pl.when(pl.program_id(1) == 0)
    def _():
        xn = _rms(x_ref[...].astype(F32), g_ref[...]).astype(BF16)
        xn_ref[...] = xn
        o_ref[...] = _dot(xn, w_ref[...]).astype(o_ref.dtype)

    @pl.when(pl.program_id(1) > 0)
    def _():
        o_ref[...] = _dot(xn_ref[...], w_ref[...]).astype(o_ref.dtype)


def norm_matmul(x, g, w, out_dtype, tm, tn, name):
    m, k = x.shape
    n = w.shape[1]
    assert m % tm == 0 and n % tn == 0, (m, tm, n, tn)
    return pl.pallas_call(
        _norm_matmul_kernel,
        grid=(m // tm, n // tn),
        in_specs=[
            pl.BlockSpec((tm, k), lambda i, j: (i, 0)),
            pl.BlockSpec((1, k), lambda i, j: (0, 0)),
            pl.BlockSpec((k, tn), lambda i, j: (0, j)),
        ],
        out_specs=pl.BlockSpec((tm, tn), lambda i, j: (i, j)),
        out_shape=jax.ShapeDtypeStruct((m, n), out_dtype),
        scratch_shapes=[pltpu.VMEM((tm, k), BF16)],
        compiler_params=_cparams("parallel", "arbitrary"),
        name=name,
    )(x, g.reshape(1, k), w)


def _rope_slab(x, cos, sin, rot):
    first_end, half, period = rot
    width = x.shape[1]
    right = pltpu.roll(x, width - half, 1)
    left = pltpu.roll(x, half, 1)
    partner = jnp.where((_lane_iota(x.shape) & (period - 1)) < first_end, right, left)
    return x * cos + partner * sin


_MLA_ROT = (MLA_NOPE + MLA_ROPE // 2, MLA_ROPE // 2, LANE)
_RET_ROT = (RET_DK // 2, RET_DK // 2, RET_DK)


_Z_MLA = Z_KV
_Z_MLA_W = Z_WIDTH - Z_KV
assert (Z_KV, Z_PE, Z_Q) == (_Z_MLA, _Z_MLA + MLA_KV_RANK, _Z_MLA + MLA_KV_RANK + HEAD_SLAB)
assert _Z_MLA % _Z_MLA_W == 0


def _mla_kv_heads(ckv, kpe, wk_ref, wv_ref, k_ref, v_ref):
    c = ckv.astype(BF16)
    kn = _dot(c, wk_ref[...])
    for h in range(MLA_HEADS):
        sl = slice(h * HEAD_SLAB, (h + 1) * HEAD_SLAB)
        k_ref[:, sl] = (kn[:, sl] + kpe).astype(BF16)
    v = _dot(c, wv_ref[...])
    ones_lane = (_lane_iota(v.shape) & (HEAD_SLAB - 1)) == MLA_V
    v_ref[...] = jnp.where(ones_lane, 1.0, v).astype(BF16)


def _mla_prep_kernel(z_ref, gq_ref, gkv_ref, wq_ref, wk_ref, wv_ref, cos_ref, sin_ref,
                     q_ref, ckv_ref, kpe_out_ref, k_ref, v_ref):
    cos = cos_ref[...]
    sin = sin_ref[...]
    zkv = z_ref[:, :MLA_KV_RANK]
    zpe = z_ref[:, MLA_KV_RANK:MLA_KV_RANK + HEAD_SLAB]
    zq = z_ref[:, MLA_KV_RANK + HEAD_SLAB:]
    qn = _rms(zq.astype(F32), gq_ref[...]).astype(BF16)
    q = _dot(qn, wq_ref[...])
    cos_q = cos * (MLA_SCALE * LOG2E)
    sin_q = sin * (MLA_SCALE * LOG2E)
    for h in range(MLA_HEADS):
        sl = slice(h * HEAD_SLAB, (h + 1) * HEAD_SLAB)
        pt = slice((MLA_HEADS + h) * HEAD_SLAB, (MLA_HEADS + h + 1) * HEAD_SLAB)
        q_ref[:, sl] = (q[:, sl] * cos_q + q[:, pt] * sin_q).astype(BF16)
    ckv = _rms(zkv.astype(F32), gkv_ref[...])
    ckv_ref[...] = ckv
    kpe = _rope_slab(zpe.astype(F32), cos, sin, _MLA_ROT)
    kpe_out_ref[...] = kpe[:, MLA_NOPE:MLA_NOPE + MLA_ROPE]
    _mla_kv_heads(ckv, kpe, wk_ref, wv_ref, k_ref, v_ref)


def mla_prep(z, g_q, g_kv, w_uq, w_uk, w_uv, cos, sin, tm, name):
    m = z.shape[0]
    nt = cos.shape[0] // tm
    row = lambda w: pl.BlockSpec((1, w), lambda i: (0, 0))
    const = lambda a: pl.BlockSpec(a.shape, lambda i: (0, 0))
    slabs = MLA_HEADS * HEAD_SLAB
    out = lambda w: pl.BlockSpec((tm, w), lambda i: (i, 0))
    return pl.pallas_call(
        _mla_prep_kernel,
        grid=(m // tm,),
        in_specs=[
            pl.BlockSpec((tm, _Z_MLA_W), lambda i: (i, _Z_MLA // _Z_MLA_W)),
            row(MLA_Q_RANK), row(MLA_KV_RANK), const(w_uq), const(w_uk), const(w_uv),
            pl.BlockSpec((tm, HEAD_SLAB), lambda i: (i % nt, 0)),
            pl.BlockSpec((tm, HEAD_SLAB), lambda i: (i % nt, 0)),
        ],
        out_specs=[out(slabs), out(MLA_KV_RANK), out(MLA_ROPE), out(slabs), out(slabs)],
        out_shape=[
            jax.ShapeDtypeStruct((m, slabs), BF16),
            jax.ShapeDtypeStruct((m, MLA_KV_RANK), F32),
            jax.ShapeDtypeStruct((m, MLA_ROPE), F32),
            jax.ShapeDtypeStruct((m, slabs), BF16),
            jax.ShapeDtypeStruct((m, slabs), BF16),
        ],
        compiler_params=_cparams("parallel"),
        name=name,
    )(z, g_q.reshape(1, -1), g_kv.reshape(1, -1), w_uq, w_uk, w_uv, cos, sin)


def _mla_kv_kernel(ckv_ref, kpe_ref, wk_ref, wv_ref, k_ref, v_ref):
    _mla_kv_heads(ckv_ref[...], kpe_ref[...], wk_ref, wv_ref, k_ref, v_ref)


def mla_kv_up(ckv, kpe, w_uk, w_uv, tm, name):
    m = ckv.shape[0]
    return pl.pallas_call(
        _mla_kv_kernel,
        grid=(m // tm,),
        in_specs=[
            pl.BlockSpec((tm, MLA_KV_RANK), lambda i: (i, 0)),
            pl.BlockSpec((tm, HEAD_SLAB), lambda i: (i, 0)),
            pl.BlockSpec(w_uk.shape, lambda i: (0, 0)),
            pl.BlockSpec(w_uv.shape, lambda i: (0, 0)),
        ],
        out_specs=[
            pl.BlockSpec((tm, MLA_HEADS * HEAD_SLAB), lambda i: (i, 0)),
            pl.BlockSpec((tm, MLA_HEADS * HEAD_SLAB), lambda i: (i, 0)),
        ],
        out_shape=[
            jax.ShapeDtypeStruct((m, MLA_HEADS * HEAD_SLAB), BF16),
            jax.ShapeDtypeStruct((m, MLA_HEADS * HEAD_SLAB), BF16),
        ],
        compiler_params=_cparams("parallel"),
        name=name,
    )(ckv, kpe, w_uk, w_uv)


def _head_of_pair(x, hh):
    lane = _lane_iota(x.shape)
    keep = (lane < BAND_HD) if hh == 0 else (lane >= BAND_HD)
    return jnp.where(keep, x, jnp.zeros_like(x))


MLA_GROUP = 4
BAND_GROUP = 4
_PAIR_SLABS = [slice(hh * HEAD_SLAB, (hh + 1) * HEAD_SLAB) for hh in range(MLA_GROUP)]


def _mla_out(acc):
    return acc[:MLA_V, :] / acc[MLA_V:MLA_V + 1, :]


def _mla_attn_kernel(q_ref, k_ref, v_ref, o_ref, acc_ref, sa_ref, sb_ref, ma_ref, mb_ref, *, t, nq):
    sa_ref, sb_ref = (sa_ref, ma_ref), (sb_ref, mb_ref)
    shift = CHUNK.bit_length() - 1
    key_chunk = jnp.right_shift(lax.broadcasted_iota(jnp.int32, (t, t), 0), shift)
    qry_chunk = jnp.right_shift(lax.broadcasted_iota(jnp.int32, (t, t), 1), shift)
    diag = key_chunk <= qry_chunk

    def rows_of(kb):
        return pl.ds(kb * t if isinstance(kb, int) else pl.multiple_of(kb * t, t), t)

    def scores(qt, kb, buf):
        for hh, sl in enumerate(_PAIR_SLABS):
            s = _dot_nt(k_ref[0, rows_of(kb), sl], q_ref[0, rows_of(qt), sl])
            buf[0][hh, :, :t] = s
            buf[1][hh] = jnp.max(s, axis=0, keepdims=True)

    def tile(qi, carry):
        _mla_attn_tile(qi, nq, scores, k_ref, v_ref, o_ref, acc_ref, sa_ref, sb_ref, diag, rows_of,
                       t)
        return carry

    acc_ref[...] = jnp.zeros_like(acc_ref)
    scores(0, 0, sa_ref)
    lax.fori_loop(0, nq, tile, 0)


def _mla_attn_tile(qi, nq, scores_of, k_ref, v_ref, o_ref, acc_ref, sa_ref, sb_ref, diag, rows_of,
                   t):
    scores = functools.partial(scores_of, qi)
    next_first = lambda: scores_of(jnp.minimum(qi + 1, nq - 1), 0, sa_ref)

    def consume(kb, buf, ms, mask):
        new_m = []
        for hh, sl in enumerate(_PAIR_SLABS):
            s = buf[0][hh, :, :t]
            if mask is None:
                blk_max = buf[1][hh]
            else:
                s = jnp.where(mask, s, NEG_INF)
                blk_max = jnp.max(s, axis=0, keepdims=True)
            m = jnp.maximum(ms[hh], blk_max)
            alpha = jnp.exp2(ms[hh] - m)
            p = jnp.exp2(s - m).astype(BF16)
            acc_ref[hh] = alpha * acc_ref[hh] + _dot_tn(v_ref[0, rows_of(kb), sl], p)
            new_m.append(m)
        return tuple(new_m)

    def finish():
        outs = [_mla_out(acc_ref[hh]) for hh in range(MLA_GROUP)]
        o_ref[0, rows_of(qi), :] = jnp.concatenate(outs, axis=0).T.astype(o_ref.dtype)

    ms = tuple(jnp.full((1, t), NEG_INF, F32) for _ in range(MLA_GROUP))

    def pair(j, ms):
        kb = 2 * j
        scores(kb + 1, sb_ref)
        ms = consume(kb, sa_ref, ms, None)
        scores(kb + 2, sa_ref)
        return consume(kb + 1, sb_ref, ms, None)

    ms = lax.fori_loop(0, qi // 2, pair, ms)

    @pl.when(qi % 2 == 0)
    def _():
        consume(qi, sa_ref, ms, diag)
        next_first()
        finish()

    @pl.when(qi % 2 == 1)
    def _():
        scores(qi, sb_ref)
        ms1 = consume(qi - 1, sa_ref, ms, None)
        next_first()
        consume(qi, sb_ref, ms1, diag)
        finish()


def _mla_attn_seg_kernel(q_ref, k1_ref, v1_ref, k2_ref, v2_ref, o_ref, *, n2_valid):
    for pair in range(MLA_HEADS // 2):
        outs = []
        for hh in range(2):
            h = 2 * pair + hh
            sl = slice(h * HEAD_SLAB, (h + 1) * HEAD_SLAB)
            q = q_ref[0, :, sl]
            s1 = _dot_nt(k1_ref[0, :, sl], q)
            s2 = _dot_nt(k2_ref[0, :, sl], q)
            s2 = jnp.where(lax.broadcasted_iota(jnp.int32, s2.shape, 0) < n2_valid, s2, NEG_INF)
            m = jnp.maximum(jnp.max(s1, axis=0, keepdims=True), jnp.max(s2, axis=0, keepdims=True))
            acc = (_dot_tn(v1_ref[0, :, sl], jnp.exp2(s1 - m).astype(BF16))
                   + _dot_tn(v2_ref[0, :, sl], jnp.exp2(s2 - m).astype(BF16)))
            outs.append(_mla_out(acc))
        o_ref[0, :, pair * LANE:(pair + 1) * LANE] = (
            jnp.concatenate(outs, axis=0).T.astype(o_ref.dtype))


def mla_attention_seg(q, k1, v1, k2, v2, n2_valid, name):
    b, sq, _ = q.shape
    blk = lambda a: pl.BlockSpec((1,) + a.shape[1:], lambda b_: (b_, 0, 0))
    return pl.pallas_call(
        functools.partial(_mla_attn_seg_kernel, n2_valid=n2_valid),
        grid=(b,),
        in_specs=[blk(q), blk(k1), blk(v1), blk(k2), blk(v2)],
        out_specs=pl.BlockSpec((1, sq, MLA_HEADS * MLA_V), lambda b_: (b_, 0, 0)),
        out_shape=jax.ShapeDtypeStruct((b, sq, MLA_HEADS * MLA_V), BF16),
        compiler_params=_cparams("parallel"),
        name=name,
    )(q, k1, v1, k2, v2)


def mla_attention(q, k, v, t, name):
    b, s, _ = q.shape
    assert s % t == 0 and k.shape[1] == s
    g = MLA_GROUP
    whole = pl.BlockSpec((1, s, g * HEAD_SLAB), lambda b_, h: (b_, 0, h))
    return pl.pallas_call(
        functools.partial(_mla_attn_kernel, t=t, nq=s // t),
        grid=(b, MLA_HEADS // g),
        in_specs=[whole, whole, whole],
        out_specs=pl.BlockSpec((1, s, g * MLA_V), lambda b_, h: (b_, 0, h)),
        out_shape=jax.ShapeDtypeStruct((b, s, MLA_HEADS * MLA_V), BF16),
        scratch_shapes=[pltpu.VMEM((g, HEAD_SLAB, t), F32), pltpu.VMEM((g, t, t + LANE), F32),
                        pltpu.VMEM((g, t, t + LANE), F32), pltpu.VMEM((g, 1, t), F32),
                        pltpu.VMEM((g, 1, t), F32)],
        compiler_params=_cparams("parallel", "parallel"),
        name=name,
    )(q, k, v)


def _band_attn_kernel(q_ref, k_ref, v_ref, bias_ref, o_ref, sa_ref, sb_ref, ma_ref, mb_ref, *,
                      tq, nq, span, sliding):
    def geom(i):
        if not sliding:
            return 0, span, 0
        if isinstance(i, int):
            n = min(i + 1, span // tq) * tq
            return max(i + 1 - span // tq, 0) * tq, n, span - n
        return pl.multiple_of((i + 1 - span // tq) * tq, tq), span, 0

    def q_rows(i):
        return pl.ds(i * tq if isinstance(i, int) else pl.multiple_of(i * tq, tq), tq)

    sa_ref, sb_ref = (sa_ref, ma_ref), (sb_ref, mb_ref)

    n_pairs = q_ref.shape[-1] // LANE
    pair_lanes = [slice(p * LANE, (p + 1) * LANE) for p in range(n_pairs)]

    def scores(i, buf):
        start, n, boff = geom(i)
        for pr, lanes in enumerate(pair_lanes):
            q = q_ref[0, q_rows(i), lanes]
            k = k_ref[0, pl.ds(start, n), lanes]
            for hh in range(2):
                h = 2 * pr + hh
                s = _dot_nt(k, _head_of_pair(q, hh)) + bias_ref[0, h, boff:boff + n, :]
                buf[0][h, :n, :tq] = s
                buf[1][h] = jnp.max(s, axis=0, keepdims=True)

    def consume(i, buf):
        start, n, _ = geom(i)
        tiles = []
        for pr, lanes in enumerate(pair_lanes):
            v = v_ref[0, pl.ds(start, n), lanes]
            outs = []
            for hh in range(2):
                h = 2 * pr + hh
                p = jnp.exp2(buf[0][h, :n, :tq] - buf[1][h])
                l = jnp.sum(p, axis=0, keepdims=True)
                outs.append(_dot_tn(v, p.astype(BF16)) / l)
            first = lax.broadcasted_iota(jnp.int32, outs[0].shape, 0) < BAND_HD
            tiles.append(jnp.where(first, outs[0], outs[1]))
        o_ref[0, q_rows(i), :] = jnp.concatenate(tiles, axis=0).T.astype(o_ref.dtype)

    scores(0, sa_ref)
    if nq == 1:
        consume(0, sa_ref)
        return
    scores(1, sb_ref)
    consume(0, sa_ref)
    scores(2, sa_ref)
    consume(1, sb_ref)

    def pair(j, carry):
        i = 2 * j + 2
        scores(i + 1, sb_ref)
        consume(i, sa_ref)
        scores(i + 2, sa_ref)
        consume(i + 1, sb_ref)
        return carry

    lax.fori_loop(0, (nq - 2) // 2 - 1, pair, 0)
    scores(nq - 1, sb_ref)
    consume(nq - 2, sa_ref)
    consume(nq - 1, sb_ref)


def band_attention(q, q_col, k, k_col, v, v_col, bias, tq, sliding, name):
    b, sq = q.shape[:2]
    sk = k.shape[1]
    nq = sq // tq
    span = bias.shape[-2]
    assert (sliding and sk == sq and nq >= 4 and nq % 2 == 0) or (nq == 1 and sk == span)
    g = BAND_GROUP
    w = g * BAND_HD
    assert q_col * LANE % w == 0 and k_col * LANE % w == 0 and v_col * LANE % w == 0
    col = lambda c: (lambda hg, b_: (b_, 0, c * LANE // w + hg))
    return pl.pallas_call(
        functools.partial(_band_attn_kernel, tq=tq, nq=nq, span=span, sliding=sliding),
        grid=(BAND_HEADS // g, b),
        in_specs=[
            pl.BlockSpec((1, sq, w), col(q_col)),
            pl.BlockSpec((1, sk, w), col(k_col)),
            pl.BlockSpec((1, sk, w), col(v_col)),
            pl.BlockSpec((1, g, span, tq), lambda hg, b_: (0, hg, 0, 0)),
        ],
        out_specs=pl.BlockSpec((1, sq, w), col(0)),
        out_shape=jax.ShapeDtypeStruct((b, sq, BAND_HEADS * BAND_HD), BF16),
        scratch_shapes=[pltpu.VMEM((g, span, tq + LANE), F32),
                        pltpu.VMEM((g, span, tq + LANE), F32),
                        pltpu.VMEM((g, 1, tq), F32), pltpu.VMEM((g, 1, tq), F32)],
        compiler_params=_cparams("parallel", "arbitrary"),
        name=name,
    )(q, k, v, bias.reshape((1,) + bias.shape))


def _retention_kernel(z_ref, cq_ref, sq_ref, ck_ref, sk_ref,
                      dmat_ref, rowdec_ref, kw_ref, sdec_ref, gn_ref, init_ref,
                      y_ref, state_out_ref, state_ref):
    c = pl.program_id(1)

    @pl.when(c == 0)
    def _():
        state_ref[...] = init_ref[...]

    n_pair = RET_HEADS // 2
    gn = gn_ref[...]
    row_is_first = lax.broadcasted_iota(jnp.int32, (LANE, RET_DV), 0) < RET_DK
    v_w = RET_HEADS * RET_DV
    qk_w = RET_HEADS * RET_DK
    for bi in range(z_ref.shape[0]):
        rq = z_ref[bi, :, 2 * v_w:2 * v_w + qk_w]
        rk = z_ref[bi, :, 2 * v_w + qk_w:]
        q = _rope_slab(rq.astype(F32), cq_ref[...], sq_ref[...], _RET_ROT)
        k = _rope_slab(rk.astype(F32), ck_ref[...], sk_ref[...], _RET_ROT)
        qb = q.astype(BF16)
        kb = k.astype(BF16)
        kwb = (k * kw_ref[...]).astype(BF16)
        for p in range(n_pair):
            psl = slice(p * LANE, (p + 1) * LANE)
            st = state_ref[bi, p]
            stb = st.astype(BF16)
            kv = []
            for hh in range(2):
                h = 2 * p + hh
                vsl = slice(h * RET_DV, (h + 1) * RET_DV)
                v = z_ref[bi, :, vsl]
                qh = _head_of_pair(qb[:, psl], hh)
                s = _dot_nt(qh, kb[:, psl]) * dmat_ref[h]
                o = _dot(s.astype(BF16), v) + _dot(qh, stb) * rowdec_ref[:, vsl]
                mu = jnp.mean(o, axis=-1, keepdims=True)
                d = o - mu
                yn = d * lax.rsqrt(jnp.mean(d * d, axis=-1, keepdims=True) + EPS)
                g = z_ref[bi, :, v_w + h * RET_DV:v_w + (h + 1) * RET_DV].astype(F32)
                y_ref[bi, :, vsl] = (yn * gn[:, vsl] * (g * _sigmoid(g))).astype(y_ref.dtype)
                kv.append(_dot_tn(kwb[:, psl], v))
            state_ref[bi, p] = st * sdec_ref[p] + jnp.where(row_is_first, kv[0], kv[1])

    @pl.when(c == pl.num_programs(1) - 1)
    def _():
        state_out_ref[...] = state_ref[...]


def retention(z, tabs, g_gn, init_state, blk, name):
    b, s = z.shape[:2]
    nc = s // blk
    nb = 2 if b % 2 == 0 else 1
    n_pair = RET_HEADS // 2
    tab = lambda w: pl.BlockSpec((blk, w), lambda b_, c: (c, 0))
    const = lambda a: pl.BlockSpec(a.shape, lambda b_, c: (0,) * a.ndim)
    qk_w = RET_HEADS * RET_DK
    v_w = RET_HEADS * RET_DV
    z_w = 2 * v_w + 2 * qk_w
    assert (Z_RG, Z_RQ, Z_RK) == (Z_RV + v_w, Z_RV + 2 * v_w, Z_RV + 2 * v_w + qk_w)
    assert Z_RV % z_w == 0
    return pl.pallas_call(
        _retention_kernel,
        grid=(b // nb, nc),
        in_specs=[
            pl.BlockSpec((nb, blk, z_w), lambda b_, c: (b_, c, Z_RV // z_w)),
            tab(qk_w), tab(qk_w), tab(qk_w), tab(qk_w),
            const(tabs["dmat"]), const(tabs["rowdec"]), const(tabs["kw"]), const(tabs["sdec"]),
            pl.BlockSpec((1, v_w), lambda b_, c: (0, 0)),
            pl.BlockSpec((nb, n_pair, LANE, RET_DV), lambda b_, c: (b_, 0, 0, 0)),
        ],
        out_specs=[
            pl.BlockSpec((nb, blk, v_w), lambda b_, c: (b_, c, 0)),
            pl.BlockSpec((nb, n_pair, LANE, RET_DV), lambda b_, c: (b_, 0, 0, 0)),
        ],
        out_shape=[
            jax.ShapeDtypeStruct((b, s, v_w), BF16),
            jax.ShapeDtypeStruct((b, n_pair, LANE, RET_DV), F32),
        ],
        scratch_shapes=[pltpu.VMEM((nb, n_pair, LANE, RET_DV), F32)],
        compiler_params=_cparams("parallel", "arbitrary"),
        name=name,
    )(z, tabs["cos_q"], tabs["sin_q"], tabs["cos_k"], tabs["sin_k"],
      tabs["dmat"], tabs["rowdec"], tabs["kw"], tabs["sdec"], g_gn.reshape(1, -1), init_state)


def _mix_out_kernel(oa_ref, yb_ref, oc_ref, zg_ref, bg_ref, wa_ref, wb_ref,
                    wc_ref, wo_ref, g_ref, x_ref, o_ref):
    d = x_ref.shape[-1]
    sub = min(x_ref.shape[0], 512)
    for r in range(x_ref.shape[0] // sub):
        rows = slice(r * sub, (r + 1) * sub)
        merged = None
        for n, (br_ref, w_ref) in enumerate(((oa_ref, wa_ref), (yb_ref, wb_ref), (oc_ref, wc_ref))):
            cols = slice(n * d, (n + 1) * d)
            gate = _sigmoid(zg_ref[rows, cols].astype(F32) + bg_ref[:, cols])
            term = gate * _dot(br_ref[rows, :], w_ref[...])
            merged = term if merged is None else merged + term
        y = _dot(merged.astype(BF16), wo_ref[...])
        o_ref[rows, :] = x_ref[rows, :] + _rms(y, g_ref[...])


def mix_out(o_a, y_b, o_c, z, b_gate, w_a, w_b, w_c, w_out, g_post, x, tm, name):
    m, d = x.shape
    e = o_a.shape[1]
    act = pl.BlockSpec((tm, e), lambda i: (i, 0))
    assert Z_G % (3 * d) == 0
    const = lambda a: pl.BlockSpec(a.shape, lambda i: (0, 0))
    return pl.pallas_call(
        _mix_out_kernel,
        grid=(m // tm,),
        in_specs=[act, act, act, pl.BlockSpec((tm, 3 * d), lambda i: (i, Z_G // (3 * d))),
                  pl.BlockSpec((1, 3 * d), lambda i: (0, 0)),
                  const(w_a), const(w_b), const(w_c), const(w_out),
                  pl.BlockSpec((1, d), lambda i: (0, 0)),
                  pl.BlockSpec((tm, d), lambda i: (i, 0))],
        out_specs=pl.BlockSpec((tm, d), lambda i: (i, 0)),
        out_shape=jax.ShapeDtypeStruct((m, d), F32),
        compiler_params=_cparams("parallel"),
        name=name,
    )(o_a, y_b, o_c, z, b_gate.reshape(1, -1), w_a, w_b, w_c, w_out,
      g_post.reshape(1, -1), x)


def _mem_attn_kernel(x_ref, mk_ref, mv_ref, gpre_ref, wq_ref, wo_ref, gpost_ref, o_ref, *, sub):
    mk = mk_ref[0].astype(BF16)
    mv = mv_ref[0].astype(BF16)
    for r in range(x_ref.shape[1] // sub):
        rows = slice(r * sub, (r + 1) * sub)
        x = x_ref[0, rows, :]
        u = _rms(x, gpre_ref[...]).astype(BF16)
        q = _dot(u, wq_ref[...]).astype(BF16)
        outs = []
        for h in range(MEM_HEADS):
            sl = slice(h * MEM_HD, (h + 1) * MEM_HD)
            s = _dot_nt(q[:, sl], mk[:, sl]) * (MEM_HD ** -0.5 * LOG2E)
            m = jnp.max(s, axis=-1, keepdims=True)
            p = jnp.exp2(s - m)
            l = jnp.sum(p, axis=-1, keepdims=True)
            outs.append(_dot(p.astype(BF16), mv[:, sl]) / l)
        o = jnp.concatenate(outs, axis=-1).astype(BF16)
        o_ref[0, rows, :] = x + _rms(_dot(o, wo_ref[...]), gpost_ref[...])


def mem_attention(x, mk, mv, g_pre, w_q, w_o, g_post, tm, name):
    b, s, d = x.shape
    const = lambda a: pl.BlockSpec(a.shape, lambda b_, i: (0, 0))
    vec = pl.BlockSpec((1, d), lambda b_, i: (0, 0))
    mem = pl.BlockSpec((1,) + mk.shape[1:], lambda b_, i: (b_, 0, 0))
    return pl.pallas_call(
        functools.partial(_mem_attn_kernel, sub=min(tm, 512)),
        grid=(b, s // tm),
        in_specs=[pl.BlockSpec((1, tm, d), lambda b_, i: (b_, i, 0)), mem, mem,
                  vec, const(w_q), const(w_o), vec],
        out_specs=pl.BlockSpec((1, tm, d), lambda b_, i: (b_, i, 0)),
        out_shape=jax.ShapeDtypeStruct((b, s, d), F32),
        compiler_params=_cparams("parallel", "parallel"),
        name=name,
    )(x, mk, mv, g_pre.reshape(1, -1), w_q, w_o, g_post.reshape(1, -1))


def _mlp_kernel(x_ref, gpre_ref, wu_ref, wd_ref, gpost_ref, o_ref, xn_ref, acc_ref):
    j = pl.program_id(1)
    last = pl.num_programs(1) - 1

    def partial_out(xn):
        h = jnp.square(jnp.maximum(_dot(xn, wu_ref[...]), 0.0))
        return _dot(h.astype(BF16), wd_ref[...])

    @pl.when(j == 0)
    def _():
        xn = _rms(x_ref[...], gpre_ref[...]).astype(BF16)
        xn_ref[...] = xn
        acc_ref[...] = partial_out(xn)

    @pl.when(jnp.logical_and(j > 0, j < last))
    def _():
        acc_ref[...] += partial_out(xn_ref[...])

    @pl.when(j == last)
    def _():
        y = acc_ref[...] + partial_out(xn_ref[...])
        o_ref[...] = x_ref[...] + _rms(y, gpost_ref[...])


def mlp(x, g_pre, w_up, w_down, g_post, tm, tf, name):
    m, d = x.shape
    f = w_up.shape[1]
    vec = pl.BlockSpec((1, d), lambda i, j: (0, 0))
    return pl.pallas_call(
        _mlp_kernel,
        grid=(m // tm, f // tf),
        in_specs=[pl.BlockSpec((tm, d), lambda i, j: (i, 0)), vec,
                  pl.BlockSpec((d, tf), lambda i, j: (0, j)),
                  pl.BlockSpec((tf, d), lambda i, j: (j, 0)), vec],
        out_specs=pl.BlockSpec((tm, d), lambda i, j: (i, 0)),
        out_shape=jax.ShapeDtypeStruct((m, d), F32),
        scratch_shapes=[pltpu.VMEM((tm, d), BF16), pltpu.VMEM((tm, d), F32)],
        compiler_params=_cparams("parallel", "arbitrary"),
        name=name,
    )(x, g_pre.reshape(1, -1), w_up, w_down, g_post.reshape(1, -1))


def _rope_angles(pos, half):
    inv = ROPE_THETA ** (-jnp.arange(half, dtype=F32) / half)
    ang = pos.astype(F32)[:, None] * inv[None, :]
    return jnp.cos(ang), jnp.sin(ang)


def _mla_rope_tables(pos):
    cos, sin = _rope_angles(pos, MLA_ROPE // 2)
    t = pos.shape[0]
    one = jnp.ones((t, MLA_NOPE), F32)
    zero64 = jnp.zeros((t, MLA_NOPE), F32)
    pad = jnp.zeros((t, HEAD_SLAB - MLA_NOPE - MLA_ROPE), F32)
    return (jnp.concatenate([one, cos, cos, pad], axis=1),
            jnp.concatenate([zero64, -sin, sin, pad], axis=1))


def _ret_tables(pos, blk, n_real):
    cos, sin = _rope_angles(pos, RET_DK // 2)
    cos_q = jnp.tile(jnp.concatenate([cos, cos], axis=1), (1, RET_HEADS))
    sin_q = jnp.tile(jnp.concatenate([-sin, sin], axis=1), (1, RET_HEADS))
    k_scale = RET_DK ** -0.5
    log_g = jnp.log1p(-jnp.exp2(-5.0 - jnp.arange(RET_HEADS, dtype=F32)))
    idx = jnp.arange(blk, dtype=F32)
    diff = idx[:, None] - idx[None, :]
    dmat = jnp.where(diff >= 0, jnp.exp(log_g[:, None, None] * jnp.maximum(diff, 0.0)), 0.0)
    rowdec = jnp.exp(log_g[None, :] * (idx[:, None] + 1.0))
    w = jnp.where(idx[:, None] < n_real,
                  jnp.exp(log_g[None, :] * jnp.maximum(n_real - 1.0 - idx[:, None], 0.0)), 0.0)
    sdec = jnp.exp(log_g * n_real)
    n_pair = RET_HEADS // 2
    return dict(
        cos_q=cos_q, sin_q=sin_q, cos_k=cos_q * k_scale, sin_k=sin_q * k_scale,
        dmat=dmat,
        rowdec=jnp.repeat(rowdec, RET_DV, axis=1),
        kw=jnp.repeat(w, RET_DK, axis=1),
        sdec=jnp.broadcast_to(jnp.repeat(sdec, RET_DK).reshape(n_pair, LANE, 1),
                              (n_pair, LANE, RET_DV)),
    )


def _band_bias(table, tq, span, q_off, allowed):
    sub = 8
    assert span % sub == 0
    length = tq + span - 1
    d = np.arange(length)
    ext = table[:, np.clip(d + q_off - span + 1, -MAX_REL, MAX_REL) + MAX_REL].astype(F32) * LOG2E
    shifted = jnp.stack([ext[:, sub - 1 - r:length - r] for r in range(sub)], axis=1)
    tile = jnp.concatenate([shifted[:, :, span - sub * (a + 1):span - sub * (a + 1) + tq]
                            for a in range(span // sub)], axis=1)
    return jnp.where(allowed[None], tile, NEG_INF)


def _layer_weights(l, w_in, w_mla_uq, w_mla_ukv, w_br_a, w_br_b, w_br_c, w_out, w_mem_q, w_mem_k,
                   w_mem_v, w_mem_o, w_up, w_down):
    d = w_in.shape[1]
    parts, start = [], 0
    for n in (MLA_Q_RANK, MLA_KV_RANK, MLA_ROPE, 256, 256, 512, 512, 512, 512, 512, 3 * d):
        parts.append(w_in[l, :, start:start + n])
        start += n
    zq, zkv, zpe, rq, rk, rv, rg, cq, ck, cv, zg = parts
    cq = cq * (BAND_HD ** -0.5 * LOG2E)
    zeros = lambda n: jnp.zeros((d, n), w_in.dtype)
    w_in_l = jnp.concatenate(
        [zg, cq, ck, cv, rv, rg, rq, rk, zkv, zeros(MLA_NOPE), zpe,
         zeros(HEAD_SLAB - MLA_NOPE - MLA_ROPE), zq], axis=1).astype(BF16)
    assert w_in_l.shape[1] == Z_WIDTH
    pad_head = lambda w: jnp.pad(w, ((0, 0), (0, 0), (0, HEAD_SLAB - w.shape[-1])))
    flat = lambda w: w.reshape(w.shape[0], -1).astype(BF16)
    return dict(
        w_in=w_in_l,
        w_uq=jnp.concatenate([flat(pad_head(w_mla_uq[l])), flat(pad_head(jnp.concatenate(
            [jnp.zeros_like(w_mla_uq[l][..., :MLA_NOPE]),
             w_mla_uq[l][..., MLA_NOPE + MLA_ROPE // 2:],
             w_mla_uq[l][..., MLA_NOPE:MLA_NOPE + MLA_ROPE // 2]], axis=-1)))], axis=1),
        w_uk=flat(pad_head(w_mla_ukv[l][..., :MLA_NOPE])),
        w_uv=flat(pad_head(w_mla_ukv[l][..., MLA_NOPE:])),
        w_a=w_br_a[l].astype(BF16), w_b=w_br_b[l].astype(BF16), w_c=w_br_c[l].astype(BF16),
        w_out=w_out[l].astype(BF16),
        w_mq=flat(w_mem_q[l]),
        w_mkv=jnp.concatenate([flat(w_mem_k[l]), flat(w_mem_v[l])], axis=1),
        w_mo=w_mem_o[l].reshape(-1, d).astype(BF16),
        w_up=w_up[l].astype(BF16), w_down=w_down[l].astype(BF16),
    )


def _tile(n, pref):
    t = min(n, pref)
    while n % t:
        t -= LANE
    return t


def _trunk_layer(x, w, P, l, tabs, mem_k, mem_v, past, tag):
    b, s, d = x.shape
    m = b * s
    x2 = x.reshape(m, d)
    tm = _tile(m, 1024)
    tm2 = _tile(m, 512)
    z = norm_matmul(x2, P["g_pre_mix"][l], w["w_in"], BF16, tm, 2304, f"in_proj_{tag}")
    q, ckv, kpe_out, k, v = mla_prep(z, P["g_mla_q"][l], P["g_mla_kv"][l], w["w_uq"], w["w_uk"],
                                     w["w_uv"], tabs["mla_cos"], tabs["mla_sin"], tm2,
                                     f"mla_prep_{tag}")
    z3 = z.reshape(b, s, Z_WIDTH)
    q3 = q.reshape(b, s, -1)
    if past is None:
        o_a = mla_attention(q3, k.reshape(b, s, -1), v.reshape(b, s, -1), _tile(s, 512),
                            f"mla_attn_{tag}")
        init = jnp.zeros((b, RET_HEADS // 2, LANE, RET_DV), F32)
        y_b, state = retention(z3, tabs["ret"], P["g_ret_gn"][l], init, tabs["ret_blk"],
                               f"retention_{tag}")
        o_c = band_attention(z3, Z_CQ // LANE, z3, Z_CK // LANE, z3, Z_CV // LANE,
                             tabs["band_bias"][l], tabs["band_tq"], True, f"band_{tag}")
        n_real = s
        band_k = z3[:, s - BAND_WINDOW:, Z_CK:Z_CK + 512]
        band_v = z3[:, s - BAND_WINDOW:, Z_CV:Z_CV + 512]
    else:
        c_ckv, c_kpe, s_ret, c_bk, c_bv = past
        n_real = CHUNK
        n_past = c_ckv.shape[1]
        kpe_pad = jnp.pad(c_kpe, ((0, 0), (0, 0), (MLA_NOPE, HEAD_SLAB - MLA_NOPE - MLA_ROPE)))
        k_c, v_c = mla_kv_up(c_ckv.reshape(b * n_past, -1), kpe_pad.reshape(b * n_past, -1),
                             w["w_uk"], w["w_uv"], _tile(b * n_past, 1024), f"mla_kv_cache_{tag}")
        per_b = lambda a: a.reshape(b, -1, a.shape[-1])
        o_a = mla_attention_seg(q3, per_b(k_c), per_b(v_c), per_b(k), per_b(v), n_real,
                                f"mla_attn_{tag}")
        init = s_ret.astype(F32).reshape(b, RET_HEADS // 2, LANE, RET_DV)
        y_b, state = retention(z3, tabs["ret"], P["g_ret_gn"][l], init, s, f"retention_{tag}")
        band_k = z3[:, :n_real, Z_CK:Z_CK + 512]
        band_v = z3[:, :n_real, Z_CV:Z_CV + 512]
        span = tabs["band_bias"][l].shape[-2]
        w_band = c_bk.shape[1]
        catb = lambda c, n: jnp.pad(
            jnp.concatenate([c.reshape(b, w_band, -1).astype(BF16), n], axis=1),
            ((0, 0), (0, span - w_band - n_real), (0, 0)))
        o_c = band_attention(z3, Z_CQ // LANE, catb(c_bk, band_k), 0, catb(c_bv, band_v), 0,
                             tabs["band_bias"][l], s, False, f"band_{tag}")
    x2 = mix_out(o_a.reshape(m, -1), y_b.reshape(m, -1), o_c.reshape(m, -1), z, P["b_gate"][l],
                 w["w_a"], w["w_b"], w["w_c"], w["w_out"], P["g_post_mix"][l], x2, tm,
                 f"mix_out_{tag}")
    x3 = mem_attention(x2.reshape(b, s, d), mem_k, mem_v, P["g_pre_mem"][l], w["w_mq"], w["w_mo"],
                       P["g_post_mem"][l], _tile(s, 1024), f"mem_attn_{tag}")
    x4 = mlp(x3.reshape(m, d), P["g_pre_ff"][l], w["w_up"], w["w_down"], P["g_post_ff"][l],
             tm, 1024, f"mlp_{tag}")
    new = (ckv.reshape(b, s, -1)[:, :n_real],
           kpe_out.reshape(b, s, -1)[:, :n_real],
           state.reshape(b, RET_HEADS, RET_DK, RET_DV),
           band_k.astype(F32).reshape(b, -1, BAND_HEADS, BAND_HD),
           band_v.astype(F32).reshape(b, -1, BAND_HEADS, BAND_HD))
    return x4.reshape(b, s, d), new


def kernel(x_prompt, x_sample, cache_mla_ckv, cache_mla_kpe, state_ret, cache_band_k, cache_band_v, cache_mem_k, cache_mem_v, mem_prompt, g_pre_mix, w_in, g_mla_q, w_mla_uq, g_mla_kv, w_mla_ukv, g_ret_gn, band_rel_bias, w_br_a, w_br_b, w_br_c, b_gate, w_out, g_post_mix, g_pre_mem, g_mem, w_mem_q, w_mem_k, w_mem_v, w_mem_o, g_post_mem, g_pre_ff, w_up, w_down, g_post_ff):
    P = dict(g_pre_mix=g_pre_mix, g_mla_q=g_mla_q, g_mla_kv=g_mla_kv, g_ret_gn=g_ret_gn,
             b_gate=b_gate, g_post_mix=g_post_mix, g_pre_mem=g_pre_mem, g_post_mem=g_post_mem,
             g_pre_ff=g_pre_ff, g_post_ff=g_post_ff)
    depth = w_in.shape[0]
    bp, sp, d = x_prompt.shape
    bs, ss, _ = x_sample.shape
    n_past = cache_mla_ckv.shape[2]
    w_band = cache_band_k.shape[2]
    assert ss == CHUNK and sp % 512 == 0 and sp >= BAND_WINDOW
    s_pad = 2 * CHUNK

    pos_p = jnp.arange(sp)
    pos_s = n_past + jnp.arange(s_pad)
    ret_blk = 256
    band_tq = 256
    cos_p, sin_p = _mla_rope_tables(pos_p)
    cos_s, sin_s = _mla_rope_tables(pos_s)

    span_p = 3 * band_tq
    jj = np.arange(span_p)[:, None]
    ii = np.arange(band_tq)[None, :]
    band_ok = (jj // CHUNK >= ii // CHUNK) & (jj // CHUNK <= ii // CHUNK + BAND_PREV_CHUNKS)
    bias_p = [_band_bias(band_rel_bias[l], band_tq, span_p, 2 * band_tq, band_ok)
              for l in range(depth)]
    span_s = -(-(w_band + CHUNK) // LANE) * LANE
    mask_s = np.broadcast_to(np.arange(span_s)[:, None] < w_band + CHUNK, (span_s, s_pad))
    bias_s = [_band_bias(band_rel_bias[l], s_pad, span_s, w_band, mask_s) for l in range(depth)]

    tabs_p = dict(mla_cos=cos_p, mla_sin=sin_p, ret=_ret_tables(pos_p, ret_blk, ret_blk),
                  ret_blk=ret_blk, band_bias=bias_p, band_tq=band_tq)
    tabs_s = dict(mla_cos=jnp.tile(cos_s, (bs, 1)), mla_sin=jnp.tile(sin_s, (bs, 1)),
                  ret=_ret_tables(pos_s, s_pad, CHUNK), band_bias=bias_s)

    xp = x_prompt
    xs = jnp.pad(x_sample, ((0, 0), (0, s_pad - ss), (0, 0)))
    mem2 = mem_prompt.reshape(-1, d)
    new_p = [[] for _ in range(7)]
    new_s = [[] for _ in range(5)]
    for l in range(depth):
        w = _layer_weights(l, w_in, w_mla_uq, w_mla_ukv, w_br_a, w_br_b, w_br_c, w_out, w_mem_q,
                           w_mem_k, w_mem_v, w_mem_o, w_up, w_down)
        mkv = norm_matmul(mem2, g_mem[l], w["w_mkv"], F32, _tile(mem2.shape[0], 1024), 512,
                          f"mem_kv_{l}")
        e = MEM_HEADS * MEM_HD
        mk = mkv[:, :e].reshape(bp, -1, e)
        mv = mkv[:, e:].reshape(bp, -1, e)
        xp, st_p = _trunk_layer(xp, w, P, l, tabs_p, mk, mv, None, f"p{l}")
        xs, st_s = _trunk_layer(xs, w, P, l, tabs_s, cache_mem_k[l].reshape(bs, -1, e),
                                cache_mem_v[l].reshape(bs, -1, e),
                                (cache_mla_ckv[l], cache_mla_kpe[l], state_ret[l], cache_band_k[l],
                                 cache_band_v[l]), f"s{l}")
        mem_shape = (bp, -1, MEM_HEADS, MEM_HD)
        for acc, t in zip(new_p, st_p + (mk.reshape(mem_shape), mv.reshape(mem_shape))):
            acc.append(t)
        for acc, t in zip(new_s, st_s):
            acc.append(t)
    stack = lambda ts: jnp.stack(ts, axis=0)
    return (xp, xs[:, :ss],
            stack(new_p[0]), stack(new_p[1]), stack(new_p[2]), stack(new_p[3]), stack(new_p[4]),
            stack(new_p[5]), stack(new_p[6]),
            stack(new_s[0]), stack(new_s[1]), stack(new_s[2]), stack(new_s[3]), stack(new_s[4]))
```

```python
import functools

import numpy as np
import jax
import jax.numpy as jnp
from jax import lax
from jax.experimental import pallas as pl
from jax.experimental.pallas import tpu as pltpu

F32 = jnp.float32
BF16 = jnp.bfloat16

CHUNK = 64
MLA_HEADS = 8
MLA_Q_RANK = 384
MLA_KV_RANK = 256
MLA_NOPE = 64
MLA_ROPE = 32
MLA_V = 64
MLA_SCALE = (MLA_NOPE + MLA_ROPE) ** -0.5
RET_HEADS = 4
RET_DK = 64
RET_DV = 128
BAND_HEADS = 8
BAND_HD = 64
BAND_PREV_CHUNKS = 8
BAND_WINDOW = BAND_PREV_CHUNKS * CHUNK
MAX_REL = 128
MEM_HEADS = 4
MEM_HD = 128
ROPE_THETA = 10000.0
EPS = 1e-6
NEG_INF = -1e30
LOG2E = 1.4426950408889634

LANE = 128
HEAD_SLAB = 128
VMEM_LIMIT = 48 * 1024 * 1024

Z_G = 0
Z_CQ = 3072
Z_CK = 3584
Z_CV = 4096
Z_RV = 4608
Z_RG = 5120
Z_RQ = 5632
Z_RK = 5888
Z_KV = 6144
Z_PE = 6400
Z_Q = 6528
Z_WIDTH = 6912


def _cparams(*sem, **kw):
    return pltpu.CompilerParams(dimension_semantics=sem, vmem_limit_bytes=VMEM_LIMIT, **kw)


def _rms(x, g):
    return x * lax.rsqrt(jnp.mean(x * x, axis=-1, keepdims=True) + EPS) * g


def _dot(a, b):
    return jnp.dot(a, b, preferred_element_type=F32)


def _dot_nt(a, b):
    return lax.dot_general(a, b, (((1,), (1,)), ((), ())), preferred_element_type=F32)


def _dot_tn(a, b):
    return lax.dot_general(a, b, (((0,), (0,)), ((), ())), preferred_element_type=F32)


def _sigmoid(x):
    return 0.5 * jnp.tanh(0.5 * x) + 0.5


def _lane_iota(shape):
    return lax.broadcasted_iota(jnp.int32, shape, len(shape) - 1)


def _norm_matmul_kernel(x_ref, g_ref, w_ref, o_ref, xn_ref):
    @pl.when(pl.program_id(1) == 0)
    def _():
        xn = _rms(x_ref[...].astype(F32), g_ref[...]).astype(BF16)
        xn_ref[...] = xn
        o_ref[...] = _dot(xn, w_ref[...]).astype(o_ref.dtype)

    @pl.when(pl.program_id(1) > 0)
    def _():
        o_ref[...] = _dot(xn_ref[...], w_ref[...]).astype(o_ref.dtype)


def norm_matmul(x, g, w, out_dtype, tm, tn, name):
    m, k = x.shape
    n = w.shape[1]
    assert m % tm == 0 and n % tn == 0, (m, tm, n, tn)
    return pl.pallas_call(
        _norm_matmul_kernel,
        grid=(m // tm, n // tn),
        in_specs=[
            pl.BlockSpec((tm, k), lambda i, j: (i, 0)),
            pl.BlockSpec((1, k), lambda i, j: (0, 0)),
            pl.BlockSpec((k, tn), lambda i, j: (0, j)),
        ],
        out_specs=pl.BlockSpec((tm, tn), lambda i, j: (i, j)),
        out_shape=jax.ShapeDtypeStruct((m, n), out_dtype),
        scratch_shapes=[pltpu.VMEM((tm, k), BF16)],
        compiler_params=_cparams("parallel", "arbitrary"),
        name=name,
    )(x, g.reshape(1, k), w)


def _rope_slab(x, cos, sin, rot):
    first_end, half, period = rot
    width = x.shape[1]
    right = pltpu.roll(x, width - half, 1)
    left = pltpu.roll(x, half, 1)
    partner = jnp.where((_lane_iota(x.shape) & (period - 1)) < first_end, right, left)
    return x * cos + partner * sin


_MLA_ROT = (MLA_NOPE + MLA_ROPE // 2, MLA_ROPE // 2, LANE)
_RET_ROT = (RET_DK // 2, RET_DK // 2, RET_DK)


_Z_MLA = Z_KV
_Z_MLA_W = Z_WIDTH - Z_KV
assert (Z_KV, Z_PE, Z_Q) == (_Z_MLA, _Z_MLA + MLA_KV_RANK, _Z_MLA + MLA_KV_RANK + HEAD_SLAB)
assert _Z_MLA % _Z_MLA_W == 0


def _mla_kv_heads(ckv, kpe, wk_ref, wv_ref, k_ref, v_ref):
    c = ckv.astype(BF16)
    kn = _dot(c, wk_ref[...])
    for h in range(MLA_HEADS):
        sl = slice(h * HEAD_SLAB, (h + 1) * HEAD_SLAB)
        k_ref[:, sl] = (kn[:, sl] + kpe).astype(BF16)
    v = _dot(c, wv_ref[...])
    ones_lane = (_lane_iota(v.shape) & (HEAD_SLAB - 1)) == MLA_V
    v_ref[...] = jnp.where(ones_lane, 1.0, v).astype(BF16)


def _mla_prep_kernel(z_ref, gq_ref, gkv_ref, wq_ref, wk_ref, wv_ref, cos_ref, sin_ref, *refs):
    q_ref, ckv_ref, kpe_out_ref, k_ref, v_ref = refs[-5:]
    cos = cos_ref[...]
    sin = sin_ref[...]
    zkv = z_ref[:, :MLA_KV_RANK]
    zpe = z_ref[:, MLA_KV_RANK:MLA_KV_RANK + HEAD_SLAB]
    zq = z_ref[:, MLA_KV_RANK + HEAD_SLAB:]
    qn = _rms(zq.astype(F32), gq_ref[...]).astype(BF16)
    q = _dot(qn, wq_ref[...])
    cos_q = cos * (MLA_SCALE * LOG2E)
    sin_q = sin * (MLA_SCALE * LOG2E)
    for h in range(MLA_HEADS):
        sl = slice(h * HEAD_SLAB, (h + 1) * HEAD_SLAB)
        pt = slice((MLA_HEADS + h) * HEAD_SLAB, (MLA_HEADS + h + 1) * HEAD_SLAB)
        q_ref[:, sl] = (q[:, sl] * cos_q + q[:, pt] * sin_q).astype(BF16)
    ckv = _rms(zkv.astype(F32), gkv_ref[...])
    ckv_ref[0] = ckv
    for later in range(1, ckv_ref.shape[0]):
        ckv_ref[later] = jnp.zeros_like(ckv)
    kpe = _rope_slab(zpe.astype(F32), cos, sin, _MLA_ROT)
    kpe_out_ref[...] = kpe[:, MLA_NOPE:MLA_NOPE + MLA_ROPE]
    _mla_kv_heads(ckv, kpe, wk_ref, wv_ref, k_ref, v_ref)


def mla_prep(z, g_q, g_kv, w_uq, w_uk, w_uv, cos, sin, tm, layer, depth, ckv_stack, name):
    m = z.shape[0]
    nt = cos.shape[0] // tm
    row = lambda w: pl.BlockSpec((1, w), lambda i: (0, 0))
    const = lambda a: pl.BlockSpec(a.shape, lambda i: (0, 0))
    slabs = MLA_HEADS * HEAD_SLAB
    out = lambda w: pl.BlockSpec((tm, w), lambda i: (i, 0))
    in_specs = [
        pl.BlockSpec((tm, _Z_MLA_W), lambda i: (i, _Z_MLA // _Z_MLA_W)),
        row(MLA_Q_RANK), row(MLA_KV_RANK), const(w_uq), const(w_uk), const(w_uv),
        pl.BlockSpec((tm, HEAD_SLAB), lambda i: (i % nt, 0)),
        pl.BlockSpec((tm, HEAD_SLAB), lambda i: (i % nt, 0)),
    ]
    operands = [z, g_q.reshape(1, -1), g_kv.reshape(1, -1), w_uq, w_uk, w_uv, cos, sin]
    if ckv_stack is None:
        assert layer == 0
        ckv_spec = pl.BlockSpec((depth, tm, MLA_KV_RANK), lambda i: (0, i, 0))
        aliases = {}
    else:
        ckv_spec = pl.BlockSpec((1, tm, MLA_KV_RANK), lambda i: (layer, i, 0))
        aliases = {len(operands): 1}
        in_specs.append(pl.BlockSpec(memory_space=pl.ANY))
        operands.append(ckv_stack)
    return pl.pallas_call(
        _mla_prep_kernel,
        grid=(m // tm,),
        in_specs=in_specs,
        out_specs=[out(slabs), ckv_spec, out(MLA_ROPE), out(slabs), out(slabs)],
        out_shape=[
            jax.ShapeDtypeStruct((m, slabs), BF16),
            jax.ShapeDtypeStruct((depth, m, MLA_KV_RANK), F32),
            jax.ShapeDtypeStruct((m, MLA_ROPE), F32),
            jax.ShapeDtypeStruct((m, slabs), BF16),
            jax.ShapeDtypeStruct((m, slabs), BF16),
        ],
        input_output_aliases=aliases,
        compiler_params=_cparams("parallel"),
        name=name,
    )(*operands)


def _head_of_pair(x, hh):
    lane = _lane_iota(x.shape)
    keep = (lane < BAND_HD) if hh == 0 else (lane >= BAND_HD)
    return jnp.where(keep, x, jnp.zeros_like(x))


MLA_GROUP = 4
BAND_GROUP = 4
_PAIR_SLABS = [slice(hh * HEAD_SLAB, (hh + 1) * HEAD_SLAB) for hh in range(MLA_GROUP)]


def _mla_out(acc):
    return acc[:MLA_V, :] / acc[MLA_V:MLA_V + 1, :]


def _mla_attn_kernel(q_ref, k_ref, v_ref, o_ref, acc_ref, sa_ref, sb_ref, ma_ref, mb_ref, *, t, nq):
    sa_ref, sb_ref = (sa_ref, ma_ref), (sb_ref, mb_ref)
    shift = CHUNK.bit_length() - 1
    key_chunk = jnp.right_shift(lax.broadcasted_iota(jnp.int32, (t, t), 0), shift)
    qry_chunk = jnp.right_shift(lax.broadcasted_iota(jnp.int32, (t, t), 1), shift)
    diag = key_chunk <= qry_chunk

    def rows_of(kb):
        return pl.ds(kb * t if isinstance(kb, int) else pl.multiple_of(kb * t, t), t)

    def scores(qt, kb, buf):
        for hh, sl in enumerate(_PAIR_SLABS):
            s = _dot_nt(k_ref[0, rows_of(kb), sl], q_ref[0, rows_of(qt), sl])
            buf[0][hh, :, :t] = s
            buf[1][hh] = jnp.max(s, axis=0, keepdims=True)

    def tile(qi, carry):
        _mla_attn_tile(qi, nq, scores, k_ref, v_ref, o_ref, acc_ref, sa_ref, sb_ref, diag, rows_of,
                       t)
        return carry

    acc_ref[...] = jnp.zeros_like(acc_ref)
    scores(0, 0, sa_ref)
    lax.fori_loop(0, nq, tile, 0)


def _mla_attn_tile(qi, nq, scores_of, k_ref, v_ref, o_ref, acc_ref, sa_ref, sb_ref, diag, rows_of,
                   t):
    scores = functools.partial(scores_of, qi)
    next_first = lambda: scores_of(jnp.minimum(qi + 1, nq - 1), 0, sa_ref)

    def consume(kb, buf, ms, mask):
        new_m = []
        for hh, sl in enumerate(_PAIR_SLABS):
            s = buf[0][hh, :, :t]
            if mask is None:
                blk_max = buf[1][hh]
            else:
                s = jnp.where(mask, s, NEG_INF)
                blk_max = jnp.max(s, axis=0, keepdims=True)
            m = jnp.maximum(ms[hh], blk_max)
            alpha = jnp.exp2(ms[hh] - m)
            p = jnp.exp2(s - m).astype(BF16)
            acc_ref[hh] = alpha * acc_ref[hh] + _dot_tn(v_ref[0, rows_of(kb), sl], p)
            new_m.append(m)
        return tuple(new_m)

    def finish():
        outs = [_mla_out(acc_ref[hh]) for hh in range(MLA_GROUP)]
        o_ref[0, rows_of(qi), :] = jnp.concatenate(outs, axis=0).T.astype(o_ref.dtype)

    ms = tuple(jnp.full((1, t), NEG_INF, F32) for _ in range(MLA_GROUP))

    def pair(j, ms):
        kb = 2 * j
        scores(kb + 1, sb_ref)
        ms = consume(kb, sa_ref, ms, None)
        scores(kb + 2, sa_ref)
        return consume(kb + 1, sb_ref, ms, None)

    ms = lax.fori_loop(0, qi // 2, pair, ms)

    @pl.when(qi % 2 == 0)
    def _():
        consume(qi, sa_ref, ms, diag)
        next_first()
        finish()

    @pl.when(qi % 2 == 1)
    def _():
        scores(qi, sb_ref)
        ms1 = consume(qi - 1, sa_ref, ms, None)
        next_first()
        consume(qi, sb_ref, ms1, diag)
        finish()


def _mla_attn_seg_kernel(q_ref, ckv1_ref, kpe1_ref, wk_ref, wv_ref, k2_ref, v2_ref, o_ref,
                         k1_ref, v1_ref, *, n2_valid, chunk):
    for r in range(ckv1_ref.shape[1] // chunk):
        rows = pl.ds(r * chunk, chunk)
        _mla_kv_heads(ckv1_ref[0, rows, :], kpe1_ref[0, rows, :], wk_ref, wv_ref,
                      k1_ref.at[rows], v1_ref.at[rows])
    for pair in range(MLA_HEADS // 2):
        outs = []
        for hh in range(2):
            h = 2 * pair + hh
            sl = slice(h * HEAD_SLAB, (h + 1) * HEAD_SLAB)
            q = q_ref[0, :, sl]
            s1 = _dot_nt(k1_ref[:, sl], q)
            s2 = _dot_nt(k2_ref[0, :, sl], q)
            s2 = jnp.where(lax.broadcasted_iota(jnp.int32, s2.shape, 0) < n2_valid, s2, NEG_INF)
            m = jnp.maximum(jnp.max(s1, axis=0, keepdims=True), jnp.max(s2, axis=0, keepdims=True))
            acc = (_dot_tn(v1_ref[:, sl], jnp.exp2(s1 - m).astype(BF16))
                   + _dot_tn(v2_ref[0, :, sl], jnp.exp2(s2 - m).astype(BF16)))
            outs.append(_mla_out(acc))
        o_ref[0, :, pair * LANE:(pair + 1) * LANE] = (
            jnp.concatenate(outs, axis=0).T.astype(o_ref.dtype))


def mla_attention_seg(q, ckv1, kpe1, w_uk, w_uv, k2, v2, n2_valid, name):
    b, sq, _ = q.shape
    n1 = ckv1.shape[1]
    slabs = MLA_HEADS * HEAD_SLAB
    blk = lambda a: pl.BlockSpec((1,) + a.shape[1:], lambda b_: (b_, 0, 0))
    const = lambda a: pl.BlockSpec(a.shape, lambda b_: (0, 0))
    return pl.pallas_call(
        functools.partial(_mla_attn_seg_kernel, n2_valid=n2_valid, chunk=_tile(n1, 512)),
        grid=(b,),
        in_specs=[blk(q), blk(ckv1), blk(kpe1), const(w_uk), const(w_uv), blk(k2), blk(v2)],
        out_specs=pl.BlockSpec((1, sq, MLA_HEADS * MLA_V), lambda b_: (b_, 0, 0)),
        out_shape=jax.ShapeDtypeStruct((b, sq, MLA_HEADS * MLA_V), BF16),
        scratch_shapes=[pltpu.VMEM((n1, slabs), BF16), pltpu.VMEM((n1, slabs), BF16)],
        compiler_params=_cparams("parallel"),
        name=name,
    )(q, ckv1, kpe1, w_uk, w_uv, k2, v2)


def mla_attention(q, k, v, t, name):
    b, s, _ = q.shape
    assert s % t == 0 and k.shape[1] == s
    g = MLA_GROUP
    whole = pl.BlockSpec((1, s, g * HEAD_SLAB), lambda b_, h: (b_, 0, h))
    return pl.pallas_call(
        functools.partial(_mla_attn_kernel, t=t, nq=s // t),
        grid=(b, MLA_HEADS // g),
        in_specs=[whole, whole, whole],
        out_specs=pl.BlockSpec((1, s, g * MLA_V), lambda b_, h: (b_, 0, h)),
        out_shape=jax.ShapeDtypeStruct((b, s, MLA_HEADS * MLA_V), BF16),
        scratch_shapes=[pltpu.VMEM((g, HEAD_SLAB, t), F32), pltpu.VMEM((g, t, t + LANE), F32),
                        pltpu.VMEM((g, t, t + LANE), F32), pltpu.VMEM((g, 1, t), F32),
                        pltpu.VMEM((g, 1, t), F32)],
        compiler_params=_cparams("parallel", "parallel"),
        name=name,
    )(q, k, v)


def _band_attn_kernel(q_ref, k_ref, v_ref, bias_ref, o_ref, sa_ref, sb_ref, ma_ref, mb_ref, *,
                      tq, nq, span, sliding):
    def geom(i):
        if not sliding:
            return 0, span, 0
        if isinstance(i, int):
            n = min(i + 1, span // tq) * tq
            return max(i + 1 - span // tq, 0) * tq, n, span - n
        return pl.multiple_of((i + 1 - span // tq) * tq, tq), span, 0

    def q_rows(i):
        return pl.ds(i * tq if isinstance(i, int) else pl.multiple_of(i * tq, tq), tq)

    sa_ref, sb_ref = (sa_ref, ma_ref), (sb_ref, mb_ref)

    n_pairs = q_ref.shape[-1] // LANE
    pair_lanes = [slice(p * LANE, (p + 1) * LANE) for p in range(n_pairs)]

    def scores(i, buf):
        start, n, boff = geom(i)
        for pr, lanes in enumerate(pair_lanes):
            q = q_ref[0, q_rows(i), lanes]
            k = k_ref[0, pl.ds(start, n), lanes]
            for hh in range(2):
                h = 2 * pr + hh
                s = _dot_nt(k, _head_of_pair(q, hh)) + bias_ref[0, h, boff:boff + n, :]
                buf[0][h, :n, :tq] = s
                buf[1][h] = jnp.max(s, axis=0, keepdims=True)

    def consume(i, buf):
        start, n, _ = geom(i)
        tiles = []
        for pr, lanes in enumerate(pair_lanes):
            v = v_ref[0, pl.ds(start, n), lanes]
            outs = []
            for hh in range(2):
                h = 2 * pr + hh
                p = jnp.exp2(buf[0][h, :n, :tq] - buf[1][h])
                l = jnp.sum(p, axis=0, keepdims=True)
                outs.append(_dot_tn(v, p.astype(BF16)) / l)
            first = lax.broadcasted_iota(jnp.int32, outs[0].shape, 0) < BAND_HD
            tiles.append(jnp.where(first, outs[0], outs[1]))
        o_ref[0, q_rows(i), :] = jnp.concatenate(tiles, axis=0).T.astype(o_ref.dtype)

    scores(0, sa_ref)
    if nq == 1:
        consume(0, sa_ref)
        return
    scores(1, sb_ref)
    consume(0, sa_ref)
    scores(2, sa_ref)
    consume(1, sb_ref)

    def pair(j, carry):
        i = 2 * j + 2
        scores(i + 1, sb_ref)
        consume(i, sa_ref)
        scores(i + 2, sa_ref)
        consume(i + 1, sb_ref)
        return carry

    lax.fori_loop(0, (nq - 2) // 2 - 1, pair, 0)
    scores(nq - 1, sb_ref)
    consume(nq - 2, sa_ref)
    consume(nq - 1, sb_ref)


def band_attention(q, q_col, k, k_col, v, v_col, bias, tq, sliding, name):
    b, sq = q.shape[:2]
    sk = k.shape[1]
    nq = sq // tq
    span = bias.shape[-2]
    assert (sliding and sk == sq and nq >= 4 and nq % 2 == 0) or (nq == 1 and sk == span)
    g = BAND_GROUP
    w = g * BAND_HD
    assert q_col * LANE % w == 0 and k_col * LANE % w == 0 and v_col * LANE % w == 0
    col = lambda c: (lambda hg, b_: (b_, 0, c * LANE // w + hg))
    return pl.pallas_call(
        functools.partial(_band_attn_kernel, tq=tq, nq=nq, span=span, sliding=sliding),
        grid=(BAND_HEADS // g, b),
        in_specs=[
            pl.BlockSpec((1, sq, w), col(q_col)),
            pl.BlockSpec((1, sk, w), col(k_col)),
            pl.BlockSpec((1, sk, w), col(v_col)),
            pl.BlockSpec((1, g, span, tq), lambda hg, b_: (0, hg, 0, 0)),
        ],
        out_specs=pl.BlockSpec((1, sq, w), col(0)),
        out_shape=jax.ShapeDtypeStruct((b, sq, BAND_HEADS * BAND_HD), BF16),
        scratch_shapes=[pltpu.VMEM((g, span, tq + LANE), F32),
                        pltpu.VMEM((g, span, tq + LANE), F32),
                        pltpu.VMEM((g, 1, tq), F32), pltpu.VMEM((g, 1, tq), F32)],
        compiler_params=_cparams("parallel", "arbitrary"),
        name=name,
    )(q, k, v, bias.reshape((1,) + bias.shape))


def _retention_kernel(z_ref, cq_ref, sq_ref, ck_ref, sk_ref,
                      dmat_ref, rowdec_ref, kw_ref, sdec_ref, gn_ref, init_ref,
                      y_ref, state_out_ref, state_ref):
    c = pl.program_id(1)

    @pl.when(c == 0)
    def _():
        state_ref[...] = init_ref[...]

    n_pair = RET_HEADS // 2
    gn = gn_ref[...]
    row_is_first = lax.broadcasted_iota(jnp.int32, (LANE, RET_DV), 0) < RET_DK
    v_w = RET_HEADS * RET_DV
    qk_w = RET_HEADS * RET_DK
    for bi in range(z_ref.shape[0]):
        rq = z_ref[bi, :, 2 * v_w:2 * v_w + qk_w]
        rk = z_ref[bi, :, 2 * v_w + qk_w:]
        q = _rope_slab(rq.astype(F32), cq_ref[...], sq_ref[...], _RET_ROT)
        k = _rope_slab(rk.astype(F32), ck_ref[...], sk_ref[...], _RET_ROT)
        qb = q.astype(BF16)
        kb = k.astype(BF16)
        kwb = (k * kw_ref[...]).astype(BF16)
        for p in range(n_pair):
            psl = slice(p * LANE, (p + 1) * LANE)
            st = state_ref[bi, p]
            stb = st.astype(BF16)
            kv = []
            for hh in range(2):
                h = 2 * p + hh
                vsl = slice(h * RET_DV, (h + 1) * RET_DV)
                v = z_ref[bi, :, vsl]
                qh = _head_of_pair(qb[:, psl], hh)
                s = _dot_nt(qh, kb[:, psl]) * dmat_ref[h]
                o = _dot(s.astype(BF16), v) + _dot(qh, stb) * rowdec_ref[:, vsl]
                mu = jnp.mean(o, axis=-1, keepdims=True)
                d = o - mu
                yn = d * lax.rsqrt(jnp.mean(d * d, axis=-1, keepdims=True) + EPS)
                g = z_ref[bi, :, v_w + h * RET_DV:v_w + (h + 1) * RET_DV].astype(F32)
                y_ref[bi, :, vsl] = (yn * gn[:, vsl] * (g * _sigmoid(g))).astype(y_ref.dtype)
                kv.append(_dot_tn(kwb[:, psl], v))
            state_ref[bi, p] = st * sdec_ref[p] + jnp.where(row_is_first, kv[0], kv[1])

    @pl.when(c == pl.num_programs(1) - 1)
    def _():
        state_out_ref[...] = state_ref[...]


def retention(z, tabs, g_gn, init_state, blk, name):
    b, s = z.shape[:2]
    nc = s // blk
    nb = 2 if b % 2 == 0 else 1
    n_pair = RET_HEADS // 2
    tab = lambda w: pl.BlockSpec((blk, w), lambda b_, c: (c, 0))
    const = lambda a: pl.BlockSpec(a.shape, lambda b_, c: (0,) * a.ndim)
    qk_w = RET_HEADS * RET_DK
    v_w = RET_HEADS * RET_DV
    z_w = 2 * v_w + 2 * qk_w
    assert (Z_RG, Z_RQ, Z_RK) == (Z_RV + v_w, Z_RV + 2 * v_w, Z_RV + 2 * v_w + qk_w)
    assert Z_RV % z_w == 0
    return pl.pallas_call(
        _retention_kernel,
        grid=(b // nb, nc),
        in_specs=[
            pl.BlockSpec((nb, blk, z_w), lambda b_, c: (b_, c, Z_RV // z_w)),
            tab(qk_w), tab(qk_w), tab(qk_w), tab(qk_w),
            const(tabs["dmat"]), const(tabs["rowdec"]), const(tabs["kw"]), const(tabs["sdec"]),
            pl.BlockSpec((1, v_w), lambda b_, c: (0, 0)),
            pl.BlockSpec((nb, n_pair, LANE, RET_DV), lambda b_, c: (b_, 0, 0, 0)),
        ],
        out_specs=[
            pl.BlockSpec((nb, blk, v_w), lambda b_, c: (b_, c, 0)),
            pl.BlockSpec((nb, n_pair, LANE, RET_DV), lambda b_, c: (b_, 0, 0, 0)),
        ],
        out_shape=[
            jax.ShapeDtypeStruct((b, s, v_w), BF16),
            jax.ShapeDtypeStruct((b, n_pair, LANE, RET_DV), F32),
        ],
        scratch_shapes=[pltpu.VMEM((nb, n_pair, LANE, RET_DV), F32)],
        compiler_params=_cparams("parallel", "arbitrary"),
        name=name,
    )(z, tabs["cos_q"], tabs["sin_q"], tabs["cos_k"], tabs["sin_k"],
      tabs["dmat"], tabs["rowdec"], tabs["kw"], tabs["sdec"], g_gn.reshape(1, -1), init_state)


def _mix_out_kernel(oa_ref, yb_ref, oc_ref, zg_ref, bg_ref, wa_ref, wb_ref,
                    wc_ref, wo_ref, g_ref, x_ref, o_ref):
    d = x_ref.shape[-1]
    sub = min(x_ref.shape[0], 512)
    for r in range(x_ref.shape[0] // sub):
        rows = slice(r * sub, (r + 1) * sub)
        merged = None
        for n, (br_ref, w_ref) in enumerate(((oa_ref, wa_ref), (yb_ref, wb_ref), (oc_ref, wc_ref))):
            cols = slice(n * d, (n + 1) * d)
            gate = _sigmoid(zg_ref[rows, cols].astype(F32) + bg_ref[:, cols])
            term = gate * _dot(br_ref[rows, :], w_ref[...])
            merged = term if merged is None else merged + term
        y = _dot(merged.astype(BF16), wo_ref[...])
        o_ref[rows, :] = x_ref[rows, :] + _rms(y, g_ref[...])


def mix_out(o_a, y_b, o_c, z, b_gate, w_a, w_b, w_c, w_out, g_post, x, tm, name):
    m, d = x.shape
    e = o_a.shape[1]
    act = pl.BlockSpec((tm, e), lambda i: (i, 0))
    assert Z_G % (3 * d) == 0
    const = lambda a: pl.BlockSpec(a.shape, lambda i: (0, 0))
    return pl.pallas_call(
        _mix_out_kernel,
        grid=(m // tm,),
        in_specs=[act, act, act, pl.BlockSpec((tm, 3 * d), lambda i: (i, Z_G // (3 * d))),
                  pl.BlockSpec((1, 3 * d), lambda i: (0, 0)),
                  const(w_a), const(w_b), const(w_c), const(w_out),
                  pl.BlockSpec((1, d), lambda i: (0, 0)),
                  pl.BlockSpec((tm, d), lambda i: (i, 0))],
        out_specs=pl.BlockSpec((tm, d), lambda i: (i, 0)),
        out_shape=jax.ShapeDtypeStruct((m, d), F32),
        compiler_params=_cparams("parallel"),
        name=name,
    )(o_a, y_b, o_c, z, b_gate.reshape(1, -1), w_a, w_b, w_c, w_out,
      g_post.reshape(1, -1), x)


def _mem_attn_kernel(x_ref, mk_ref, mv_ref, gpre_ref, wq_ref, wo_ref, gpost_ref, o_ref, *, sub):
    mk = mk_ref[0].astype(BF16)
    mv = mv_ref[0].astype(BF16)
    for r in range(x_ref.shape[1] // sub):
        rows = slice(r * sub, (r + 1) * sub)
        x = x_ref[0, rows, :]
        u = _rms(x, gpre_ref[...]).astype(BF16)
        q = _dot(u, wq_ref[...]).astype(BF16)
        outs = []
        for h in range(MEM_HEADS):
            sl = slice(h * MEM_HD, (h + 1) * MEM_HD)
            s = _dot_nt(q[:, sl], mk[:, sl]) * (MEM_HD ** -0.5 * LOG2E)
            m = jnp.max(s, axis=-1, keepdims=True)
            p = jnp.exp2(s - m)
            l = jnp.sum(p, axis=-1, keepdims=True)
            outs.append(_dot(p.astype(BF16), mv[:, sl]) / l)
        o = jnp.concatenate(outs, axis=-1).astype(BF16)
        o_ref[0, rows, :] = x + _rms(_dot(o, wo_ref[...]), gpost_ref[...])


def mem_attention(x, mk, mv, g_pre, w_q, w_o, g_post, tm, name):
    b, s, d = x.shape
    const = lambda a: pl.BlockSpec(a.shape, lambda b_, i: (0, 0))
    vec = pl.BlockSpec((1, d), lambda b_, i: (0, 0))
    mem = pl.BlockSpec((1,) + mk.shape[1:], lambda b_, i: (b_, 0, 0))
    return pl.pallas_call(
        functools.partial(_mem_attn_kernel, sub=min(tm, 512)),
        grid=(b, s // tm),
        in_specs=[pl.BlockSpec((1, tm, d), lambda b_, i: (b_, i, 0)), mem, mem,
                  vec, const(w_q), const(w_o), vec],
        out_specs=pl.BlockSpec((1, tm, d), lambda b_, i: (b_, i, 0)),
        out_shape=jax.ShapeDtypeStruct((b, s, d), F32),
        compiler_params=_cparams("parallel", "parallel"),
        name=name,
    )(x, mk, mv, g_pre.reshape(1, -1), w_q, w_o, g_post.reshape(1, -1))


def _mlp_kernel(x_ref, gpre_ref, wu_ref, wd_ref, gpost_ref, o_ref, xn_ref, acc_ref):
    j = pl.program_id(1)
    last = pl.num_programs(1) - 1

    def partial_out(xn):
        h = jnp.square(jnp.maximum(_dot(xn, wu_ref[...]), 0.0))
        return _dot(h.astype(BF16), wd_ref[...])

    @pl.when(j == 0)
    def _():
        xn = _rms(x_ref[...], gpre_ref[...]).astype(BF16)
        xn_ref[...] = xn
        acc_ref[...] = partial_out(xn)

    @pl.when(jnp.logical_and(j > 0, j < last))
    def _():
        acc_ref[...] += partial_out(xn_ref[...])

    @pl.when(j == last)
    def _():
        y = acc_ref[...] + partial_out(xn_ref[...])
        o_ref[...] = x_ref[...] + _rms(y, gpost_ref[...])


def mlp(x, g_pre, w_up, w_down, g_post, tm, tf, name):
    m, d = x.shape
    f = w_up.shape[1]
    vec = pl.BlockSpec((1, d), lambda i, j: (0, 0))
    return pl.pallas_call(
        _mlp_kernel,
        grid=(m // tm, f // tf),
        in_specs=[pl.BlockSpec((tm, d), lambda i, j: (i, 0)), vec,
                  pl.BlockSpec((d, tf), lambda i, j: (0, j)),
                  pl.BlockSpec((tf, d), lambda i, j: (j, 0)), vec],
        out_specs=pl.BlockSpec((tm, d), lambda i, j: (i, 0)),
        out_shape=jax.ShapeDtypeStruct((m, d), F32),
        scratch_shapes=[pltpu.VMEM((tm, d), BF16), pltpu.VMEM((tm, d), F32)],
        compiler_params=_cparams("parallel", "arbitrary"),
        name=name,
    )(x, g_pre.reshape(1, -1), w_up, w_down, g_post.reshape(1, -1))


def _rope_angles(pos, half):
    inv = ROPE_THETA ** (-jnp.arange(half, dtype=F32) / half)
    ang = pos.astype(F32)[:, None] * inv[None, :]
    return jnp.cos(ang), jnp.sin(ang)


def _mla_rope_tables(pos):
    cos, sin = _rope_angles(pos, MLA_ROPE // 2)
    t = pos.shape[0]
    one = jnp.ones((t, MLA_NOPE), F32)
    zero64 = jnp.zeros((t, MLA_NOPE), F32)
    pad = jnp.zeros((t, HEAD_SLAB - MLA_NOPE - MLA_ROPE), F32)
    return (jnp.concatenate([one, cos, cos, pad], axis=1),
            jnp.concatenate([zero64, -sin, sin, pad], axis=1))


def _ret_tables(pos, blk, n_real):
    cos, sin = _rope_angles(pos, RET_DK // 2)
    cos_q = jnp.tile(jnp.concatenate([cos, cos], axis=1), (1, RET_HEADS))
    sin_q = jnp.tile(jnp.concatenate([-sin, sin], axis=1), (1, RET_HEADS))
    k_scale = RET_DK ** -0.5
    log_g = jnp.log1p(-jnp.exp2(-5.0 - jnp.arange(RET_HEADS, dtype=F32)))
    idx = jnp.arange(blk, dtype=F32)
    diff = idx[:, None] - idx[None, :]
    dmat = jnp.where(diff >= 0, jnp.exp(log_g[:, None, None] * jnp.maximum(diff, 0.0)), 0.0)
    rowdec = jnp.exp(log_g[None, :] * (idx[:, None] + 1.0))
    w = jnp.where(idx[:, None] < n_real,
                  jnp.exp(log_g[None, :] * jnp.maximum(n_real - 1.0 - idx[:, None], 0.0)), 0.0)
    sdec = jnp.exp(log_g * n_real)
    n_pair = RET_HEADS // 2
    return dict(
        cos_q=cos_q, sin_q=sin_q, cos_k=cos_q * k_scale, sin_k=sin_q * k_scale,
        dmat=dmat,
        rowdec=jnp.repeat(rowdec, RET_DV, axis=1),
        kw=jnp.repeat(w, RET_DK, axis=1),
        sdec=jnp.broadcast_to(jnp.repeat(sdec, RET_DK).reshape(n_pair, LANE, 1),
                              (n_pair, LANE, RET_DV)),
    )


def _band_bias(table, tq, span, q_off, allowed):
    sub = 8
    assert span % (sub * sub) == 0
    length = tq + span - 1
    d = np.arange(length)
    ext = table[:, np.clip(d + q_off - span + 1, -MAX_REL, MAX_REL) + MAX_REL].astype(F32) * LOG2E
    rows, r_now = ext[:, None, :], 1
    for f in (sub, sub, span // (sub * sub)):
        width = rows.shape[-1] - (f - 1) * r_now
        rows = jnp.concatenate([rows[:, :, (f - 1 - a) * r_now:(f - 1 - a) * r_now + width]
                                for a in range(f)], axis=1)
        r_now *= f
    return jnp.where(allowed[None], rows[:, :, :tq], NEG_INF)


def _layer_weights(l, w_in, w_mla_uq, w_mla_ukv, w_br_a, w_br_b, w_br_c, w_out, w_mem_q, w_mem_k,
                   w_mem_v, w_mem_o, w_up, w_down):
    d = w_in.shape[1]
    parts, start = [], 0
    for n in (MLA_Q_RANK, MLA_KV_RANK, MLA_ROPE, 256, 256, 512, 512, 512, 512, 512, 3 * d):
        parts.append(w_in[l, :, start:start + n])
        start += n
    zq, zkv, zpe, rq, rk, rv, rg, cq, ck, cv, zg = parts
    cq = cq * (BAND_HD ** -0.5 * LOG2E)
    zeros = lambda n: jnp.zeros((d, n), w_in.dtype)
    w_in_l = jnp.concatenate(
        [zg, cq, ck, cv, rv, rg, rq, rk, zkv, zeros(MLA_NOPE), zpe,
         zeros(HEAD_SLAB - MLA_NOPE - MLA_ROPE), zq], axis=1).astype(BF16)
    assert w_in_l.shape[1] == Z_WIDTH
    pad_head = lambda w: jnp.pad(w, ((0, 0), (0, 0), (0, HEAD_SLAB - w.shape[-1])))
    flat = lambda w: w.reshape(w.shape[0], -1).astype(BF16)
    return dict(
        w_in=w_in_l,
        w_uq=jnp.concatenate([flat(pad_head(w_mla_uq[l])), flat(pad_head(jnp.concatenate(
            [jnp.zeros_like(w_mla_uq[l][..., :MLA_NOPE]),
             w_mla_uq[l][..., MLA_NOPE + MLA_ROPE // 2:],
             w_mla_uq[l][..., MLA_NOPE:MLA_NOPE + MLA_ROPE // 2]], axis=-1)))], axis=1),
        w_uk=flat(pad_head(w_mla_ukv[l][..., :MLA_NOPE])),
        w_uv=flat(pad_head(w_mla_ukv[l][..., MLA_NOPE:])),
        w_a=w_br_a[l].astype(BF16), w_b=w_br_b[l].astype(BF16), w_c=w_br_c[l].astype(BF16),
        w_out=w_out[l].astype(BF16),
        w_mq=flat(w_mem_q[l]),
        w_mkv=jnp.concatenate([flat(w_mem_k[l]), flat(w_mem_v[l])], axis=1),
        w_mo=w_mem_o[l].reshape(-1, d).astype(BF16),
        w_up=w_up[l].astype(BF16), w_down=w_down[l].astype(BF16),
    )


def _tile(n, pref):
    t = min(n, pref)
    while n % t:
        t -= LANE
    return t


def _trunk_layer(x, w, P, l, depth, ckv_stack, tabs, mem_k, mem_v, past, tag):
    b, s, d = x.shape
    m = b * s
    x2 = x.reshape(m, d)
    tm = _tile(m, 1024)
    tm2 = _tile(m, 512)
    z = norm_matmul(x2, P["g_pre_mix"][l], w["w_in"], BF16, tm, 2304, f"in_proj_{tag}")
    q, ckv_stack, kpe_out, k, v = mla_prep(
        z, P["g_mla_q"][l], P["g_mla_kv"][l], w["w_uq"], w["w_uk"], w["w_uv"], tabs["mla_cos"],
        tabs["mla_sin"], tm2, l, depth, ckv_stack, f"mla_prep_{tag}")
    z3 = z.reshape(b, s, Z_WIDTH)
    q3 = q.reshape(b, s, -1)
    if past is None:
        o_a = mla_attention(q3, k.reshape(b, s, -1), v.reshape(b, s, -1), _tile(s, 512),
                            f"mla_attn_{tag}")
        init = jnp.zeros((b, RET_HEADS // 2, LANE, RET_DV), F32)
        y_b, state = retention(z3, tabs["ret"], P["g_ret_gn"][l], init, tabs["ret_blk"],
                               f"retention_{tag}")
        o_c = band_attention(z3, Z_CQ // LANE, z3, Z_CK // LANE, z3, Z_CV // LANE,
                             tabs["band_bias"][l], tabs["band_tq"], True, f"band_{tag}")
        n_real = s
        band_k = z3[:, s - BAND_WINDOW:, Z_CK:Z_CK + 512]
        band_v = z3[:, s - BAND_WINDOW:, Z_CV:Z_CV + 512]
    else:
        c_ckv, c_kpe, s_ret, c_bk, c_bv = past
        n_real = CHUNK
        n_past = c_ckv.shape[1]
        kpe_pad = jnp.pad(c_kpe, ((0, 0), (0, 0), (MLA_NOPE, HEAD_SLAB - MLA_NOPE - MLA_ROPE)))
        per_b = lambda a: a.reshape(b, -1, a.shape[-1])
        o_a = mla_attention_seg(q3, c_ckv, kpe_pad, w["w_uk"], w["w_uv"], per_b(k), per_b(v),
                                n_real, f"mla_attn_{tag}")
        init = s_ret.astype(F32).reshape(b, RET_HEADS // 2, LANE, RET_DV)
        y_b, state = retention(z3, tabs["ret"], P["g_ret_gn"][l], init, s, f"retention_{tag}")
        band_k = z3[:, :n_real, Z_CK:Z_CK + 512]
        band_v = z3[:, :n_real, Z_CV:Z_CV + 512]
        span = tabs["band_bias"][l].shape[-2]
        w_band = c_bk.shape[1]
        catb = lambda c, n: jnp.pad(
            jnp.concatenate([c.reshape(b, w_band, -1).astype(BF16), n], axis=1),
            ((0, 0), (0, span - w_band - n_real), (0, 0)))
        o_c = band_attention(z3, Z_CQ // LANE, catb(c_bk, band_k), 0, catb(c_bv, band_v), 0,
                             tabs["band_bias"][l], s, False, f"band_{tag}")
    x2 = mix_out(o_a.reshape(m, -1), y_b.reshape(m, -1), o_c.reshape(m, -1), z, P["b_gate"][l],
                 w["w_a"], w["w_b"], w["w_c"], w["w_out"], P["g_post_mix"][l], x2, tm,
                 f"mix_out_{tag}")
    x3 = mem_attention(x2.reshape(b, s, d), mem_k, mem_v, P["g_pre_mem"][l], w["w_mq"], w["w_mo"],
                       P["g_post_mem"][l], _tile(s, 1024), f"mem_attn_{tag}")
    x4 = mlp(x3.reshape(m, d), P["g_pre_ff"][l], w["w_up"], w["w_down"], P["g_post_ff"][l],
             tm, 1024, f"mlp_{tag}")
    new = (kpe_out.reshape(b, s, -1)[:, :n_real],
           state.reshape(b, RET_HEADS, RET_DK, RET_DV),
           band_k.astype(F32).reshape(b, -1, BAND_HEADS, BAND_HD),
           band_v.astype(F32).reshape(b, -1, BAND_HEADS, BAND_HD))
    return x4.reshape(b, s, d), ckv_stack, new


def kernel(x_prompt, x_sample, cache_mla_ckv, cache_mla_kpe, state_ret, cache_band_k, cache_band_v, cache_mem_k, cache_mem_v, mem_prompt, g_pre_mix, w_in, g_mla_q, w_mla_uq, g_mla_kv, w_mla_ukv, g_ret_gn, band_rel_bias, w_br_a, w_br_b, w_br_c, b_gate, w_out, g_post_mix, g_pre_mem, g_mem, w_mem_q, w_mem_k, w_mem_v, w_mem_o, g_post_mem, g_pre_ff, w_up, w_down, g_post_ff):
    P = dict(g_pre_mix=g_pre_mix, g_mla_q=g_mla_q, g_mla_kv=g_mla_kv, g_ret_gn=g_ret_gn,
             b_gate=b_gate, g_post_mix=g_post_mix, g_pre_mem=g_pre_mem, g_post_mem=g_post_mem,
             g_pre_ff=g_pre_ff, g_post_ff=g_post_ff)
    depth = w_in.shape[0]
    bp, sp, d = x_prompt.shape
    bs, ss, _ = x_sample.shape
    n_past = cache_mla_ckv.shape[2]
    w_band = cache_band_k.shape[2]
    assert ss == CHUNK and sp % 512 == 0 and sp >= BAND_WINDOW
    s_pad = 2 * CHUNK

    pos_p = jnp.arange(sp)
    pos_s = n_past + jnp.arange(s_pad)
    ret_blk = 256
    band_tq = 256
    cos_p, sin_p = _mla_rope_tables(pos_p)
    cos_s, sin_s = _mla_rope_tables(pos_s)

    span_p = 3 * band_tq
    jj = np.arange(span_p)[:, None]
    ii = np.arange(band_tq)[None, :]
    band_ok = (jj // CHUNK >= ii // CHUNK) & (jj // CHUNK <= ii // CHUNK + BAND_PREV_CHUNKS)
    bias_p = [_band_bias(band_rel_bias[l], band_tq, span_p, 2 * band_tq, band_ok)
              for l in range(depth)]
    span_s = -(-(w_band + CHUNK) // LANE) * LANE
    mask_s = np.broadcast_to(np.arange(span_s)[:, None] < w_band + CHUNK, (span_s, s_pad))
    bias_s = [_band_bias(band_rel_bias[l], s_pad, span_s, w_band, mask_s) for l in range(depth)]

    tabs_p = dict(mla_cos=cos_p, mla_sin=sin_p, ret=_ret_tables(pos_p, ret_blk, ret_blk),
                  ret_blk=ret_blk, band_bias=bias_p, band_tq=band_tq)
    tabs_s = dict(mla_cos=jnp.tile(cos_s, (bs, 1)), mla_sin=jnp.tile(sin_s, (bs, 1)),
                  ret=_ret_tables(pos_s, s_pad, CHUNK), band_bias=bias_s)

    xp = x_prompt
    xs = jnp.pad(x_sample, ((0, 0), (0, s_pad - ss), (0, 0)))
    mem2 = mem_prompt.reshape(-1, d)
    new_p = [[] for _ in range(6)]
    new_s = [[] for _ in range(4)]
    ckv_p = ckv_s = None
    for l in range(depth):
        w = _layer_weights(l, w_in, w_mla_uq, w_mla_ukv, w_br_a, w_br_b, w_br_c, w_out, w_mem_q,
                           w_mem_k, w_mem_v, w_mem_o, w_up, w_down)
        mkv = norm_matmul(mem2, g_mem[l], w["w_mkv"], F32, _tile(mem2.shape[0], 1024), 512,
                          f"mem_kv_{l}")
        e = MEM_HEADS * MEM_HD
        mk = mkv[:, :e].reshape(bp, -1, e)
        mv = mkv[:, e:].reshape(bp, -1, e)
        xp, ckv_p, st_p = _trunk_layer(xp, w, P, l, depth, ckv_p, tabs_p, mk, mv, None, f"p{l}")
        xs, ckv_s, st_s = _trunk_layer(
            xs, w, P, l, depth, ckv_s, tabs_s, cache_mem_k[l].reshape(bs, -1, e),
            cache_mem_v[l].reshape(bs, -1, e),
            (cache_mla_ckv[l], cache_mla_kpe[l], state_ret[l], cache_band_k[l], cache_band_v[l]),
            f"s{l}")
        mem_shape = (bp, -1, MEM_HEADS, MEM_HD)
        for acc, t in zip(new_p, st_p + (mk.reshape(mem_shape), mv.reshape(mem_shape))):
            acc.append(t)
        for acc, t in zip(new_s, st_s):
            acc.append(t)
    stack = lambda ts: jnp.stack(ts, axis=0)
    return (xp, xs[:, :ss],
            ckv_p.reshape(depth, bp, sp, -1),
            stack(new_p[0]), stack(new_p[1]), stack(new_p[2]), stack(new_p[3]),
            stack(new_p[4]), stack(new_p[5]),
            ckv_s.reshape(depth, bs, s_pad, -1)[:, :, :ss],
            stack(new_s[0]), stack(new_s[1]), stack(new_s[2]), stack(new_s[3]))
```

```python
import functools

import numpy as np
import jax
import jax.numpy as jnp
from jax import lax
from jax.experimental import pallas as pl
from jax.experimental.pallas import tpu as pltpu

F32 = jnp.float32
BF16 = jnp.bfloat16

CHUNK = 64
MLA_HEADS = 8
MLA_Q_RANK = 384
MLA_KV_RANK = 256
MLA_NOPE = 64
MLA_ROPE = 32
MLA_V = 64
MLA_SCALE = (MLA_NOPE + MLA_ROPE) ** -0.5
RET_HEADS = 4
RET_DK = 64
RET_DV = 128
BAND_HEADS = 8
BAND_HD = 64
BAND_PREV_CHUNKS = 8
BAND_WINDOW = BAND_PREV_CHUNKS * CHUNK
MAX_REL = 128
MEM_HEADS = 4
MEM_HD = 128
ROPE_THETA = 10000.0
EPS = 1e-6
NEG_INF = -1e30
LOG2E = 1.4426950408889634

LANE = 128
HEAD_SLAB = 128
VMEM_LIMIT = 48 * 1024 * 1024

Z_G = 0
Z_CQ = 3072
Z_CK = 3584
Z_CV = 4096
Z_RV = 4608
Z_RG = 5120
Z_RQ = 5632
Z_RK = 5888
Z_KV = 6144
Z_PE = 6400
Z_Q = 6528
Z_WIDTH = 6912


def _cparams(*sem, **kw):
    return pltpu.CompilerParams(dimension_semantics=sem, vmem_limit_bytes=VMEM_LIMIT, **kw)


def _rms(x, g):
    return x * lax.rsqrt(jnp.mean(x * x, axis=-1, keepdims=True) + EPS) * g


def _dot(a, b):
    return jnp.dot(a, b, preferred_element_type=F32)


def _dot_nt(a, b):
    return lax.dot_general(a, b, (((1,), (1,)), ((), ())), preferred_element_type=F32)


def _dot_tn(a, b):
    return lax.dot_general(a, b, (((0,), (0,)), ((), ())), preferred_element_type=F32)


def _sigmoid(x):
    return 0.5 * jnp.tanh(0.5 * x) + 0.5


def _lane_iota(shape):
    return lax.broadcasted_iota(jnp.int32, shape, len(shape) - 1)


def _norm_matmul_kernel(x_ref, g_ref, w_ref, o_ref, xn_ref):
    @pl.when(pl.program_id(1) == 0)
    def _():
        xn = _rms(x_ref[...].astype(F32), g_ref[...]).astype(BF16)
        xn_ref[...] = xn
        o_ref[...] = _dot(xn, w_ref[...]).astype(o_ref.dtype)

    @pl.when(pl.program_id(1) > 0)
    def _():
        o_ref[...] = _dot(xn_ref[...], w_ref[...]).astype(o_ref.dtype)


def norm_matmul(x, g, w, out_dtype, tm, tn, name):
    m, k = x.shape
    n = w.shape[1]
    assert m % tm == 0 and n % tn == 0, (m, tm, n, tn)
    return pl.pallas_call(
        _norm_matmul_kernel,
        grid=(m // tm, n // tn),
        in_specs=[
            pl.BlockSpec((tm, k), lambda i, j: (i, 0)),
            pl.BlockSpec((1, k), lambda i, j: (0, 0)),
            pl.BlockSpec((k, tn), lambda i, j: (0, j)),
        ],
        out_specs=pl.BlockSpec((tm, tn), lambda i, j: (i, j)),
        out_shape=jax.ShapeDtypeStruct((m, n), out_dtype),
        scratch_shapes=[pltpu.VMEM((tm, k), BF16)],
        compiler_params=_cparams("parallel", "arbitrary"),
        name=name,
    )(x, g.reshape(1, k), w)


def _rope_slab(x, cos, sin, rot):
    first_end, half, period = rot
    width = x.shape[1]
    right = pltpu.roll(x, width - half, 1)
    left = pltpu.roll(x, half, 1)
    partner = jnp.where((_lane_iota(x.shape) & (period - 1)) < first_end, right, left)
    return x * cos + partner * sin


_MLA_ROT = (MLA_NOPE + MLA_ROPE // 2, MLA_ROPE // 2, LANE)
_RET_ROT = (RET_DK // 2, RET_DK // 2, RET_DK)


_Z_MLA = Z_KV
_Z_MLA_W = Z_WIDTH - Z_KV
assert (Z_KV, Z_PE, Z_Q) == (_Z_MLA, _Z_MLA + MLA_KV_RANK, _Z_MLA + MLA_KV_RANK + HEAD_SLAB)
assert _Z_MLA % _Z_MLA_W == 0


def _mla_kv_heads(ckv, kpe, wk_ref, wv_ref, k_ref, v_ref):
    c = ckv.astype(BF16)
    kn = _dot(c, wk_ref[...])
    for h in range(MLA_HEADS):
        sl = slice(h * HEAD_SLAB, (h + 1) * HEAD_SLAB)
        k_ref[:, sl] = (kn[:, sl] + kpe).astype(BF16)
    v = _dot(c, wv_ref[...])
    ones_lane = (_lane_iota(v.shape) & (HEAD_SLAB - 1)) == MLA_V
    v_ref[...] = jnp.where(ones_lane, 1.0, v).astype(BF16)


def _mla_prep_kernel(z_ref, gq_ref, gkv_ref, wq_ref, wk_ref, wv_ref, cos_ref, sin_ref, *refs):
    q_ref, ckv_ref, kpe_out_ref, k_ref, v_ref = refs[-5:]
    cos = cos_ref[...]
    sin = sin_ref[...]
    zkv = z_ref[:, :MLA_KV_RANK]
    zpe = z_ref[:, MLA_KV_RANK:MLA_KV_RANK + HEAD_SLAB]
    zq = z_ref[:, MLA_KV_RANK + HEAD_SLAB:]
    qn = _rms(zq.astype(F32), gq_ref[...]).astype(BF16)
    q = _dot(qn, wq_ref[...])
    cos_q = cos * (MLA_SCALE * LOG2E)
    sin_q = sin * (MLA_SCALE * LOG2E)
    for h in range(MLA_HEADS):
        sl = slice(h * HEAD_SLAB, (h + 1) * HEAD_SLAB)
        pt = slice((MLA_HEADS + h) * HEAD_SLAB, (MLA_HEADS + h + 1) * HEAD_SLAB)
        q_ref[:, sl] = (q[:, sl] * cos_q + q[:, pt] * sin_q).astype(BF16)
    ckv = _rms(zkv.astype(F32), gkv_ref[...])
    ckv_ref[0] = ckv
    for later in range(1, ckv_ref.shape[0]):
        ckv_ref[later] = jnp.zeros_like(ckv)
    kpe = _rope_slab(zpe.astype(F32), cos, sin, _MLA_ROT)
    kpe_out_ref[...] = kpe[:, MLA_NOPE:MLA_NOPE + MLA_ROPE]
    _mla_kv_heads(ckv, kpe, wk_ref, wv_ref, k_ref, v_ref)


def mla_prep(z, g_q, g_kv, w_uq, w_uk, w_uv, cos, sin, tm, layer, depth, ckv_stack, name):
    m = z.shape[0]
    nt = cos.shape[0] // tm
    row = lambda w: pl.BlockSpec((1, w), lambda i: (0, 0))
    const = lambda a: pl.BlockSpec(a.shape, lambda i: (0, 0))
    slabs = MLA_HEADS * HEAD_SLAB
    out = lambda w: pl.BlockSpec((tm, w), lambda i: (i, 0))
    in_specs = [
        pl.BlockSpec((tm, _Z_MLA_W), lambda i: (i, _Z_MLA // _Z_MLA_W)),
        row(MLA_Q_RANK), row(MLA_KV_RANK), const(w_uq), const(w_uk), const(w_uv),
        pl.BlockSpec((tm, HEAD_SLAB), lambda i: (i % nt, 0)),
        pl.BlockSpec((tm, HEAD_SLAB), lambda i: (i % nt, 0)),
    ]
    operands = [z, g_q.reshape(1, -1), g_kv.reshape(1, -1), w_uq, w_uk, w_uv, cos, sin]
    if ckv_stack is None:
        assert layer == 0
        ckv_spec = pl.BlockSpec((depth, tm, MLA_KV_RANK), lambda i: (0, i, 0))
        aliases = {}
    else:
        ckv_spec = pl.BlockSpec((1, tm, MLA_KV_RANK), lambda i: (layer, i, 0))
        aliases = {len(operands): 1}
        in_specs.append(pl.BlockSpec(memory_space=pl.ANY))
        operands.append(ckv_stack)
    return pl.pallas_call(
        _mla_prep_kernel,
        grid=(m // tm,),
        in_specs=in_specs,
        out_specs=[out(slabs), ckv_spec, out(MLA_ROPE), out(slabs), out(slabs)],
        out_shape=[
            jax.ShapeDtypeStruct((m, slabs), BF16),
            jax.ShapeDtypeStruct((depth, m, MLA_KV_RANK), F32),
            jax.ShapeDtypeStruct((m, MLA_ROPE), F32),
            jax.ShapeDtypeStruct((m, slabs), BF16),
            jax.ShapeDtypeStruct((m, slabs), BF16),
        ],
        input_output_aliases=aliases,
        compiler_params=_cparams("parallel"),
        name=name,
    )(*operands)


def _head_of_pair(x, hh):
    lane = _lane_iota(x.shape)
    keep = (lane < BAND_HD) if hh == 0 else (lane >= BAND_HD)
    return jnp.where(keep, x, jnp.zeros_like(x))


MLA_GROUP = 4
BAND_GROUP = 4
_PAIR_SLABS = [slice(hh * HEAD_SLAB, (hh + 1) * HEAD_SLAB) for hh in range(MLA_GROUP)]


def _mla_out(acc):
    return acc[:MLA_V, :] / acc[MLA_V:MLA_V + 1, :]


def _mla_attn_kernel(q_ref, k_ref, v_ref, o_ref, acc_ref, sa_ref, sb_ref, ma_ref, mb_ref, *, t, nq):
    sa_ref, sb_ref = (sa_ref, ma_ref), (sb_ref, mb_ref)
    shift = CHUNK.bit_length() - 1
    key_chunk = jnp.right_shift(lax.broadcasted_iota(jnp.int32, (t, t), 0), shift)
    qry_chunk = jnp.right_shift(lax.broadcasted_iota(jnp.int32, (t, t), 1), shift)
    diag = key_chunk <= qry_chunk

    def rows_of(kb):
        return pl.ds(kb * t if isinstance(kb, int) else pl.multiple_of(kb * t, t), t)

    def scores(qt, kb, buf):
        for hh, sl in enumerate(_PAIR_SLABS):
            s = _dot_nt(k_ref[0, rows_of(kb), sl], q_ref[0, rows_of(qt), sl])
            buf[0][hh, :, :t] = s
            buf[1][hh] = jnp.max(s, axis=0, keepdims=True)

    def tile(qi, carry):
        _mla_attn_tile(qi, nq, scores, k_ref, v_ref, o_ref, acc_ref, sa_ref, sb_ref, diag, rows_of,
                       t)
        return carry

    acc_ref[...] = jnp.zeros_like(acc_ref)
    scores(0, 0, sa_ref)
    lax.fori_loop(0, nq, tile, 0)


def _mla_attn_tile(qi, nq, scores_of, k_ref, v_ref, o_ref, acc_ref, sa_ref, sb_ref, diag, rows_of,
                   t):
    scores = functools.partial(scores_of, qi)
    next_first = lambda: scores_of(jnp.minimum(qi + 1, nq - 1), 0, sa_ref)

    def consume(kb, buf, ms, mask):
        new_m = []
        for hh, sl in enumerate(_PAIR_SLABS):
            s = buf[0][hh, :, :t]
            if mask is None:
                blk_max = buf[1][hh]
            else:
                s = jnp.where(mask, s, NEG_INF)
                blk_max = jnp.max(s, axis=0, keepdims=True)
            m = jnp.maximum(ms[hh], blk_max)
            alpha = jnp.exp2(ms[hh] - m)
            p = jnp.exp2(s - m).astype(BF16)
            acc_ref[hh] = alpha * acc_ref[hh] + _dot_tn(v_ref[0, rows_of(kb), sl], p)
            new_m.append(m)
        return tuple(new_m)

    def finish():
        outs = [_mla_out(acc_ref[hh]) for hh in range(MLA_GROUP)]
        o_ref[0, rows_of(qi), :] = jnp.concatenate(outs, axis=0).T.astype(o_ref.dtype)

    ms = tuple(jnp.full((1, t), NEG_INF, F32) for _ in range(MLA_GROUP))

    def pair(j, ms):
        kb = 2 * j
        scores(kb + 1, sb_ref)
        ms = consume(kb, sa_ref, ms, None)
        scores(kb + 2, sa_ref)
        return consume(kb + 1, sb_ref, ms, None)

    ms = lax.fori_loop(0, qi // 2, pair, ms)

    @pl.when(qi % 2 == 0)
    def _():
        consume(qi, sa_ref, ms, diag)
        next_first()
        finish()

    @pl.when(qi % 2 == 1)
    def _():
        scores(qi, sb_ref)
        ms1 = consume(qi - 1, sa_ref, ms, None)
        next_first()
        consume(qi, sb_ref, ms1, diag)
        finish()


def _mla_attn_seg_kernel(q_ref, ckv1_ref, kpe1_ref, wk_ref, wv_ref, k2_ref, v2_ref, o_ref,
                         k1_ref, v1_ref, *, n2_valid, chunk):
    for r in range(ckv1_ref.shape[1] // chunk):
        rows = pl.ds(r * chunk, chunk)
        _mla_kv_heads(ckv1_ref[0, rows, :], kpe1_ref[0, rows, :], wk_ref, wv_ref,
                      k1_ref.at[rows], v1_ref.at[rows])
    for pair in range(MLA_HEADS // 2):
        outs = []
        for hh in range(2):
            h = 2 * pair + hh
            sl = slice(h * HEAD_SLAB, (h + 1) * HEAD_SLAB)
            q = q_ref[0, :, sl]
            s1 = _dot_nt(k1_ref[:, sl], q)
            s2 = _dot_nt(k2_ref[0, :, sl], q)
            s2 = jnp.where(lax.broadcasted_iota(jnp.int32, s2.shape, 0) < n2_valid, s2, NEG_INF)
            m = jnp.maximum(jnp.max(s1, axis=0, keepdims=True), jnp.max(s2, axis=0, keepdims=True))
            acc = (_dot_tn(v1_ref[:, sl], jnp.exp2(s1 - m).astype(BF16))
                   + _dot_tn(v2_ref[0, :, sl], jnp.exp2(s2 - m).astype(BF16)))
            outs.append(_mla_out(acc))
        o_ref[0, :, pair * LANE:(pair + 1) * LANE] = (
            jnp.concatenate(outs, axis=0).T.astype(o_ref.dtype))


def mla_attention_seg(q, ckv1, kpe1, w_uk, w_uv, k2, v2, n2_valid, name):
    b, sq, _ = q.shape
    n1 = ckv1.shape[1]
    slabs = MLA_HEADS * HEAD_SLAB
    blk = lambda a: pl.BlockSpec((1,) + a.shape[1:], lambda b_: (b_, 0, 0))
    const = lambda a: pl.BlockSpec(a.shape, lambda b_: (0, 0))
    return pl.pallas_call(
        functools.partial(_mla_attn_seg_kernel, n2_valid=n2_valid, chunk=_tile(n1, 512)),
        grid=(b,),
        in_specs=[blk(q), blk(ckv1), blk(kpe1), const(w_uk), const(w_uv), blk(k2), blk(v2)],
        out_specs=pl.BlockSpec((1, sq, MLA_HEADS * MLA_V), lambda b_: (b_, 0, 0)),
        out_shape=jax.ShapeDtypeStruct((b, sq, MLA_HEADS * MLA_V), BF16),
        scratch_shapes=[pltpu.VMEM((n1, slabs), BF16), pltpu.VMEM((n1, slabs), BF16)],
        compiler_params=_cparams("parallel"),
        name=name,
    )(q, ckv1, kpe1, w_uk, w_uv, k2, v2)


def mla_attention(q, k, v, t, name):
    b, s, _ = q.shape
    assert s % t == 0 and k.shape[1] == s
    g = MLA_GROUP
    whole = pl.BlockSpec((1, s, g * HEAD_SLAB), lambda b_, h: (b_, 0, h))
    return pl.pallas_call(
        functools.partial(_mla_attn_kernel, t=t, nq=s // t),
        grid=(b, MLA_HEADS // g),
        in_specs=[whole, whole, whole],
        out_specs=pl.BlockSpec((1, s, g * MLA_V), lambda b_, h: (b_, 0, h)),
        out_shape=jax.ShapeDtypeStruct((b, s, MLA_HEADS * MLA_V), BF16),
        scratch_shapes=[pltpu.VMEM((g, HEAD_SLAB, t), F32), pltpu.VMEM((g, t, t + LANE), F32),
                        pltpu.VMEM((g, t, t + LANE), F32), pltpu.VMEM((g, 1, t), F32),
                        pltpu.VMEM((g, 1, t), F32)],
        compiler_params=_cparams("parallel", "parallel"),
        name=name,
    )(q, k, v)


def _band_attn_kernel(q_ref, k_ref, v_ref, bias_ref, o_ref, sa_ref, sb_ref, ma_ref, mb_ref, *,
                      tq, nq, span, sliding):
    def geom(i):
        if not sliding:
            return 0, span, 0
        if isinstance(i, int):
            n = min(i + 1, span // tq) * tq
            return max(i + 1 - span // tq, 0) * tq, n, span - n
        return pl.multiple_of((i + 1 - span // tq) * tq, tq), span, 0

    def q_rows(i):
        return pl.ds(i * tq if isinstance(i, int) else pl.multiple_of(i * tq, tq), tq)

    sa_ref, sb_ref = (sa_ref, ma_ref), (sb_ref, mb_ref)

    n_pairs = q_ref.shape[-1] // LANE
    pair_lanes = [slice(p * LANE, (p + 1) * LANE) for p in range(n_pairs)]

    def scores(i, buf):
        start, n, boff = geom(i)
        for pr, lanes in enumerate(pair_lanes):
            q = q_ref[0, q_rows(i), lanes]
            k = k_ref[0, pl.ds(start, n), lanes]
            for hh in range(2):
                h = 2 * pr + hh
                s = _dot_nt(k, _head_of_pair(q, hh)) + bias_ref[0, h, boff:boff + n, :]
                buf[0][h, :n, :tq] = s
                buf[1][h] = jnp.max(s, axis=0, keepdims=True)

    def consume(i, buf):
        start, n, _ = geom(i)
        tiles = []
        for pr, lanes in enumerate(pair_lanes):
            v = v_ref[0, pl.ds(start, n), lanes]
            outs = []
            for hh in range(2):
                h = 2 * pr + hh
                p = jnp.exp2(buf[0][h, :n, :tq] - buf[1][h])
                l = jnp.sum(p, axis=0, keepdims=True)
                outs.append(_dot_tn(v, p.astype(BF16)) / l)
            first = lax.broadcasted_iota(jnp.int32, outs[0].shape, 0) < BAND_HD
            tiles.append(jnp.where(first, outs[0], outs[1]))
        o_ref[0, q_rows(i), :] = jnp.concatenate(tiles, axis=0).T.astype(o_ref.dtype)

    scores(0, sa_ref)
    if nq == 1:
        consume(0, sa_ref)
        return
    scores(1, sb_ref)
    consume(0, sa_ref)
    scores(2, sa_ref)
    consume(1, sb_ref)

    def pair(j, carry):
        i = 2 * j + 2
        scores(i + 1, sb_ref)
        consume(i, sa_ref)
        scores(i + 2, sa_ref)
        consume(i + 1, sb_ref)
        return carry

    lax.fori_loop(0, (nq - 2) // 2 - 1, pair, 0)
    scores(nq - 1, sb_ref)
    consume(nq - 2, sa_ref)
    consume(nq - 1, sb_ref)


def band_attention(q, q_col, k, k_col, v, v_col, bias, tq, sliding, name):
    b, sq = q.shape[:2]
    sk = k.shape[1]
    nq = sq // tq
    span = bias.shape[-2]
    assert (sliding and sk == sq and nq >= 4 and nq % 2 == 0) or (nq == 1 and sk == span)
    g = BAND_GROUP
    w = g * BAND_HD
    assert q_col * LANE % w == 0 and k_col * LANE % w == 0 and v_col * LANE % w == 0
    col = lambda c: (lambda hg, b_: (b_, 0, c * LANE // w + hg))
    return pl.pallas_call(
        functools.partial(_band_attn_kernel, tq=tq, nq=nq, span=span, sliding=sliding),
        grid=(BAND_HEADS // g, b),
        in_specs=[
            pl.BlockSpec((1, sq, w), col(q_col)),
            pl.BlockSpec((1, sk, w), col(k_col)),
            pl.BlockSpec((1, sk, w), col(v_col)),
            pl.BlockSpec((1, g, span, tq), lambda hg, b_: (0, hg, 0, 0)),
        ],
        out_specs=pl.BlockSpec((1, sq, w), col(0)),
        out_shape=jax.ShapeDtypeStruct((b, sq, BAND_HEADS * BAND_HD), BF16),
        scratch_shapes=[pltpu.VMEM((g, span, tq + LANE), F32),
                        pltpu.VMEM((g, span, tq + LANE), F32),
                        pltpu.VMEM((g, 1, tq), F32), pltpu.VMEM((g, 1, tq), F32)],
        compiler_params=_cparams("parallel", "arbitrary"),
        name=name,
    )(q, k, v, bias.reshape((1,) + bias.shape))


def _retention_kernel(z_ref, cq_ref, sq_ref, ck_ref, sk_ref,
                      dmat_ref, rowdec_ref, kw_ref, sdec_ref, gn_ref, init_ref,
                      y_ref, state_out_ref, state_ref):
    c = pl.program_id(1)

    @pl.when(c == 0)
    def _():
        state_ref[...] = init_ref[...]

    n_pair = RET_HEADS // 2
    gn = gn_ref[...]
    row_is_first = lax.broadcasted_iota(jnp.int32, (LANE, RET_DV), 0) < RET_DK
    v_w = RET_HEADS * RET_DV
    qk_w = RET_HEADS * RET_DK
    for bi in range(z_ref.shape[0]):
        rq = z_ref[bi, :, 2 * v_w:2 * v_w + qk_w]
        rk = z_ref[bi, :, 2 * v_w + qk_w:]
        q = _rope_slab(rq.astype(F32), cq_ref[...], sq_ref[...], _RET_ROT)
        k = _rope_slab(rk.astype(F32), ck_ref[...], sk_ref[...], _RET_ROT)
        qb = q.astype(BF16)
        kb = k.astype(BF16)
        kwb = (k * kw_ref[...]).astype(BF16)
        for p in range(n_pair):
            psl = slice(p * LANE, (p + 1) * LANE)
            st = state_ref[bi, p]
            stb = st.astype(BF16)
            kv = []
            for hh in range(2):
                h = 2 * p + hh
                vsl = slice(h * RET_DV, (h + 1) * RET_DV)
                v = z_ref[bi, :, vsl]
                qh = _head_of_pair(qb[:, psl], hh)
                s = _dot_nt(qh, kb[:, psl]) * dmat_ref[h]
                o = _dot(s.astype(BF16), v) + _dot(qh, stb) * rowdec_ref[:, vsl]
                mu = jnp.mean(o, axis=-1, keepdims=True)
                d = o - mu
                yn = d * lax.rsqrt(jnp.mean(d * d, axis=-1, keepdims=True) + EPS)
                g = z_ref[bi, :, v_w + h * RET_DV:v_w + (h + 1) * RET_DV].astype(F32)
                y_ref[bi, :, vsl] = (yn * gn[:, vsl] * (g * _sigmoid(g))).astype(y_ref.dtype)
                kv.append(_dot_tn(kwb[:, psl], v))
            state_ref[bi, p] = st * sdec_ref[p] + jnp.where(row_is_first, kv[0], kv[1])

    @pl.when(c == pl.num_programs(1) - 1)
    def _():
        state_out_ref[...] = state_ref[...]


def retention(z, tabs, g_gn, init_state, blk, name):
    b, s = z.shape[:2]
    nc = s // blk
    nb = 2 if b % 2 == 0 else 1
    n_pair = RET_HEADS // 2
    tab = lambda w: pl.BlockSpec((blk, w), lambda b_, c: (c, 0))
    const = lambda a: pl.BlockSpec(a.shape, lambda b_, c: (0,) * a.ndim)
    qk_w = RET_HEADS * RET_DK
    v_w = RET_HEADS * RET_DV
    z_w = 2 * v_w + 2 * qk_w
    assert (Z_RG, Z_RQ, Z_RK) == (Z_RV + v_w, Z_RV + 2 * v_w, Z_RV + 2 * v_w + qk_w)
    assert Z_RV % z_w == 0
    return pl.pallas_call(
        _retention_kernel,
        grid=(b // nb, nc),
        in_specs=[
            pl.BlockSpec((nb, blk, z_w), lambda b_, c: (b_, c, Z_RV // z_w)),
            tab(qk_w), tab(qk_w), tab(qk_w), tab(qk_w),
            const(tabs["dmat"]), const(tabs["rowdec"]), const(tabs["kw"]), const(tabs["sdec"]),
            pl.BlockSpec((1, v_w), lambda b_, c: (0, 0)),
            pl.BlockSpec((nb, n_pair, LANE, RET_DV), lambda b_, c: (b_, 0, 0, 0)),
        ],
        out_specs=[
            pl.BlockSpec((nb, blk, v_w), lambda b_, c: (b_, c, 0)),
            pl.BlockSpec((nb, n_pair, LANE, RET_DV), lambda b_, c: (b_, 0, 0, 0)),
        ],
        out_shape=[
            jax.ShapeDtypeStruct((b, s, v_w), BF16),
            jax.ShapeDtypeStruct((b, n_pair, LANE, RET_DV), F32),
        ],
        scratch_shapes=[pltpu.VMEM((nb, n_pair, LANE, RET_DV), F32)],
        compiler_params=_cparams("parallel", "arbitrary"),
        name=name,
    )(z, tabs["cos_q"], tabs["sin_q"], tabs["cos_k"], tabs["sin_k"],
      tabs["dmat"], tabs["rowdec"], tabs["kw"], tabs["sdec"], g_gn.reshape(1, -1), init_state)


def _mix_out_kernel(oa_ref, yb_ref, oc_ref, zg_ref, bg_ref, wa_ref, wb_ref,
                    wc_ref, wo_ref, g_ref, x_ref, o_ref):
    d = x_ref.shape[-1]
    sub = min(x_ref.shape[0], 512)
    for r in range(x_ref.shape[0] // sub):
        rows = slice(r * sub, (r + 1) * sub)
        merged = None
        for n, (br_ref, w_ref) in enumerate(((oa_ref, wa_ref), (yb_ref, wb_ref), (oc_ref, wc_ref))):
            cols = slice(n * d, (n + 1) * d)
            gate = _sigmoid(zg_ref[rows, cols].astype(F32) + bg_ref[:, cols])
            term = gate * _dot(br_ref[rows, :], w_ref[...])
            merged = term if merged is None else merged + term
        y = _dot(merged.astype(BF16), wo_ref[...])
        o_ref[rows, :] = x_ref[rows, :] + _rms(y, g_ref[...])


def mix_out(o_a, y_b, o_c, z, b_gate, w_a, w_b, w_c, w_out, g_post, x, tm, name):
    m, d = x.shape
    e = o_a.shape[1]
    act = pl.BlockSpec((tm, e), lambda i: (i, 0))
    assert Z_G % (3 * d) == 0
    const = lambda a: pl.BlockSpec(a.shape, lambda i: (0, 0))
    return pl.pallas_call(
        _mix_out_kernel,
        grid=(m // tm,),
        in_specs=[act, act, act, pl.BlockSpec((tm, 3 * d), lambda i: (i, Z_G // (3 * d))),
                  pl.BlockSpec((1, 3 * d), lambda i: (0, 0)),
                  const(w_a), const(w_b), const(w_c), const(w_out),
                  pl.BlockSpec((1, d), lambda i: (0, 0)),
                  pl.BlockSpec((tm, d), lambda i: (i, 0))],
        out_specs=pl.BlockSpec((tm, d), lambda i: (i, 0)),
        out_shape=jax.ShapeDtypeStruct((m, d), F32),
        compiler_params=_cparams("parallel"),
        name=name,
    )(o_a, y_b, o_c, z, b_gate.reshape(1, -1), w_a, w_b, w_c, w_out,
      g_post.reshape(1, -1), x)


def _mem_attn_kernel(x_ref, mk_ref, mv_ref, gpre_ref, wq_ref, wo_ref, gpost_ref, o_ref, *, sub):
    mk = mk_ref[0].astype(BF16)
    mv = mv_ref[0].astype(BF16)
    for r in range(x_ref.shape[1] // sub):
        rows = slice(r * sub, (r + 1) * sub)
        x = x_ref[0, rows, :]
        u = _rms(x, gpre_ref[...]).astype(BF16)
        q = _dot(u, wq_ref[...]).astype(BF16)
        outs = []
        for h in range(MEM_HEADS):
            sl = slice(h * MEM_HD, (h + 1) * MEM_HD)
            s = _dot_nt(q[:, sl], mk[:, sl]) * (MEM_HD ** -0.5 * LOG2E)
            m = jnp.max(s, axis=-1, keepdims=True)
            p = jnp.exp2(s - m)
            l = jnp.sum(p, axis=-1, keepdims=True)
            outs.append(_dot(p.astype(BF16), mv[:, sl]) / l)
        o = jnp.concatenate(outs, axis=-1).astype(BF16)
        o_ref[0, rows, :] = x + _rms(_dot(o, wo_ref[...]), gpost_ref[...])


def mem_attention(x, mk, mv, g_pre, w_q, w_o, g_post, tm, name):
    b, s, d = x.shape
    const = lambda a: pl.BlockSpec(a.shape, lambda b_, i: (0, 0))
    vec = pl.BlockSpec((1, d), lambda b_, i: (0, 0))
    mem = pl.BlockSpec((1,) + mk.shape[1:], lambda b_, i: (b_, 0, 0))
    return pl.pallas_call(
        functools.partial(_mem_attn_kernel, sub=min(tm, 512)),
        grid=(b, s // tm),
        in_specs=[pl.BlockSpec((1, tm, d), lambda b_, i: (b_, i, 0)), mem, mem,
                  vec, const(w_q), const(w_o), vec],
        out_specs=pl.BlockSpec((1, tm, d), lambda b_, i: (b_, i, 0)),
        out_shape=jax.ShapeDtypeStruct((b, s, d), F32),
        compiler_params=_cparams("parallel", "parallel"),
        name=name,
    )(x, mk, mv, g_pre.reshape(1, -1), w_q, w_o, g_post.reshape(1, -1))


def _mlp_kernel(x_ref, gpre_ref, wu_ref, wd_ref, gpost_ref, o_ref, xn_ref, acc_ref):
    j = pl.program_id(1)
    last = pl.num_programs(1) - 1

    def partial_out(xn):
        h = jnp.square(jnp.maximum(_dot(xn, wu_ref[...]), 0.0))
        return _dot(h.astype(BF16), wd_ref[...])

    @pl.when(j == 0)
    def _():
        xn = _rms(x_ref[...], gpre_ref[...]).astype(BF16)
        xn_ref[...] = xn
        acc_ref[...] = partial_out(xn)

    @pl.when(jnp.logical_and(j > 0, j < last))
    def _():
        acc_ref[...] += partial_out(xn_ref[...])

    @pl.when(j == last)
    def _():
        y = acc_ref[...] + partial_out(xn_ref[...])
        o_ref[...] = x_ref[...] + _rms(y, gpost_ref[...])


def mlp(x, g_pre, w_up, w_down, g_post, tm, tf, name):
    m, d = x.shape
    f = w_up.shape[1]
    vec = pl.BlockSpec((1, d), lambda i, j: (0, 0))
    return pl.pallas_call(
        _mlp_kernel,
        grid=(m // tm, f // tf),
        in_specs=[pl.BlockSpec((tm, d), lambda i, j: (i, 0)), vec,
                  pl.BlockSpec((d, tf), lambda i, j: (0, j)),
                  pl.BlockSpec((tf, d), lambda i, j: (j, 0)), vec],
        out_specs=pl.BlockSpec((tm, d), lambda i, j: (i, 0)),
        out_shape=jax.ShapeDtypeStruct((m, d), F32),
        scratch_shapes=[pltpu.VMEM((tm, d), BF16), pltpu.VMEM((tm, d), F32)],
        compiler_params=_cparams("parallel", "arbitrary"),
        name=name,
    )(x, g_pre.reshape(1, -1), w_up, w_down, g_post.reshape(1, -1))


def _rope_angles(pos, half):
    inv = ROPE_THETA ** (-jnp.arange(half, dtype=F32) / half)
    ang = pos.astype(F32)[:, None] * inv[None, :]
    return jnp.cos(ang), jnp.sin(ang)


def _mla_rope_tables(pos):
    cos, sin = _rope_angles(pos, MLA_ROPE // 2)
    t = pos.shape[0]
    one = jnp.ones((t, MLA_NOPE), F32)
    zero64 = jnp.zeros((t, MLA_NOPE), F32)
    pad = jnp.zeros((t, HEAD_SLAB - MLA_NOPE - MLA_ROPE), F32)
    return (jnp.concatenate([one, cos, cos, pad], axis=1),
            jnp.concatenate([zero64, -sin, sin, pad], axis=1))


def _ret_tables(pos, blk, n_real):
    cos, sin = _rope_angles(pos, RET_DK // 2)
    cos_q = jnp.tile(jnp.concatenate([cos, cos], axis=1), (1, RET_HEADS))
    sin_q = jnp.tile(jnp.concatenate([-sin, sin], axis=1), (1, RET_HEADS))
    k_scale = RET_DK ** -0.5
    log_g = jnp.log1p(-jnp.exp2(-5.0 - jnp.arange(RET_HEADS, dtype=F32)))
    idx = jnp.arange(blk, dtype=F32)
    diff = idx[:, None] - idx[None, :]
    dmat = jnp.where(diff >= 0, jnp.exp(log_g[:, None, None] * jnp.maximum(diff, 0.0)), 0.0)
    rowdec = jnp.exp(log_g[None, :] * (idx[:, None] + 1.0))
    w = jnp.where(idx[:, None] < n_real,
                  jnp.exp(log_g[None, :] * jnp.maximum(n_real - 1.0 - idx[:, None], 0.0)), 0.0)
    sdec = jnp.exp(log_g * n_real)
    n_pair = RET_HEADS // 2
    return dict(
        cos_q=cos_q, sin_q=sin_q, cos_k=cos_q * k_scale, sin_k=sin_q * k_scale,
        dmat=dmat,
        rowdec=jnp.repeat(rowdec, RET_DV, axis=1),
        kw=jnp.repeat(w, RET_DK, axis=1),
        sdec=jnp.broadcast_to(jnp.repeat(sdec, RET_DK).reshape(n_pair, LANE, 1),
                              (n_pair, LANE, RET_DV)),
    )


def _band_bias(table, tq, span, q_off, allowed):
    sub = 8
    assert span % (sub * sub) == 0
    length = tq + span - 1
    d = np.arange(length)
    ext = table[:, np.clip(d + q_off - span + 1, -MAX_REL, MAX_REL) + MAX_REL].astype(F32) * LOG2E
    rows, r_now = ext[:, None, :], 1
    for f in (sub, sub, span // (sub * sub)):
        width = rows.shape[-1] - (f - 1) * r_now
        rows = jnp.concatenate([rows[:, :, (f - 1 - a) * r_now:(f - 1 - a) * r_now + width]
                                for a in range(f)], axis=1)
        r_now *= f
    return jnp.where(allowed[None], rows[:, :, :tq], NEG_INF)


def _layer_weights(l, w_in, w_mla_uq, w_mla_ukv, w_br_a, w_br_b, w_br_c, w_out, w_mem_q, w_mem_k,
                   w_mem_v, w_mem_o, w_up, w_down):
    d = w_in.shape[1]
    parts, start = [], 0
    for n in (MLA_Q_RANK, MLA_KV_RANK, MLA_ROPE, 256, 256, 512, 512, 512, 512, 512, 3 * d):
        parts.append(w_in[l, :, start:start + n])
        start += n
    zq, zkv, zpe, rq, rk, rv, rg, cq, ck, cv, zg = parts
    cq = cq * (BAND_HD ** -0.5 * LOG2E)
    zeros = lambda n: jnp.zeros((d, n), w_in.dtype)
    w_in_l = jnp.concatenate(
        [zg, cq, ck, cv, rv, rg, rq, rk, zkv, zeros(MLA_NOPE), zpe,
         zeros(HEAD_SLAB - MLA_NOPE - MLA_ROPE), zq], axis=1).astype(BF16)
    assert w_in_l.shape[1] == Z_WIDTH
    pad_head = lambda w: jnp.pad(w, ((0, 0), (0, 0), (0, HEAD_SLAB - w.shape[-1])))
    flat = lambda w: w.reshape(w.shape[0], -1).astype(BF16)
    return dict(
        w_in=w_in_l,
        w_uq=jnp.concatenate([flat(pad_head(w_mla_uq[l])), flat(pad_head(jnp.concatenate(
            [jnp.zeros_like(w_mla_uq[l][..., :MLA_NOPE]),
             w_mla_uq[l][..., MLA_NOPE + MLA_ROPE // 2:],
             w_mla_uq[l][..., MLA_NOPE:MLA_NOPE + MLA_ROPE // 2]], axis=-1)))], axis=1),
        w_uk=flat(pad_head(w_mla_ukv[l][..., :MLA_NOPE])),
        w_uv=flat(pad_head(w_mla_ukv[l][..., MLA_NOPE:])),
        w_a=w_br_a[l].astype(BF16), w_b=w_br_b[l].astype(BF16), w_c=w_br_c[l].astype(BF16),
        w_out=w_out[l].astype(BF16),
        w_mq=flat(w_mem_q[l]),
        w_mkv=jnp.concatenate([flat(w_mem_k[l]), flat(w_mem_v[l])], axis=1),
        w_mo=w_mem_o[l].reshape(-1, d).astype(BF16),
        w_up=w_up[l].astype(BF16), w_down=w_down[l].astype(BF16),
    )


def _tile(n, pref):
    t = min(n, pref)
    while n % t:
        t -= LANE
    return t


def _trunk_layer(x, w, P, l, depth, ckv_stack, tabs, mem_k, mem_v, past, tag):
    b, s, d = x.shape
    m = b * s
    x2 = x.reshape(m, d)
    tm = _tile(m, 1024)
    z = norm_matmul(x2, P["g_pre_mix"][l], w["w_in"], BF16, tm, 2304, f"in_proj_{tag}")
    z3 = z.reshape(b, s, Z_WIDTH)
    s_att = s if past is None else 2 * CHUNK
    if s_att != s:
        z3 = jnp.pad(z3, ((0, 0), (0, s_att - s), (0, 0)))
    q, ckv_stack, kpe_out, k, v = mla_prep(
        z3.reshape(b * s_att, Z_WIDTH), P["g_mla_q"][l], P["g_mla_kv"][l], w["w_uq"], w["w_uk"],
        w["w_uv"], tabs["mla_cos"], tabs["mla_sin"], _tile(b * s_att, 1024), l, depth, ckv_stack,
        f"mla_prep_{tag}")
    q3 = q.reshape(b, s_att, -1)
    real_rows = lambda a: a[:, :s].reshape(m, a.shape[-1])
    if past is None:
        o_a = mla_attention(q3, k.reshape(b, s, -1), v.reshape(b, s, -1), _tile(s, 512),
                            f"mla_attn_{tag}")
        init = jnp.zeros((b, RET_HEADS // 2, LANE, RET_DV), F32)
        y_b, state = retention(z3, tabs["ret"], P["g_ret_gn"][l], init, tabs["ret_blk"],
                               f"retention_{tag}")
        o_c = band_attention(z3, Z_CQ // LANE, z3, Z_CK // LANE, z3, Z_CV // LANE,
                             tabs["band_bias"][l], tabs["band_tq"], True, f"band_{tag}")
        n_real = s
        band_k = z3[:, s - BAND_WINDOW:, Z_CK:Z_CK + 512]
        band_v = z3[:, s - BAND_WINDOW:, Z_CV:Z_CV + 512]
    else:
        c_ckv, c_kpe, s_ret, c_bk, c_bv = past
        n_real = CHUNK
        n_past = c_ckv.shape[1]
        kpe_pad = jnp.pad(c_kpe, ((0, 0), (0, 0), (MLA_NOPE, HEAD_SLAB - MLA_NOPE - MLA_ROPE)))
        per_b = lambda a: a.reshape(b, -1, a.shape[-1])
        o_a = mla_attention_seg(q3, c_ckv, kpe_pad, w["w_uk"], w["w_uv"], per_b(k), per_b(v),
                                n_real, f"mla_attn_{tag}")
        init = s_ret.astype(F32).reshape(b, RET_HEADS // 2, LANE, RET_DV)
        y_b, state = retention(z3, tabs["ret"], P["g_ret_gn"][l], init, s_att, f"retention_{tag}")
        band_k = z3[:, :n_real, Z_CK:Z_CK + 512]
        band_v = z3[:, :n_real, Z_CV:Z_CV + 512]
        span = tabs["band_bias"][l].shape[-2]
        w_band = c_bk.shape[1]
        catb = lambda c, n: jnp.pad(
            jnp.concatenate([c.reshape(b, w_band, -1).astype(BF16), n], axis=1),
            ((0, 0), (0, span - w_band - n_real), (0, 0)))
        o_c = band_attention(z3, Z_CQ // LANE, catb(c_bk, band_k), 0, catb(c_bv, band_v), 0,
                             tabs["band_bias"][l], s_att, False, f"band_{tag}")
    x2 = mix_out(real_rows(o_a), real_rows(y_b), real_rows(o_c), z, P["b_gate"][l],
                 w["w_a"], w["w_b"], w["w_c"], w["w_out"], P["g_post_mix"][l], x2, tm,
                 f"mix_out_{tag}")
    x3 = mem_attention(x2.reshape(b, s, d), mem_k, mem_v, P["g_pre_mem"][l], w["w_mq"], w["w_mo"],
                       P["g_post_mem"][l], _tile(s, 1024), f"mem_attn_{tag}")
    x4 = mlp(x3.reshape(m, d), P["g_pre_ff"][l], w["w_up"], w["w_down"], P["g_post_ff"][l],
             tm, 1024, f"mlp_{tag}")
    new = (kpe_out.reshape(b, s_att, -1)[:, :n_real],
           state.reshape(b, RET_HEADS, RET_DK, RET_DV),
           band_k.astype(F32).reshape(b, -1, BAND_HEADS, BAND_HD),
           band_v.astype(F32).reshape(b, -1, BAND_HEADS, BAND_HD))
    return x4.reshape(b, s, d), ckv_stack, new


def kernel(x_prompt, x_sample, cache_mla_ckv, cache_mla_kpe, state_ret, cache_band_k, cache_band_v, cache_mem_k, cache_mem_v, mem_prompt, g_pre_mix, w_in, g_mla_q, w_mla_uq, g_mla_kv, w_mla_ukv, g_ret_gn, band_rel_bias, w_br_a, w_br_b, w_br_c, b_gate, w_out, g_post_mix, g_pre_mem, g_mem, w_mem_q, w_mem_k, w_mem_v, w_mem_o, g_post_mem, g_pre_ff, w_up, w_down, g_post_ff):
    P = dict(g_pre_mix=g_pre_mix, g_mla_q=g_mla_q, g_mla_kv=g_mla_kv, g_ret_gn=g_ret_gn,
             b_gate=b_gate, g_post_mix=g_post_mix, g_pre_mem=g_pre_mem, g_post_mem=g_post_mem,
             g_pre_ff=g_pre_ff, g_post_ff=g_post_ff)
    depth = w_in.shape[0]
    bp, sp, d = x_prompt.shape
    bs, ss, _ = x_sample.shape
    n_past = cache_mla_ckv.shape[2]
    w_band = cache_band_k.shape[2]
    assert ss == CHUNK and sp % 512 == 0 and sp >= BAND_WINDOW
    s_pad = 2 * CHUNK

    pos_p = jnp.arange(sp)
    pos_s = n_past + jnp.arange(s_pad)
    ret_blk = 256
    band_tq = 256
    cos_p, sin_p = _mla_rope_tables(pos_p)
    cos_s, sin_s = _mla_rope_tables(pos_s)

    span_p = 3 * band_tq
    jj = np.arange(span_p)[:, None]
    ii = np.arange(band_tq)[None, :]
    band_ok = (jj // CHUNK >= ii // CHUNK) & (jj // CHUNK <= ii // CHUNK + BAND_PREV_CHUNKS)
    bias_p = [_band_bias(band_rel_bias[l], band_tq, span_p, 2 * band_tq, band_ok)
              for l in range(depth)]
    span_s = -(-(w_band + CHUNK) // LANE) * LANE
    mask_s = np.broadcast_to(np.arange(span_s)[:, None] < w_band + CHUNK, (span_s, s_pad))
    bias_s = [_band_bias(band_rel_bias[l], s_pad, span_s, w_band, mask_s) for l in range(depth)]

    tabs_p = dict(mla_cos=cos_p, mla_sin=sin_p, ret=_ret_tables(pos_p, ret_blk, ret_blk),
                  ret_blk=ret_blk, band_bias=bias_p, band_tq=band_tq)
    tabs_s = dict(mla_cos=jnp.tile(cos_s, (bs, 1)), mla_sin=jnp.tile(sin_s, (bs, 1)),
                  ret=_ret_tables(pos_s, s_pad, CHUNK), band_bias=bias_s)

    xp = x_prompt
    xs = x_sample
    mem2 = mem_prompt.reshape(-1, d)
    new_p = [[] for _ in range(6)]
    new_s = [[] for _ in range(4)]
    ckv_p = ckv_s = None
    for l in range(depth):
        w = _layer_weights(l, w_in, w_mla_uq, w_mla_ukv, w_br_a, w_br_b, w_br_c, w_out, w_mem_q,
                           w_mem_k, w_mem_v, w_mem_o, w_up, w_down)
        mkv = norm_matmul(mem2, g_mem[l], w["w_mkv"], F32, _tile(mem2.shape[0], 1024), 512,
                          f"mem_kv_{l}")
        e = MEM_HEADS * MEM_HD
        mk = mkv[:, :e].reshape(bp, -1, e)
        mv = mkv[:, e:].reshape(bp, -1, e)
        xp, ckv_p, st_p = _trunk_layer(xp, w, P, l, depth, ckv_p, tabs_p, mk, mv, None, f"p{l}")
        xs, ckv_s, st_s = _trunk_layer(
            xs, w, P, l, depth, ckv_s, tabs_s, cache_mem_k[l].reshape(bs, -1, e),
            cache_mem_v[l].reshape(bs, -1, e),
            (cache_mla_ckv[l], cache_mla_kpe[l], state_ret[l], cache_band_k[l], cache_band_v[l]),
            f"s{l}")
        mem_shape = (bp, -1, MEM_HEADS, MEM_HD)
        for acc, t in zip(new_p, st_p + (mk.reshape(mem_shape), mv.reshape(mem_shape))):
            acc.append(t)
        for acc, t in zip(new_s, st_s):
            acc.append(t)
    stack = lambda ts: jnp.stack(ts, axis=0)
    return (xp, xs,
            ckv_p.reshape(depth, bp, sp, -1),
            stack(new_p[0]), stack(new_p[1]), stack(new_p[2]), stack(new_p[3]),
            stack(new_p[4]), stack(new_p[5]),
            ckv_s.reshape(depth, bs, s_pad, -1)[:, :, :ss],
            stack(new_s[0]), stack(new_s[1]), stack(new_s[2]), stack(new_s[3]))
```

```python
import functools

import numpy as np
import jax
import jax.numpy as jnp
from jax import lax
from jax.experimental import pallas as pl
from jax.experimental.pallas import tpu as pltpu

F32 = jnp.float32
BF16 = jnp.bfloat16

CHUNK = 64
MLA_HEADS = 8
MLA_Q_RANK = 384
MLA_KV_RANK = 256
MLA_NOPE = 64
MLA_ROPE = 32
MLA_V = 64
MLA_SCALE = (MLA_NOPE + MLA_ROPE) ** -0.5
RET_HEADS = 4
RET_DK = 64
RET_DV = 128
BAND_HEADS = 8
BAND_HD = 64
BAND_PREV_CHUNKS = 8
BAND_WINDOW = BAND_PREV_CHUNKS * CHUNK
MAX_REL = 128
MEM_HEADS = 4
MEM_HD = 128
ROPE_THETA = 10000.0
EPS = 1e-6
NEG_INF = -1e30
LOG2E = 1.4426950408889634

LANE = 128
HEAD_SLAB = 128
VMEM_LIMIT = 48 * 1024 * 1024

Z_G = 0
Z_CQ = 3072
Z_CK = 3584
Z_CV = 4096
Z_RV = 4608
Z_RG = 5120
Z_RQ = 5632
Z_RK = 5888
Z_KV = 6144
Z_PE = 6400
Z_Q = 6528
Z_WIDTH = 6912


def _cparams(*sem, **kw):
    return pltpu.CompilerParams(dimension_semantics=sem, vmem_limit_bytes=VMEM_LIMIT, **kw)


def _rms(x, g):
    return x * lax.rsqrt(jnp.mean(x * x, axis=-1, keepdims=True) + EPS) * g


def _dot(a, b):
    return jnp.dot(a, b, preferred_element_type=F32)


def _dot_nt(a, b):
    return lax.dot_general(a, b, (((1,), (1,)), ((), ())), preferred_element_type=F32)


def _dot_tn(a, b):
    return lax.dot_general(a, b, (((0,), (0,)), ((), ())), preferred_element_type=F32)


def _sigmoid(x):
    return 0.5 * jnp.tanh(0.5 * x) + 0.5


def _lane_iota(shape):
    return lax.broadcasted_iota(jnp.int32, shape, len(shape) - 1)


def _norm_matmul_kernel(x_ref, g_ref, w_ref, o_ref, xn_ref):
    @pl.when(pl.program_id(1) == 0)
    def _():
        xn = _rms(x_ref[...].astype(F32), g_ref[...]).astype(BF16)
        xn_ref[...] = xn
        o_ref[...] = _dot(xn, w_ref[...]).astype(o_ref.dtype)

    @pl.when(pl.program_id(1) > 0)
    def _():
        o_ref[...] = _dot(xn_ref[...], w_ref[...]).astype(o_ref.dtype)


def norm_matmul(x, g, w, out_dtype, tm, tn, name):
    m, k = x.shape
    n = w.shape[1]
    assert m % tm == 0 and n % tn == 0, (m, tm, n, tn)
    return pl.pallas_call(
        _norm_matmul_kernel,
        grid=(m // tm, n // tn),
        in_specs=[
            pl.BlockSpec((tm, k), lambda i, j: (i, 0)),
            pl.BlockSpec((1, k), lambda i, j: (0, 0)),
            pl.BlockSpec((k, tn), lambda i, j: (0, j)),
        ],
        out_specs=pl.BlockSpec((tm, tn), lambda i, j: (i, j)),
        out_shape=jax.ShapeDtypeStruct((m, n), out_dtype),
        scratch_shapes=[pltpu.VMEM((tm, k), BF16)],
        compiler_params=_cparams("parallel", "arbitrary"),
        name=name,
    )(x, g.reshape(1, k), w)


def _rope_slab(x, cos, sin, rot):
    first_end, half, period = rot
    width = x.shape[1]
    right = pltpu.roll(x, width - half, 1)
    left = pltpu.roll(x, half, 1)
    partner = jnp.where((_lane_iota(x.shape) & (period - 1)) < first_end, right, left)
    return x * cos + partner * sin


_MLA_ROT = (MLA_NOPE + MLA_ROPE // 2, MLA_ROPE // 2, LANE)
_RET_ROT = (RET_DK // 2, RET_DK // 2, RET_DK)


_Z_MLA = Z_KV
_Z_MLA_W = Z_WIDTH - Z_KV
assert (Z_KV, Z_PE, Z_Q) == (_Z_MLA, _Z_MLA + MLA_KV_RANK, _Z_MLA + MLA_KV_RANK + HEAD_SLAB)
assert _Z_MLA % _Z_MLA_W == 0


def _mla_kv_heads(ckv, kpe, wk_ref, wv_ref, k_ref, v_ref):
    c = ckv.astype(BF16)
    kn = _dot(c, wk_ref[...])
    for h in range(MLA_HEADS):
        sl = slice(h * HEAD_SLAB, (h + 1) * HEAD_SLAB)
        k_ref[:, sl] = (kn[:, sl] + kpe).astype(BF16)
    v = _dot(c, wv_ref[...])
    ones_lane = (_lane_iota(v.shape) & (HEAD_SLAB - 1)) == MLA_V
    v_ref[...] = jnp.where(ones_lane, 1.0, v).astype(BF16)


def _mla_prep_kernel(z_ref, gq_ref, gkv_ref, wq_ref, wk_ref, wv_ref, cos_ref, sin_ref, *refs):
    q_ref, ckv_ref, kpe_out_ref, k_ref, v_ref = refs[-5:]
    cos = cos_ref[...]
    sin = sin_ref[...]
    zkv = z_ref[:, :MLA_KV_RANK]
    zpe = z_ref[:, MLA_KV_RANK:MLA_KV_RANK + HEAD_SLAB]
    zq = z_ref[:, MLA_KV_RANK + HEAD_SLAB:]
    qn = _rms(zq.astype(F32), gq_ref[...]).astype(BF16)
    q = _dot(qn, wq_ref[...])
    cos_q = cos * (MLA_SCALE * LOG2E)
    sin_q = sin * (MLA_SCALE * LOG2E)
    for h in range(MLA_HEADS):
        sl = slice(h * HEAD_SLAB, (h + 1) * HEAD_SLAB)
        pt = slice((MLA_HEADS + h) * HEAD_SLAB, (MLA_HEADS + h + 1) * HEAD_SLAB)
        q_ref[:, sl] = (q[:, sl] * cos_q + q[:, pt] * sin_q).astype(BF16)
    ckv = _rms(zkv.astype(F32), gkv_ref[...])
    ckv_ref[0] = ckv
    for later in range(1, ckv_ref.shape[0]):
        ckv_ref[later] = jnp.zeros_like(ckv)
    kpe = _rope_slab(zpe.astype(F32), cos, sin, _MLA_ROT)
    kpe_out_ref[...] = kpe[:, MLA_NOPE:MLA_NOPE + MLA_ROPE]
    _mla_kv_heads(ckv, kpe, wk_ref, wv_ref, k_ref, v_ref)


def mla_prep(z, g_q, g_kv, w_uq, w_uk, w_uv, cos, sin, tm, layer, depth, ckv_stack, name):
    m = z.shape[0]
    nt = cos.shape[0] // tm
    row = lambda w: pl.BlockSpec((1, w), lambda i: (0, 0))
    const = lambda a: pl.BlockSpec(a.shape, lambda i: (0, 0))
    slabs = MLA_HEADS * HEAD_SLAB
    out = lambda w: pl.BlockSpec((tm, w), lambda i: (i, 0))
    in_specs = [
        pl.BlockSpec((tm, _Z_MLA_W), lambda i: (i, _Z_MLA // _Z_MLA_W)),
        row(MLA_Q_RANK), row(MLA_KV_RANK), const(w_uq), const(w_uk), const(w_uv),
        pl.BlockSpec((tm, HEAD_SLAB), lambda i: (i % nt, 0)),
        pl.BlockSpec((tm, HEAD_SLAB), lambda i: (i % nt, 0)),
    ]
    operands = [z, g_q.reshape(1, -1), g_kv.reshape(1, -1), w_uq, w_uk, w_uv, cos, sin]
    if ckv_stack is None:
        assert layer == 0
        ckv_spec = pl.BlockSpec((depth, tm, MLA_KV_RANK), lambda i: (0, i, 0))
        aliases = {}
    else:
        ckv_spec = pl.BlockSpec((1, tm, MLA_KV_RANK), lambda i: (layer, i, 0))
        aliases = {len(operands): 1}
        in_specs.append(pl.BlockSpec(memory_space=pl.ANY))
        operands.append(ckv_stack)
    return pl.pallas_call(
        _mla_prep_kernel,
        grid=(m // tm,),
        in_specs=in_specs,
        out_specs=[out(slabs), ckv_spec, out(MLA_ROPE), out(slabs), out(slabs)],
        out_shape=[
            jax.ShapeDtypeStruct((m, slabs), BF16),
            jax.ShapeDtypeStruct((depth, m, MLA_KV_RANK), F32),
            jax.ShapeDtypeStruct((m, MLA_ROPE), F32),
            jax.ShapeDtypeStruct((m, slabs), BF16),
            jax.ShapeDtypeStruct((m, slabs), BF16),
        ],
        input_output_aliases=aliases,
        compiler_params=_cparams("parallel"),
        name=name,
    )(*operands)


def _head_of_pair(x, hh):
    lane = _lane_iota(x.shape)
    keep = (lane < BAND_HD) if hh == 0 else (lane >= BAND_HD)
    return jnp.where(keep, x, jnp.zeros_like(x))


MLA_GROUP = 4
BAND_GROUP = 4
_PAIR_SLABS = [slice(hh * HEAD_SLAB, (hh + 1) * HEAD_SLAB) for hh in range(MLA_GROUP)]


def _mla_out(acc):
    return acc[:MLA_V, :] / acc[MLA_V:MLA_V + 1, :]


def _mla_attn_kernel(q_ref, k_ref, v_ref, o_ref, acc_ref, sa_ref, sb_ref, ma_ref, mb_ref, *, t, nq):
    sa_ref, sb_ref = (sa_ref, ma_ref), (sb_ref, mb_ref)
    shift = CHUNK.bit_length() - 1
    key_chunk = jnp.right_shift(lax.broadcasted_iota(jnp.int32, (t, t), 0), shift)
    qry_chunk = jnp.right_shift(lax.broadcasted_iota(jnp.int32, (t, t), 1), shift)
    diag = key_chunk <= qry_chunk

    def rows_of(kb):
        return pl.ds(kb * t if isinstance(kb, int) else pl.multiple_of(kb * t, t), t)

    def scores(qt, kb, buf):
        for hh, sl in enumerate(_PAIR_SLABS):
            s = _dot_nt(k_ref[0, rows_of(kb), sl], q_ref[0, rows_of(qt), sl])
            buf[0][hh, :, :t] = s
            buf[1][hh] = jnp.max(s, axis=0, keepdims=True)

    def tile(qi, carry):
        _mla_attn_tile(qi, nq, scores, k_ref, v_ref, o_ref, acc_ref, sa_ref, sb_ref, diag, rows_of,
                       t)
        return carry

    acc_ref[...] = jnp.zeros_like(acc_ref)
    scores(0, 0, sa_ref)
    lax.fori_loop(0, nq, tile, 0)


def _mla_attn_tile(qi, nq, scores_of, k_ref, v_ref, o_ref, acc_ref, sa_ref, sb_ref, diag, rows_of,
                   t):
    scores = functools.partial(scores_of, qi)
    next_first = lambda: scores_of(jnp.minimum(qi + 1, nq - 1), 0, sa_ref)

    def consume(kb, buf, ms, mask):
        new_m = []
        for hh, sl in enumerate(_PAIR_SLABS):
            s = buf[0][hh, :, :t]
            if mask is None:
                blk_max = buf[1][hh]
            else:
                s = jnp.where(mask, s, NEG_INF)
                blk_max = jnp.max(s, axis=0, keepdims=True)
            m = jnp.maximum(ms[hh], blk_max)
            alpha = jnp.exp2(ms[hh] - m)
            p = jnp.exp2(s - m).astype(BF16)
            acc_ref[hh] = alpha * acc_ref[hh] + _dot_tn(v_ref[0, rows_of(kb), sl], p)
            new_m.append(m)
        return tuple(new_m)

    def finish():
        outs = [_mla_out(acc_ref[hh]) for hh in range(MLA_GROUP)]
        o_ref[0, rows_of(qi), :] = jnp.concatenate(outs, axis=0).T.astype(o_ref.dtype)

    ms = tuple(jnp.full((1, t), NEG_INF, F32) for _ in range(MLA_GROUP))

    def pair(j, ms):
        kb = 2 * j
        scores(kb + 1, sb_ref)
        ms = consume(kb, sa_ref, ms, None)
        scores(kb + 2, sa_ref)
        return consume(kb + 1, sb_ref, ms, None)

    ms = lax.fori_loop(0, qi // 2, pair, ms)

    @pl.when(qi % 2 == 0)
    def _():
        consume(qi, sa_ref, ms, diag)
        next_first()
        finish()

    @pl.when(qi % 2 == 1)
    def _():
        scores(qi, sb_ref)
        ms1 = consume(qi - 1, sa_ref, ms, None)
        next_first()
        consume(qi, sb_ref, ms1, diag)
        finish()


def _mla_attn_seg_kernel(q_ref, ckv1_ref, kpe1_ref, wk_ref, wv_ref, k2_ref, v2_ref, o_ref,
                         k1_ref, v1_ref, *, n2_valid, chunk):
    for r in range(ckv1_ref.shape[1] // chunk):
        rows = pl.ds(r * chunk, chunk)
        _mla_kv_heads(ckv1_ref[0, rows, :], kpe1_ref[0, rows, :], wk_ref, wv_ref,
                      k1_ref.at[rows], v1_ref.at[rows])
    for pair in range(MLA_HEADS // 2):
        outs = []
        for hh in range(2):
            h = 2 * pair + hh
            sl = slice(h * HEAD_SLAB, (h + 1) * HEAD_SLAB)
            q = q_ref[0, :, sl]
            s1 = _dot_nt(k1_ref[:, sl], q)
            s2 = _dot_nt(k2_ref[0, :, sl], q)
            s2 = jnp.where(lax.broadcasted_iota(jnp.int32, s2.shape, 0) < n2_valid, s2, NEG_INF)
            m = jnp.maximum(jnp.max(s1, axis=0, keepdims=True), jnp.max(s2, axis=0, keepdims=True))
            acc = (_dot_tn(v1_ref[:, sl], jnp.exp2(s1 - m).astype(BF16))
                   + _dot_tn(v2_ref[0, :, sl], jnp.exp2(s2 - m).astype(BF16)))
            outs.append(_mla_out(acc))
        o_ref[0, :, pair * LANE:(pair + 1) * LANE] = (
            jnp.concatenate(outs, axis=0).T.astype(o_ref.dtype))


def mla_attention_seg(q, ckv1, kpe1, w_uk, w_uv, k2, v2, n2_valid, name):
    b, sq, _ = q.shape
    n1 = ckv1.shape[1]
    slabs = MLA_HEADS * HEAD_SLAB
    blk = lambda a: pl.BlockSpec((1,) + a.shape[1:], lambda b_: (b_, 0, 0))
    const = lambda a: pl.BlockSpec(a.shape, lambda b_: (0, 0))
    return pl.pallas_call(
        functools.partial(_mla_attn_seg_kernel, n2_valid=n2_valid, chunk=_tile(n1, 512)),
        grid=(b,),
        in_specs=[blk(q), blk(ckv1), blk(kpe1), const(w_uk), const(w_uv), blk(k2), blk(v2)],
        out_specs=pl.BlockSpec((1, sq, MLA_HEADS * MLA_V), lambda b_: (b_, 0, 0)),
        out_shape=jax.ShapeDtypeStruct((b, sq, MLA_HEADS * MLA_V), BF16),
        scratch_shapes=[pltpu.VMEM((n1, slabs), BF16), pltpu.VMEM((n1, slabs), BF16)],
        compiler_params=_cparams("parallel"),
        name=name,
    )(q, ckv1, kpe1, w_uk, w_uv, k2, v2)


def mla_attention(q, k, v, t, name):
    b, s, _ = q.shape
    assert s % t == 0 and k.shape[1] == s
    g = MLA_GROUP
    whole = pl.BlockSpec((1, s, g * HEAD_SLAB), lambda b_, h: (b_, 0, h))
    return pl.pallas_call(
        functools.partial(_mla_attn_kernel, t=t, nq=s // t),
        grid=(b, MLA_HEADS // g),
        in_specs=[whole, whole, whole],
        out_specs=pl.BlockSpec((1, s, g * MLA_V), lambda b_, h: (b_, 0, h)),
        out_shape=jax.ShapeDtypeStruct((b, s, MLA_HEADS * MLA_V), BF16),
        scratch_shapes=[pltpu.VMEM((g, HEAD_SLAB, t), F32), pltpu.VMEM((g, t, t + LANE), F32),
                        pltpu.VMEM((g, t, t + LANE), F32), pltpu.VMEM((g, 1, t), F32),
                        pltpu.VMEM((g, 1, t), F32)],
        compiler_params=_cparams("parallel", "parallel"),
        name=name,
    )(q, k, v)


def _band_attn_kernel(q_ref, k_ref, v_ref, bias_ref, o_ref, sa_ref, sb_ref, ma_ref, mb_ref, *,
                      tq, nq, span, sliding):
    def geom(i):
        if not sliding:
            return 0, span, 0
        if isinstance(i, int):
            n = min(i + 1, span // tq) * tq
            return max(i + 1 - span // tq, 0) * tq, n, span - n
        return pl.multiple_of((i + 1 - span // tq) * tq, tq), span, 0

    def q_rows(i):
        return pl.ds(i * tq if isinstance(i, int) else pl.multiple_of(i * tq, tq), tq)

    sa_ref, sb_ref = (sa_ref, ma_ref), (sb_ref, mb_ref)

    n_pairs = q_ref.shape[-1] // LANE
    pair_lanes = [slice(p * LANE, (p + 1) * LANE) for p in range(n_pairs)]

    def scores(i, buf):
        start, n, boff = geom(i)
        for pr, lanes in enumerate(pair_lanes):
            q = q_ref[0, q_rows(i), lanes]
            k = k_ref[0, pl.ds(start, n), lanes]
            for hh in range(2):
                h = 2 * pr + hh
                s = _dot_nt(k, _head_of_pair(q, hh)) + bias_ref[0, h, boff:boff + n, :]
                buf[0][h, :n, :tq] = s
                buf[1][h] = jnp.max(s, axis=0, keepdims=True)

    def consume(i, buf):
        start, n, _ = geom(i)
        tiles = []
        for pr, lanes in enumerate(pair_lanes):
            v = v_ref[0, pl.ds(start, n), lanes]
            outs = []
            for hh in range(2):
                h = 2 * pr + hh
                p = jnp.exp2(buf[0][h, :n, :tq] - buf[1][h])
                l = jnp.sum(p, axis=0, keepdims=True)
                outs.append(_dot_tn(v, p.astype(BF16)) / l)
            first = lax.broadcasted_iota(jnp.int32, outs[0].shape, 0) < BAND_HD
            tiles.append(jnp.where(first, outs[0], outs[1]))
        o_ref[0, q_rows(i), :] = jnp.concatenate(tiles, axis=0).T.astype(o_ref.dtype)

    scores(0, sa_ref)
    if nq == 1:
        consume(0, sa_ref)
        return
    scores(1, sb_ref)
    consume(0, sa_ref)
    scores(2, sa_ref)
    consume(1, sb_ref)

    def pair(j, carry):
        i = 2 * j + 2
        scores(i + 1, sb_ref)
        consume(i, sa_ref)
        scores(i + 2, sa_ref)
        consume(i + 1, sb_ref)
        return carry

    lax.fori_loop(0, (nq - 2) // 2 - 1, pair, 0)
    scores(nq - 1, sb_ref)
    consume(nq - 2, sa_ref)
    consume(nq - 1, sb_ref)


def band_attention(q, q_col, k, k_col, v, v_col, bias, tq, sliding, name):
    b, sq = q.shape[:2]
    sk = k.shape[1]
    nq = sq // tq
    span = bias.shape[-2]
    assert (sliding and sk == sq and nq >= 4 and nq % 2 == 0) or (nq == 1 and sk == span)
    g = BAND_GROUP
    w = g * BAND_HD
    assert q_col * LANE % w == 0 and k_col * LANE % w == 0 and v_col * LANE % w == 0
    col = lambda c: (lambda hg, b_: (b_, 0, c * LANE // w + hg))
    return pl.pallas_call(
        functools.partial(_band_attn_kernel, tq=tq, nq=nq, span=span, sliding=sliding),
        grid=(BAND_HEADS // g, b),
        in_specs=[
            pl.BlockSpec((1, sq, w), col(q_col)),
            pl.BlockSpec((1, sk, w), col(k_col)),
            pl.BlockSpec((1, sk, w), col(v_col)),
            pl.BlockSpec((1, g, span, tq), lambda hg, b_: (0, hg, 0, 0)),
        ],
        out_specs=pl.BlockSpec((1, sq, w), col(0)),
        out_shape=jax.ShapeDtypeStruct((b, sq, BAND_HEADS * BAND_HD), BF16),
        scratch_shapes=[pltpu.VMEM((g, span, tq + LANE), F32),
                        pltpu.VMEM((g, span, tq + LANE), F32),
                        pltpu.VMEM((g, 1, tq), F32), pltpu.VMEM((g, 1, tq), F32)],
        compiler_params=_cparams("parallel", "arbitrary"),
        name=name,
    )(q, k, v, bias.reshape((1,) + bias.shape))


def _retention_kernel(z_ref, cq_ref, sq_ref, ck_ref, sk_ref,
                      dmat_ref, rowdec_ref, kw_ref, sdec_ref, gn_ref, init_ref,
                      y_ref, state_out_ref, state_ref):
    c = pl.program_id(1)

    @pl.when(c == 0)
    def _():
        state_ref[...] = init_ref[...]

    n_pair = RET_HEADS // 2
    gn = gn_ref[...]
    row_is_first = lax.broadcasted_iota(jnp.int32, (LANE, RET_DV), 0) < RET_DK
    v_w = RET_HEADS * RET_DV
    qk_w = RET_HEADS * RET_DK
    for bi in range(z_ref.shape[0]):
        rq = z_ref[bi, :, 2 * v_w:2 * v_w + qk_w]
        rk = z_ref[bi, :, 2 * v_w + qk_w:]
        q = _rope_slab(rq.astype(F32), cq_ref[...], sq_ref[...], _RET_ROT)
        k = _rope_slab(rk.astype(F32), ck_ref[...], sk_ref[...], _RET_ROT)
        qb = q.astype(BF16)
        kb = k.astype(BF16)
        kwb = (k * kw_ref[...]).astype(BF16)
        for p in range(n_pair):
            psl = slice(p * LANE, (p + 1) * LANE)
            st = state_ref[bi, p]
            stb = st.astype(BF16)
            kv = []
            for hh in range(2):
                h = 2 * p + hh
                vsl = slice(h * RET_DV, (h + 1) * RET_DV)
                v = z_ref[bi, :, vsl]
                qh = _head_of_pair(qb[:, psl], hh)
                s = _dot_nt(qh, kb[:, psl]) * dmat_ref[h]
                o = _dot(s.astype(BF16), v) + _dot(qh, stb) * rowdec_ref[:, vsl]
                mu = jnp.mean(o, axis=-1, keepdims=True)
                d = o - mu
                yn = d * lax.rsqrt(jnp.mean(d * d, axis=-1, keepdims=True) + EPS)
                g = z_ref[bi, :, v_w + h * RET_DV:v_w + (h + 1) * RET_DV].astype(F32)
                y_ref[bi, :, vsl] = (yn * gn[:, vsl] * (g * _sigmoid(g))).astype(y_ref.dtype)
                kv.append(_dot_tn(kwb[:, psl], v))
            state_ref[bi, p] = st * sdec_ref[p] + jnp.where(row_is_first, kv[0], kv[1])

    @pl.when(c == pl.num_programs(1) - 1)
    def _():
        state_out_ref[...] = state_ref[...]


def retention(z, tabs, g_gn, init_state, blk, name):
    b, s = z.shape[:2]
    nc = s // blk
    nb = 2 if b % 2 == 0 else 1
    n_pair = RET_HEADS // 2
    tab = lambda w: pl.BlockSpec((blk, w), lambda b_, c: (c, 0))
    const = lambda a: pl.BlockSpec(a.shape, lambda b_, c: (0,) * a.ndim)
    qk_w = RET_HEADS * RET_DK
    v_w = RET_HEADS * RET_DV
    z_w = 2 * v_w + 2 * qk_w
    assert (Z_RG, Z_RQ, Z_RK) == (Z_RV + v_w, Z_RV + 2 * v_w, Z_RV + 2 * v_w + qk_w)
    assert Z_RV % z_w == 0
    return pl.pallas_call(
        _retention_kernel,
        grid=(b // nb, nc),
        in_specs=[
            pl.BlockSpec((nb, blk, z_w), lambda b_, c: (b_, c, Z_RV // z_w)),
            tab(qk_w), tab(qk_w), tab(qk_w), tab(qk_w),
            const(tabs["dmat"]), const(tabs["rowdec"]), const(tabs["kw"]), const(tabs["sdec"]),
            pl.BlockSpec((1, v_w), lambda b_, c: (0, 0)),
            pl.BlockSpec((nb, n_pair, LANE, RET_DV), lambda b_, c: (b_, 0, 0, 0)),
        ],
        out_specs=[
            pl.BlockSpec((nb, blk, v_w), lambda b_, c: (b_, c, 0)),
            pl.BlockSpec((nb, n_pair, LANE, RET_DV), lambda b_, c: (b_, 0, 0, 0)),
        ],
        out_shape=[
            jax.ShapeDtypeStruct((b, s, v_w), BF16),
            jax.ShapeDtypeStruct((b, n_pair, LANE, RET_DV), F32),
        ],
        scratch_shapes=[pltpu.VMEM((nb, n_pair, LANE, RET_DV), F32)],
        compiler_params=_cparams("parallel", "arbitrary"),
        name=name,
    )(z, tabs["cos_q"], tabs["sin_q"], tabs["cos_k"], tabs["sin_k"],
      tabs["dmat"], tabs["rowdec"], tabs["kw"], tabs["sdec"], g_gn.reshape(1, -1), init_state)


def _mix_out_kernel(oa_ref, yb_ref, oc_ref, zg_ref, bg_ref, wa_ref, wb_ref,
                    wc_ref, wo_ref, g_ref, x_ref, o_ref):
    d = x_ref.shape[-1]
    sub = min(x_ref.shape[0], 512)
    for r in range(x_ref.shape[0] // sub):
        rows = slice(r * sub, (r + 1) * sub)
        merged = None
        for n, (br_ref, w_ref) in enumerate(((oa_ref, wa_ref), (yb_ref, wb_ref), (oc_ref, wc_ref))):
            cols = slice(n * d, (n + 1) * d)
            gate = _sigmoid(zg_ref[rows, cols].astype(F32) + bg_ref[:, cols])
            term = gate * _dot(br_ref[rows, :], w_ref[...])
            merged = term if merged is None else merged + term
        y = _dot(merged.astype(BF16), wo_ref[...])
        o_ref[rows, :] = x_ref[rows, :] + _rms(y, g_ref[...])


def mix_out(o_a, y_b, o_c, z, b_gate, w_a, w_b, w_c, w_out, g_post, x, tm, name):
    m, d = x.shape
    e = o_a.shape[1]
    act = pl.BlockSpec((tm, e), lambda i: (i, 0))
    assert Z_G % (3 * d) == 0
    const = lambda a: pl.BlockSpec(a.shape, lambda i: (0, 0))
    return pl.pallas_call(
        _mix_out_kernel,
        grid=(m // tm,),
        in_specs=[act, act, act, pl.BlockSpec((tm, 3 * d), lambda i: (i, Z_G // (3 * d))),
                  pl.BlockSpec((1, 3 * d), lambda i: (0, 0)),
                  const(w_a), const(w_b), const(w_c), const(w_out),
                  pl.BlockSpec((1, d), lambda i: (0, 0)),
                  pl.BlockSpec((tm, d), lambda i: (i, 0))],
        out_specs=pl.BlockSpec((tm, d), lambda i: (i, 0)),
        out_shape=jax.ShapeDtypeStruct((m, d), F32),
        compiler_params=_cparams("parallel"),
        name=name,
    )(o_a, y_b, o_c, z, b_gate.reshape(1, -1), w_a, w_b, w_c, w_out,
      g_post.reshape(1, -1), x)


def _mem_attn_kernel(x_ref, mk_ref, mv_ref, gpre_ref, wq_ref, wo_ref, gpost_ref, o_ref, *, sub):
    mk = mk_ref[0].astype(BF16)
    mv = mv_ref[0].astype(BF16)
    for r in range(x_ref.shape[1] // sub):
        rows = slice(r * sub, (r + 1) * sub)
        x = x_ref[0, rows, :]
        u = _rms(x, gpre_ref[...]).astype(BF16)
        q = _dot(u, wq_ref[...]).astype(BF16)
        outs = []
        for h in range(MEM_HEADS):
            sl = slice(h * MEM_HD, (h + 1) * MEM_HD)
            s = _dot_nt(q[:, sl], mk[:, sl]) * (MEM_HD ** -0.5 * LOG2E)
            m = jnp.max(s, axis=-1, keepdims=True)
            p = jnp.exp2(s - m)
            l = jnp.sum(p, axis=-1, keepdims=True)
            outs.append(_dot(p.astype(BF16), mv[:, sl]) / l)
        o = jnp.concatenate(outs, axis=-1).astype(BF16)
        o_ref[0, rows, :] = x + _rms(_dot(o, wo_ref[...]), gpost_ref[...])


def mem_attention(x, mk, mv, g_pre, w_q, w_o, g_post, tm, name):
    b, s, d = x.shape
    const = lambda a: pl.BlockSpec(a.shape, lambda b_, i: (0, 0))
    vec = pl.BlockSpec((1, d), lambda b_, i: (0, 0))
    mem = pl.BlockSpec((1,) + mk.shape[1:], lambda b_, i: (b_, 0, 0))
    return pl.pallas_call(
        functools.partial(_mem_attn_kernel, sub=min(tm, 512)),
        grid=(b, s // tm),
        in_specs=[pl.BlockSpec((1, tm, d), lambda b_, i: (b_, i, 0)), mem, mem,
                  vec, const(w_q), const(w_o), vec],
        out_specs=pl.BlockSpec((1, tm, d), lambda b_, i: (b_, i, 0)),
        out_shape=jax.ShapeDtypeStruct((b, s, d), F32),
        compiler_params=_cparams("parallel", "parallel"),
        name=name,
    )(x, mk, mv, g_pre.reshape(1, -1), w_q, w_o, g_post.reshape(1, -1))


def _mlp_kernel(x_ref, gpre_ref, wu_ref, wd_ref, gpost_ref, o_ref, xn_ref, acc_ref):
    j = pl.program_id(1)
    last = pl.num_programs(1) - 1

    def partial_out(xn):
        h = jnp.square(jnp.maximum(_dot(xn, wu_ref[...]), 0.0))
        return _dot(h.astype(BF16), wd_ref[...])

    @pl.when(j == 0)
    def _():
        xn = _rms(x_ref[...], gpre_ref[...]).astype(BF16)
        xn_ref[...] = xn
        acc_ref[...] = partial_out(xn)

    @pl.when(jnp.logical_and(j > 0, j < last))
    def _():
        acc_ref[...] += partial_out(xn_ref[...])

    @pl.when(j == last)
    def _():
        y = acc_ref[...] + partial_out(xn_ref[...])
        o_ref[...] = x_ref[...] + _rms(y, gpost_ref[...])


def mlp(x, g_pre, w_up, w_down, g_post, tm, tf, name):
    m, d = x.shape
    f = w_up.shape[1]
    vec = pl.BlockSpec((1, d), lambda i, j: (0, 0))
    return pl.pallas_call(
        _mlp_kernel,
        grid=(m // tm, f // tf),
        in_specs=[pl.BlockSpec((tm, d), lambda i, j: (i, 0)), vec,
                  pl.BlockSpec((d, tf), lambda i, j: (0, j)),
                  pl.BlockSpec((tf, d), lambda i, j: (j, 0)), vec],
        out_specs=pl.BlockSpec((tm, d), lambda i, j: (i, 0)),
        out_shape=jax.ShapeDtypeStruct((m, d), F32),
        scratch_shapes=[pltpu.VMEM((tm, d), BF16), pltpu.VMEM((tm, d), F32)],
        compiler_params=_cparams("parallel", "arbitrary"),
        name=name,
    )(x, g_pre.reshape(1, -1), w_up, w_down, g_post.reshape(1, -1))


def _mem_mlp_kernel(x_ref, mk_ref, mv_ref, gpm_ref, wq_ref, wo_ref, gqm_ref, gpre_ref, wu_ref,
                    wd_ref, gpost_ref, o_ref, xn_ref, acc_ref, xres_ref):
    j = pl.program_id(1)
    last = pl.num_programs(1) - 1
    sub = min(x_ref.shape[0], 512)

    def partial_out(xn):
        h = jnp.square(jnp.maximum(_dot(xn, wu_ref[...]), 0.0))
        return _dot(h.astype(BF16), wd_ref[...])

    @pl.when(j == 0)
    def _():
        mk = mk_ref[0].astype(BF16)
        mv = mv_ref[0].astype(BF16)
        for r in range(x_ref.shape[0] // sub):
            rows = slice(r * sub, (r + 1) * sub)
            x = x_ref[rows, :]
            q = _dot(_rms(x, gpm_ref[...]).astype(BF16), wq_ref[...]).astype(BF16)
            outs = []
            for h in range(MEM_HEADS):
                sl = slice(h * MEM_HD, (h + 1) * MEM_HD)
                s = _dot_nt(q[:, sl], mk[:, sl]) * (MEM_HD ** -0.5 * LOG2E)
                p = jnp.exp2(s - jnp.max(s, axis=-1, keepdims=True))
                l = jnp.sum(p, axis=-1, keepdims=True)
                outs.append(_dot(p.astype(BF16), mv[:, sl]) / l)
            o = jnp.concatenate(outs, axis=-1).astype(BF16)
            x3 = x + _rms(_dot(o, wo_ref[...]), gqm_ref[...])
            xres_ref[rows, :] = x3
            xn_ref[rows, :] = _rms(x3, gpre_ref[...]).astype(BF16)
        acc_ref[...] = partial_out(xn_ref[...])

    @pl.when(jnp.logical_and(j > 0, j < last))
    def _():
        acc_ref[...] += partial_out(xn_ref[...])

    @pl.when(j == last)
    def _():
        y = acc_ref[...] + partial_out(xn_ref[...])
        o_ref[...] = xres_ref[...] + _rms(y, gpost_ref[...])


def mem_mlp(x, rows_per_batch, mk, mv, g_pre_mem, w_q, w_o, g_post_mem, g_pre, w_up, w_down,
            g_post, tm, tf, name):
    m, d = x.shape
    f = w_up.shape[1]
    assert rows_per_batch % tm == 0 and f // tf >= 2
    per_b = rows_per_batch // tm
    vec = pl.BlockSpec((1, d), lambda i, j: (0, 0))
    const = lambda a: pl.BlockSpec(a.shape, lambda i, j: (0, 0))
    mem = pl.BlockSpec((1,) + mk.shape[1:], lambda i, j: (i // per_b, 0, 0))
    return pl.pallas_call(
        _mem_mlp_kernel,
        grid=(m // tm, f // tf),
        in_specs=[pl.BlockSpec((tm, d), lambda i, j: (i, 0)), mem, mem, vec, const(w_q),
                  const(w_o), vec, vec,
                  pl.BlockSpec((d, tf), lambda i, j: (0, j)),
                  pl.BlockSpec((tf, d), lambda i, j: (j, 0)), vec],
        out_specs=pl.BlockSpec((tm, d), lambda i, j: (i, 0)),
        out_shape=jax.ShapeDtypeStruct((m, d), F32),
        scratch_shapes=[pltpu.VMEM((tm, d), BF16), pltpu.VMEM((tm, d), F32),
                        pltpu.VMEM((tm, d), F32)],
        compiler_params=_cparams("parallel", "arbitrary"),
        name=name,
    )(x, mk, mv, g_pre_mem.reshape(1, -1), w_q, w_o, g_post_mem.reshape(1, -1),
      g_pre.reshape(1, -1), w_up, w_down, g_post.reshape(1, -1))


def _rope_angles(pos, half):
    inv = ROPE_THETA ** (-jnp.arange(half, dtype=F32) / half)
    ang = pos.astype(F32)[:, None] * inv[None, :]
    return jnp.cos(ang), jnp.sin(ang)


def _mla_rope_tables(pos):
    cos, sin = _rope_angles(pos, MLA_ROPE // 2)
    t = pos.shape[0]
    one = jnp.ones((t, MLA_NOPE), F32)
    zero64 = jnp.zeros((t, MLA_NOPE), F32)
    pad = jnp.zeros((t, HEAD_SLAB - MLA_NOPE - MLA_ROPE), F32)
    return (jnp.concatenate([one, cos, cos, pad], axis=1),
            jnp.concatenate([zero64, -sin, sin, pad], axis=1))


def _ret_tables(pos, blk, n_real):
    cos, sin = _rope_angles(pos, RET_DK // 2)
    cos_q = jnp.tile(jnp.concatenate([cos, cos], axis=1), (1, RET_HEADS))
    sin_q = jnp.tile(jnp.concatenate([-sin, sin], axis=1), (1, RET_HEADS))
    k_scale = RET_DK ** -0.5
    log_g = jnp.log1p(-jnp.exp2(-5.0 - jnp.arange(RET_HEADS, dtype=F32)))
    idx = jnp.arange(blk, dtype=F32)
    diff = idx[:, None] - idx[None, :]
    dmat = jnp.where(diff >= 0, jnp.exp(log_g[:, None, None] * jnp.maximum(diff, 0.0)), 0.0)
    rowdec = jnp.exp(log_g[None, :] * (idx[:, None] + 1.0))
    w = jnp.where(idx[:, None] < n_real,
                  jnp.exp(log_g[None, :] * jnp.maximum(n_real - 1.0 - idx[:, None], 0.0)), 0.0)
    sdec = jnp.exp(log_g * n_real)
    n_pair = RET_HEADS // 2
    return dict(
        cos_q=cos_q, sin_q=sin_q, cos_k=cos_q * k_scale, sin_k=sin_q * k_scale,
        dmat=dmat,
        rowdec=jnp.repeat(rowdec, RET_DV, axis=1),
        kw=jnp.repeat(w, RET_DK, axis=1),
        sdec=jnp.broadcast_to(jnp.repeat(sdec, RET_DK).reshape(n_pair, LANE, 1),
                              (n_pair, LANE, RET_DV)),
    )


def _band_bias(table, tq, span, q_off, allowed):
    sub = 8
    assert span % (sub * sub) == 0
    length = tq + span - 1
    d = np.arange(length)
    ext = table[:, np.clip(d + q_off - span + 1, -MAX_REL, MAX_REL) + MAX_REL].astype(F32) * LOG2E
    rows, r_now = ext[:, None, :], 1
    for f in (sub, sub, span // (sub * sub)):
        width = rows.shape[-1] - (f - 1) * r_now
        rows = jnp.concatenate([rows[:, :, (f - 1 - a) * r_now:(f - 1 - a) * r_now + width]
                                for a in range(f)], axis=1)
        r_now *= f
    return jnp.where(allowed[None], rows[:, :, :tq], NEG_INF)


def _layer_weights(l, w_in, w_mla_uq, w_mla_ukv, w_br_a, w_br_b, w_br_c, w_out, w_mem_q, w_mem_k,
                   w_mem_v, w_mem_o, w_up, w_down):
    d = w_in.shape[1]
    parts, start = [], 0
    for n in (MLA_Q_RANK, MLA_KV_RANK, MLA_ROPE, 256, 256, 512, 512, 512, 512, 512, 3 * d):
        parts.append(w_in[l, :, start:start + n])
        start += n
    zq, zkv, zpe, rq, rk, rv, rg, cq, ck, cv, zg = parts
    cq = cq * (BAND_HD ** -0.5 * LOG2E)
    zeros = lambda n: jnp.zeros((d, n), w_in.dtype)
    w_in_l = jnp.concatenate(
        [zg, cq, ck, cv, rv, rg, rq, rk, zkv, zeros(MLA_NOPE), zpe,
         zeros(HEAD_SLAB - MLA_NOPE - MLA_ROPE), zq], axis=1).astype(BF16)
    assert w_in_l.shape[1] == Z_WIDTH
    pad_head = lambda w: jnp.pad(w, ((0, 0), (0, 0), (0, HEAD_SLAB - w.shape[-1])))
    flat = lambda w: w.reshape(w.shape[0], -1).astype(BF16)
    return dict(
        w_in=w_in_l,
        w_uq=jnp.concatenate([flat(pad_head(w_mla_uq[l])), flat(pad_head(jnp.concatenate(
            [jnp.zeros_like(w_mla_uq[l][..., :MLA_NOPE]),
             w_mla_uq[l][..., MLA_NOPE + MLA_ROPE // 2:],
             w_mla_uq[l][..., MLA_NOPE:MLA_NOPE + MLA_ROPE // 2]], axis=-1)))], axis=1),
        w_uk=flat(pad_head(w_mla_ukv[l][..., :MLA_NOPE])),
        w_uv=flat(pad_head(w_mla_ukv[l][..., MLA_NOPE:])),
        w_a=w_br_a[l].astype(BF16), w_b=w_br_b[l].astype(BF16), w_c=w_br_c[l].astype(BF16),
        w_out=w_out[l].astype(BF16),
        w_mq=flat(w_mem_q[l]),
        w_mkv=jnp.concatenate([flat(w_mem_k[l]), flat(w_mem_v[l])], axis=1),
        w_mo=w_mem_o[l].reshape(-1, d).astype(BF16),
        w_up=w_up[l].astype(BF16), w_down=w_down[l].astype(BF16),
    )


def _tile(n, pref):
    t = min(n, pref)
    while n % t:
        t -= LANE
    return t


def _trunk_layer(x, w, P, l, depth, ckv_stack, tabs, mem_k, mem_v, past, tag):
    b, s, d = x.shape
    m = b * s
    x2 = x.reshape(m, d)
    tm = _tile(m, 1024)
    z = norm_matmul(x2, P["g_pre_mix"][l], w["w_in"], BF16, tm, 2304, f"in_proj_{tag}")
    z3 = z.reshape(b, s, Z_WIDTH)
    s_att = s if past is None else 2 * CHUNK
    if s_att != s:
        z3 = jnp.pad(z3, ((0, 0), (0, s_att - s), (0, 0)))
    q, ckv_stack, kpe_out, k, v = mla_prep(
        z3.reshape(b * s_att, Z_WIDTH), P["g_mla_q"][l], P["g_mla_kv"][l], w["w_uq"], w["w_uk"],
        w["w_uv"], tabs["mla_cos"], tabs["mla_sin"], _tile(b * s_att, 1024), l, depth, ckv_stack,
        f"mla_prep_{tag}")
    q3 = q.reshape(b, s_att, -1)
    real_rows = lambda a: a[:, :s].reshape(m, a.shape[-1])
    if past is None:
        o_a = mla_attention(q3, k.reshape(b, s, -1), v.reshape(b, s, -1), _tile(s, 512),
                            f"mla_attn_{tag}")
        init = jnp.zeros((b, RET_HEADS // 2, LANE, RET_DV), F32)
        y_b, state = retention(z3, tabs["ret"], P["g_ret_gn"][l], init, tabs["ret_blk"],
                               f"retention_{tag}")
        o_c = band_attention(z3, Z_CQ // LANE, z3, Z_CK // LANE, z3, Z_CV // LANE,
                             tabs["band_bias"][l], tabs["band_tq"], True, f"band_{tag}")
        n_real = s
        band_k = z3[:, s - BAND_WINDOW:, Z_CK:Z_CK + 512]
        band_v = z3[:, s - BAND_WINDOW:, Z_CV:Z_CV + 512]
    else:
        c_ckv, c_kpe, s_ret, c_bk, c_bv = past
        n_real = CHUNK
        n_past = c_ckv.shape[1]
        kpe_pad = jnp.pad(c_kpe, ((0, 0), (0, 0), (MLA_NOPE, HEAD_SLAB - MLA_NOPE - MLA_ROPE)))
        per_b = lambda a: a.reshape(b, -1, a.shape[-1])
        o_a = mla_attention_seg(q3, c_ckv, kpe_pad, w["w_uk"], w["w_uv"], per_b(k), per_b(v),
                                n_real, f"mla_attn_{tag}")
        init = s_ret.astype(F32).reshape(b, RET_HEADS // 2, LANE, RET_DV)
        y_b, state = retention(z3, tabs["ret"], P["g_ret_gn"][l], init, s_att, f"retention_{tag}")
        band_k = z3[:, :n_real, Z_CK:Z_CK + 512]
        band_v = z3[:, :n_real, Z_CV:Z_CV + 512]
        span = tabs["band_bias"][l].shape[-2]
        w_band = c_bk.shape[1]
        catb = lambda c, n: jnp.pad(
            jnp.concatenate([c.reshape(b, w_band, -1).astype(BF16), n], axis=1),
            ((0, 0), (0, span - w_band - n_real), (0, 0)))
        o_c = band_attention(z3, Z_CQ // LANE, catb(c_bk, band_k), 0, catb(c_bv, band_v), 0,
                             tabs["band_bias"][l], s_att, False, f"band_{tag}")
    x2 = mix_out(real_rows(o_a), real_rows(y_b), real_rows(o_c), z, P["b_gate"][l],
                 w["w_a"], w["w_b"], w["w_c"], w["w_out"], P["g_post_mix"][l], x2, tm,
                 f"mix_out_{tag}")
    if s % tm == 0:
        x4 = mem_mlp(x2, s, mem_k, mem_v, P["g_pre_mem"][l], w["w_mq"], w["w_mo"],
                     P["g_post_mem"][l], P["g_pre_ff"][l], w["w_up"], w["w_down"],
                     P["g_post_ff"][l], tm, 1024, f"mem_mlp_{tag}")
    else:
        x3 = mem_attention(x2.reshape(b, s, d), mem_k, mem_v, P["g_pre_mem"][l], w["w_mq"],
                           w["w_mo"], P["g_post_mem"][l], _tile(s, 1024), f"mem_attn_{tag}")
        x4 = mlp(x3.reshape(m, d), P["g_pre_ff"][l], w["w_up"], w["w_down"], P["g_post_ff"][l],
                 tm, 1024, f"mlp_{tag}")
    new = (kpe_out.reshape(b, s_att, -1)[:, :n_real],
           state.reshape(b, RET_HEADS, RET_DK, RET_DV),
           band_k.astype(F32).reshape(b, -1, BAND_HEADS, BAND_HD),
           band_v.astype(F32).reshape(b, -1, BAND_HEADS, BAND_HD))
    return x4.reshape(b, s, d), ckv_stack, new


def kernel(x_prompt, x_sample, cache_mla_ckv, cache_mla_kpe, state_ret, cache_band_k, cache_band_v, cache_mem_k, cache_mem_v, mem_prompt, g_pre_mix, w_in, g_mla_q, w_mla_uq, g_mla_kv, w_mla_ukv, g_ret_gn, band_rel_bias, w_br_a, w_br_b, w_br_c, b_gate, w_out, g_post_mix, g_pre_mem, g_mem, w_mem_q, w_mem_k, w_mem_v, w_mem_o, g_post_mem, g_pre_ff, w_up, w_down, g_post_ff):
    P = dict(g_pre_mix=g_pre_mix, g_mla_q=g_mla_q, g_mla_kv=g_mla_kv, g_ret_gn=g_ret_gn,
             b_gate=b_gate, g_post_mix=g_post_mix, g_pre_mem=g_pre_mem, g_post_mem=g_post_mem,
             g_pre_ff=g_pre_ff, g_post_ff=g_post_ff)
    depth = w_in.shape[0]
    bp, sp, d = x_prompt.shape
    bs, ss, _ = x_sample.shape
    n_past = cache_mla_ckv.shape[2]
    w_band = cache_band_k.shape[2]
    assert ss == CHUNK and sp % 512 == 0 and sp >= BAND_WINDOW
    s_pad = 2 * CHUNK

    pos_p = jnp.arange(sp)
    pos_s = n_past + jnp.arange(s_pad)
    ret_blk = 256
    band_tq = 256
    cos_p, sin_p = _mla_rope_tables(pos_p)
    cos_s, sin_s = _mla_rope_tables(pos_s)

    span_p = 3 * band_tq
    jj = np.arange(span_p)[:, None]
    ii = np.arange(band_tq)[None, :]
    band_ok = (jj // CHUNK >= ii // CHUNK) & (jj // CHUNK <= ii // CHUNK + BAND_PREV_CHUNKS)
    bias_p = [_band_bias(band_rel_bias[l], band_tq, span_p, 2 * band_tq, band_ok)
              for l in range(depth)]
    span_s = -(-(w_band + CHUNK) // LANE) * LANE
    mask_s = np.broadcast_to(np.arange(span_s)[:, None] < w_band + CHUNK, (span_s, s_pad))
    bias_s = [_band_bias(band_rel_bias[l], s_pad, span_s, w_band, mask_s) for l in range(depth)]

    tabs_p = dict(mla_cos=cos_p, mla_sin=sin_p, ret=_ret_tables(pos_p, ret_blk, ret_blk),
                  ret_blk=ret_blk, band_bias=bias_p, band_tq=band_tq)
    tabs_s = dict(mla_cos=jnp.tile(cos_s, (bs, 1)), mla_sin=jnp.tile(sin_s, (bs, 1)),
                  ret=_ret_tables(pos_s, s_pad, CHUNK), band_bias=bias_s)

    xp = x_prompt
    xs = x_sample
    mem2 = mem_prompt.reshape(-1, d)
    new_p = [[] for _ in range(6)]
    new_s = [[] for _ in range(4)]
    ckv_p = ckv_s = None
    for l in range(depth):
        w = _layer_weights(l, w_in, w_mla_uq, w_mla_ukv, w_br_a, w_br_b, w_br_c, w_out, w_mem_q,
                           w_mem_k, w_mem_v, w_mem_o, w_up, w_down)
        mkv = norm_matmul(mem2, g_mem[l], w["w_mkv"], F32, _tile(mem2.shape[0], 1024), 512,
                          f"mem_kv_{l}")
        e = MEM_HEADS * MEM_HD
        mk = mkv[:, :e].reshape(bp, -1, e)
        mv = mkv[:, e:].reshape(bp, -1, e)
        xp, ckv_p, st_p = _trunk_layer(xp, w, P, l, depth, ckv_p, tabs_p, mk, mv, None, f"p{l}")
        xs, ckv_s, st_s = _trunk_layer(
            xs, w, P, l, depth, ckv_s, tabs_s, cache_mem_k[l].reshape(bs, -1, e),
            cache_mem_v[l].reshape(bs, -1, e),
            (cache_mla_ckv[l], cache_mla_kpe[l], state_ret[l], cache_band_k[l], cache_band_v[l]),
            f"s{l}")
        mem_shape = (bp, -1, MEM_HEADS, MEM_HD)
        for acc, t in zip(new_p, st_p + (mk.reshape(mem_shape), mv.reshape(mem_shape))):
            acc.append(t)
        for acc, t in zip(new_s, st_s):
            acc.append(t)
    stack = lambda ts: jnp.stack(ts, axis=0)
    return (xp, xs,
            ckv_p.reshape(depth, bp, sp, -1),
            stack(new_p[0]), stack(new_p[1]), stack(new_p[2]), stack(new_p[3]),
            stack(new_p[4]), stack(new_p[5]),
            ckv_s.reshape(depth, bs, s_pad, -1)[:, :, :ss],
            stack(new_s[0]), stack(new_s[1]), stack(new_s[2]), stack(new_s[3]))
```
